```python
import math
import jax, jax.numpy as jnp
from jax import lax
import numpy as np

D_MODEL = 1024
BATCH = 8
SEQ = 4096
DEPTH = 1

HEAD_DIM = 64
ROPE_DIM = HEAD_DIM // 4
ROPE_THETA = 500000.0
DIFF_HEADS = 4
DIFF_V_DIM = 2 * HEAD_DIM
NSA_HEADS = 8
NSA_KV_GROUPS = 2
NSA_GQA = NSA_HEADS // NSA_KV_GROUPS
CMP_BLOCK = 32
CMP_STRIDE = 16
CMP_HIDDEN = 2 * HEAD_DIM
SLC_BLOCK = 64
SLC_TOPK = 16
WINDOW = 512
D_MIX = DIFF_HEADS * DIFF_V_DIM + NSA_HEADS * HEAD_DIM
D_IN = (2 * DIFF_HEADS * 2 * HEAD_DIM + DIFF_HEADS * DIFF_V_DIM
        + NSA_HEADS * HEAD_DIM + 6 * NSA_KV_GROUPS * HEAD_DIM + 3 * NSA_HEADS)
D_FF = 2816
CONV_WIDTH = 3
Q_BLOCK = 128
LN_EPS = 1e-5
RMS_EPS = 1e-5
NEG_INF = -1e30
FORCED_SCORE = 1e6
DEEPNORM_ALPHA = (2 * DEPTH) ** 0.25
DEEPNORM_BETA = (8 * DEPTH) ** -0.25

kernel_name = "hymba_diff_nsa_convglu_deepnorm"

f32 = jnp.float32


def layer_norm(x, g, b):
    xf = x.astype(f32)
    mu = jnp.mean(xf, axis=-1, keepdims=True)
    var = jnp.mean(jnp.square(xf - mu), axis=-1, keepdims=True)
    return ((xf - mu) * lax.rsqrt(var + LN_EPS) * g.astype(f32) + b.astype(f32)).astype(x.dtype)


def rope_partial(x, pos):
    half = ROPE_DIM // 2
    inv_freq = 1.0 / (ROPE_THETA ** (jnp.arange(half, dtype=f32) / half))
    ang = pos.astype(f32)[..., None] * inv_freq
    ang = ang.reshape(ang.shape[:2] + (1,) * (x.ndim - 3) + (half,))
    cos = jnp.cos(ang).astype(x.dtype)
    sin = jnp.sin(ang).astype(x.dtype)
    x1 = x[..., :half]
    x2 = x[..., half:ROPE_DIM]
    return jnp.concatenate([x1 * cos - x2 * sin, x2 * cos + x1 * sin, x[..., ROPE_DIM:]], axis=-1)


def compress(x, pe, w1, b1, w2):
    B, S, G, d = x.shape
    n_chunk = S // CMP_STRIDE
    r = CMP_BLOCK // CMP_STRIDE
    n_cmp = n_chunk - r + 1
    chunks = x.reshape(B, n_chunk, CMP_STRIDE, G, d)
    blocks = jnp.concatenate([chunks[:, i:i + n_cmp] for i in range(r)], axis=2)
    blocks = blocks + pe[None, None, :, None, :]
    flat = blocks.transpose(0, 1, 3, 2, 4).reshape(B, n_cmp, G, CMP_BLOCK * d)
    hid = jax.nn.gelu(flat @ w1 + b1)
    return hid @ w2


def cmp_to_slc_overlap(n_cmp, n_slc):
    cs = np.arange(n_cmp)[:, None] * CMP_STRIDE
    ss = np.arange(n_slc)[None, :] * SLC_BLOCK
    ov = np.clip(np.minimum(cs + CMP_BLOCK, ss + SLC_BLOCK) - np.maximum(cs, ss), 0, None)
    return jnp.asarray(ov / CMP_BLOCK, dtype=f32)


def masked_softmax(s, mask):
    s = jnp.where(mask, s.astype(f32), NEG_INF)
    return jnp.where(mask, jax.nn.softmax(s, axis=-1), 0.0)


def hybrid_mixer(h, positions, w_in, lq1, lk1, lq2, lk2, diff_g, lam_init,
                 pe_k, w1_k, b1_k, w2_k, pe_v, w1_v, b1_v, w2_v):
    B, S, _ = h.shape
    G, R, d = NSA_KV_GROUPS, NSA_GQA, HEAD_DIM
    scale = HEAD_DIM ** -0.5
    proj = h @ w_in
    sizes = [DIFF_HEADS * 2 * d, DIFF_HEADS * 2 * d, DIFF_HEADS * DIFF_V_DIM, NSA_HEADS * d] + [G * d] * 6 + [NSA_HEADS * 3]
    splits = np.cumsum(sizes)[:-1].tolist()
    q_d, k_d, v_d, q_n, k_c, v_c, k_s, v_s, k_w, v_w, g_n = jnp.split(proj, splits, axis=-1)

    q_d = rope_partial(q_d.reshape(B, S, DIFF_HEADS, 2, d), positions)
    k_d = rope_partial(k_d.reshape(B, S, DIFF_HEADS, 2, d), positions)
    v_d = v_d.reshape(B, S, DIFF_HEADS, DIFF_V_DIM)
    lam = (jnp.exp(jnp.sum(lq1.astype(f32) * lk1.astype(f32)))
           - jnp.exp(jnp.sum(lq2.astype(f32) * lk2.astype(f32))) + lam_init)

    q_n = rope_partial(q_n.reshape(B, S, G, R, d), positions)
    r_c = CMP_BLOCK // CMP_STRIDE
    n_cmp = S // CMP_STRIDE - r_c + 1
    cmp_pos = positions[:, CMP_BLOCK - 1::CMP_STRIDE][:, :n_cmp]
    k_cmp = rope_partial(compress(k_c.reshape(B, S, G, d), pe_k, w1_k, b1_k, w2_k), cmp_pos)
    v_cmp = compress(v_c.reshape(B, S, G, d), pe_v, w1_v, b1_v, w2_v)
    cmp_end = jnp.arange(n_cmp) * CMP_STRIDE + CMP_BLOCK - 1
    n_slc = S // SLC_BLOCK
    top_k = min(SLC_TOPK, n_slc)
    overlap = cmp_to_slc_overlap(n_cmp, n_slc)
    k_sel = rope_partial(k_s.reshape(B, S, G, d), positions)
    kb = k_sel.reshape(B, n_slc, SLC_BLOCK, G, d).transpose(0, 3, 1, 2, 4)
    vb = v_s.reshape(B, n_slc, SLC_BLOCK, G, d).transpose(0, 3, 1, 2, 4)
    pad = ((0, 0), (WINDOW, 0), (0, 0), (0, 0))
    k_win = jnp.pad(rope_partial(k_w.reshape(B, S, G, d), positions), pad)
    v_win = jnp.pad(v_w.reshape(B, S, G, d), pad)
    gates = jax.nn.sigmoid(g_n).reshape(B, S, G, R, 3)
    bi = jnp.arange(B)[:, None, None, None]
    gi = jnp.arange(G)[None, :, None, None]
    blk = jnp.arange(n_slc)
    key_idx = jnp.arange(S)

    def block_fn(i):
        s0 = i * Q_BLOCK
        t = s0 + jnp.arange(Q_BLOCK)
        qd = lax.dynamic_slice_in_dim(q_d, s0, Q_BLOCK, axis=1)
        s = jnp.einsum('bqhcd,bkhcd->bhcqk', qd, k_d) * scale
        p = masked_softmax(s, t[:, None] >= key_idx[None, :])
        a = p[:, :, 0] - lam * p[:, :, 1]
        od = jnp.einsum('bhqk,bkhe->bqhe', a.astype(v_d.dtype), v_d).astype(f32)
        od = od * lax.rsqrt(jnp.mean(od * od, axis=-1, keepdims=True) + RMS_EPS)
        od = (od * diff_g.astype(f32) * (1.0 - lam_init)).astype(h.dtype).reshape(B, Q_BLOCK, DIFF_HEADS * DIFF_V_DIM)

        qn = lax.dynamic_slice_in_dim(q_n, s0, Q_BLOCK, axis=1)
        s_c = jnp.einsum('bqgrd,bngd->bgrqn', qn, k_cmp) * scale
        p_c = masked_softmax(s_c, cmp_end[None, :] <= t[:, None])
        o_c = jnp.einsum('bgrqn,bngd->bqgrd', p_c.astype(v_cmp.dtype), v_cmp)

        imp = jnp.sum(p_c, axis=2) @ overlap
        cur = t // SLC_BLOCK
        forced = (blk[None, :] == 0) | (blk[None, :] == cur[:, None]) | (blk[None, :] == cur[:, None] - 1)
        imp = jnp.where(blk[None, :] > cur[:, None], -1.0, imp)
        imp = jnp.where(forced, FORCED_SCORE, imp)
        _, idx = lax.top_k(imp, top_k)
        ks = kb[bi, gi, idx].reshape(B, G, Q_BLOCK, top_k * SLC_BLOCK, d)
        vs = vb[bi, gi, idx].reshape(B, G, Q_BLOCK, top_k * SLC_BLOCK, d)
        kpos = (idx[..., None] * SLC_BLOCK + jnp.arange(SLC_BLOCK)).reshape(B, G, Q_BLOCK, top_k * SLC_BLOCK)
        s_s = jnp.einsum('bqgrd,bgqnd->bgrqn', qn, ks) * scale
        p_s = masked_softmax(s_s, (kpos <= t[None, None, :, None])[:, :, None])
        o_s = jnp.einsum('bgrqn,bgqnd->bqgrd', p_s.astype(vs.dtype), vs)

        kw = lax.dynamic_slice_in_dim(k_win, s0, WINDOW + Q_BLOCK, axis=1)
        vw = lax.dynamic_slice_in_dim(v_win, s0, WINDOW + Q_BLOCK, axis=1)
        wpos = s0 - WINDOW + jnp.arange(WINDOW + Q_BLOCK)
        wmask = (wpos[None, :] >= 0) & (wpos[None, :] <= t[:, None]) & (wpos[None, :] > t[:, None] - WINDOW)
        s_w = jnp.einsum('bqgrd,bkgd->bgrqk', qn, kw) * scale
        p_w = masked_softmax(s_w, wmask)
        o_w = jnp.einsum('bgrqk,bkgd->bqgrd', p_w.astype(vw.dtype), vw)

        gb = lax.dynamic_slice_in_dim(gates, s0, Q_BLOCK, axis=1)
        on = gb[..., 0:1] * o_c + gb[..., 1:2] * o_s + gb[..., 2:3] * o_w
        on = on.astype(h.dtype).reshape(B, Q_BLOCK, NSA_HEADS * d)
        return jnp.concatenate([od, on], axis=-1)

    out = lax.map(block_fn, jnp.arange(S // Q_BLOCK))
    return out.transpose(1, 0, 2, 3).reshape(B, S, D_MIX)


def conv_glu_ffn(h, w_up, conv_w, conv_b, w_down):
    S = h.shape[1]
    u = h @ w_up
    up = jnp.pad(u, ((0, 0), (CONV_WIDTH - 1, 0), (0, 0)))
    u = sum(conv_w[k] * up[:, k:k + S] for k in range(CONV_WIDTH)) + conv_b
    gate, val = jnp.split(u, 2, axis=-1)
    return (jax.nn.silu(gate) * val) @ w_down


def setup_inputs(seed: int = 0) -> dict:
    key = jax.random.key(seed)
    ks = jax.random.split(key, 26)
    L, d, G = DEPTH, HEAD_DIM, NSA_KV_GROUPS
    beta = DEEPNORM_BETA
    nrm = lambda k, shp: jax.random.normal(k, shp, f32)
    x = nrm(ks[0], (BATCH, SEQ, D_MODEL))
    positions = jnp.tile(jnp.arange(SEQ, dtype=jnp.int32)[None, :], (BATCH, 1))
    col_scale = jnp.concatenate([
        jnp.ones(2 * DIFF_HEADS * 2 * d, f32), jnp.full(DIFF_HEADS * DIFF_V_DIM, beta, f32),
        jnp.ones(NSA_HEADS * d, f32),
        jnp.tile(jnp.concatenate([jnp.ones(G * d, f32), jnp.full(G * d, beta, f32)]), 3),
        jnp.ones(3 * NSA_HEADS, f32)])
    w_in = nrm(ks[1], (L, D_MODEL, D_IN)) * (D_MODEL ** -0.5) * col_scale
    lambda_q1 = 0.1 * nrm(ks[2], (L, d))
    lambda_k1 = 0.1 * nrm(ks[3], (L, d))
    lambda_q2 = 0.1 * nrm(ks[4], (L, d))
    lambda_k2 = 0.1 * nrm(ks[5], (L, d))
    diff_norm_g = 1.0 + 0.02 * nrm(ks[6], (L, DIFF_V_DIM))
    cmp_pe_k = 0.02 * nrm(ks[7], (L, CMP_BLOCK, d))
    cmp_w1_k = nrm(ks[8], (L, CMP_BLOCK * d, CMP_HIDDEN)) * (CMP_BLOCK * d) ** -0.5
    cmp_b1_k = 0.02 * nrm(ks[9], (L, CMP_HIDDEN))
    cmp_w2_k = nrm(ks[10], (L, CMP_HIDDEN, d)) * CMP_HIDDEN ** -0.5
    cmp_pe_v = 0.02 * nrm(ks[11], (L, CMP_BLOCK, d))
    cmp_w1_v = nrm(ks[12], (L, CMP_BLOCK * d, CMP_HIDDEN)) * (CMP_BLOCK * d) ** -0.5
    cmp_b1_v = 0.02 * nrm(ks[13], (L, CMP_HIDDEN))
    cmp_w2_v = nrm(ks[14], (L, CMP_HIDDEN, d)) * CMP_HIDDEN ** -0.5
    w_out = nrm(ks[15], (L, D_MIX, D_MODEL)) * (D_MIX ** -0.5) * beta
    ln1_g = 1.0 + 0.02 * nrm(ks[16], (L, D_MODEL))
    ln1_b = 0.02 * nrm(ks[17], (L, D_MODEL))
    w_up = nrm(ks[18], (L, D_MODEL, 2 * D_FF)) * (D_MODEL ** -0.5) * beta
    conv_w = nrm(ks[19], (L, CONV_WIDTH, 2 * D_FF)) * CONV_WIDTH ** -0.5
    conv_b = 0.02 * nrm(ks[20], (L, 2 * D_FF))
    w_down = nrm(ks[21], (L, D_FF, D_MODEL)) * (D_FF ** -0.5) * beta
    ln2_g = 1.0 + 0.02 * nrm(ks[22], (L, D_MODEL))
    ln2_b = 0.02 * nrm(ks[23], (L, D_MODEL))
    return {"x": x, "positions": positions, "w_in": w_in,
            "lambda_q1": lambda_q1, "lambda_k1": lambda_k1, "lambda_q2": lambda_q2, "lambda_k2": lambda_k2,
            "diff_norm_g": diff_norm_g,
            "cmp_pe_k": cmp_pe_k, "cmp_w1_k": cmp_w1_k, "cmp_b1_k": cmp_b1_k, "cmp_w2_k": cmp_w2_k,
            "cmp_pe_v": cmp_pe_v, "cmp_w1_v": cmp_w1_v, "cmp_b1_v": cmp_b1_v, "cmp_w2_v": cmp_w2_v,
            "w_out": w_out, "ln1_g": ln1_g, "ln1_b": ln1_b,
            "w_up": w_up, "conv_w": conv_w, "conv_b": conv_b, "w_down": w_down,
            "ln2_g": ln2_g, "ln2_b": ln2_b}


def reference(x, positions, w_in, lambda_q1, lambda_k1, lambda_q2, lambda_k2, diff_norm_g,
              cmp_pe_k, cmp_w1_k, cmp_b1_k, cmp_w2_k, cmp_pe_v, cmp_w1_v, cmp_b1_v, cmp_w2_v,
              w_out, ln1_g, ln1_b, w_up, conv_w, conv_b, w_down, ln2_g, ln2_b):
    h = x
    for l in range(DEPTH):
        lam_init = 0.8 - 0.6 * math.exp(-0.3 * l)
        mix = hybrid_mixer(h, positions, w_in[l], lambda_q1[l], lambda_k1[l], lambda_q2[l], lambda_k2[l],
                           diff_norm_g[l], lam_init,
                           cmp_pe_k[l], cmp_w1_k[l], cmp_b1_k[l], cmp_w2_k[l],
                           cmp_pe_v[l], cmp_w1_v[l], cmp_b1_v[l], cmp_w2_v[l])
        h = layer_norm(DEEPNORM_ALPHA * h + mix @ w_out[l], ln1_g[l], ln1_b[l])
        h = layer_norm(DEEPNORM_ALPHA * h + conv_glu_ffn(h, w_up[l], conv_w[l], conv_b[l], w_down[l]),
                       ln2_g[l], ln2_b[l])
    return h
```

```python
import functools
import math

import jax
import jax.numpy as jnp
import numpy as np
from jax import lax
from jax.experimental import pallas as pl
from jax.experimental.pallas import tpu as pltpu

f32 = jnp.float32
bf16 = jnp.bfloat16

LANES = 128
HEAD_DIM = 64
ROPE_DIM = HEAD_DIM // 4
ROPE_THETA = 500000.0
DIFF_HEADS = 4
NSA_HEADS = 8
NSA_KV_GROUPS = 2
NSA_GQA = NSA_HEADS // NSA_KV_GROUPS
CMP_BLOCK = 32
CMP_STRIDE = 16
CMP_HIDDEN = 2 * HEAD_DIM
SLC_BLOCK = 64
SLC_TOPK = 16
WINDOW = 512
CONV_WIDTH = 3
LN_EPS = 1e-5
RMS_EPS = 1e-5
NEG_INF = -1e30
SEL_BIAS = -1e9
FORCED_SCORE = 1e6
VMEM_LIMIT = 56 * 1024 * 1024

PROJ_ROWS = 512
ATTN_TILE = 512
NSA_Q_TILE = 256
NSA_K_TILE = 512
DENSE_ROWS = 512
FFN_CHUNK = 256
HALO = 8


def _cparams(sem):
    return pltpu.CompilerParams(dimension_semantics=sem, vmem_limit_bytes=VMEM_LIMIT)


def _const_spec(shape):
    n = len(shape)
    return pl.BlockSpec(shape, lambda *_: (0,) * n)


def _layer_norm(y, g, b):
    mu = jnp.mean(y, axis=-1, keepdims=True)
    d = y - mu
    var = jnp.mean(d * d, axis=-1, keepdims=True)
    return d * lax.rsqrt(var + LN_EPS) * g + b


def _rope_consts():
    lane = np.arange(LANES)
    in_head = lane % HEAD_DIM
    half = ROPE_DIM // 2
    inv_freq = 1.0 / (ROPE_THETA ** (jnp.arange(half, dtype=f32) / half))
    c = jnp.zeros((8, LANES), f32)
    c = c.at[0].set(jnp.tile(inv_freq, LANES // half))
    c = c.at[1].set(jnp.asarray(in_head < ROPE_DIM, f32))
    c = c.at[2].set(jnp.asarray(np.where(in_head < half, -1.0, np.where(in_head < ROPE_DIM, 1.0, 0.0)), f32))
    c = c.at[3].set(jnp.asarray(in_head < half, f32))
    return c


def _rope_tables(pos_col, c_ref):
    ang = pos_col.astype(f32) * c_ref[0:1, :]
    cos_t = jnp.where(c_ref[1:2, :] > 0.0, jnp.cos(ang), 1.0)
    sin_t = jnp.sin(ang) * c_ref[2:3, :]
    return cos_t, sin_t, c_ref[3:4, :] > 0.0


def _rope(y, tables):
    cos_t, sin_t, first = tables
    half = ROPE_DIM // 2
    partner = jnp.where(first, pltpu.roll(y, LANES - half, 1), pltpu.roll(y, half, 1))
    return y * cos_t + partner * sin_t


def _proj_kernel(x_ref, pos_ref, w_ref, c_ref,
                 qd_ref, kd_ref, vd_ref, qn_ref, kc_ref, vc_ref, ks_ref, vs_ref, kw_ref, vw_ref, g_ref,
                 *, seq):
    tm = x_ref.shape[0]
    xb = x_ref[...].astype(bf16)
    tables = _rope_tables(pos_ref[...], c_ref)
    lane = lax.broadcasted_iota(jnp.int32, (tm, LANES), 1)
    low = lane < HEAD_DIM
    scale = HEAD_DIM ** -0.5

    def seg(col, width):
        return jnp.dot(xb, w_ref[:, col:col + width], preferred_element_type=f32)

    hd2 = DIFF_HEADS * 2 * HEAD_DIM
    y = seg(0, hd2)
    for c in range(hd2 // LANES):
        sl = slice(c * LANES, (c + 1) * LANES)
        qd_ref[:, sl] = (_rope(y[:, sl], tables) * scale).astype(bf16)
    y = seg(hd2, hd2)
    for c in range(hd2 // LANES):
        sl = slice(c * LANES, (c + 1) * LANES)
        kd_ref[:, sl] = _rope(y[:, sl], tables).astype(bf16)
    vd_ref[...] = seg(2 * hd2, hd2).astype(bf16)

    col = 3 * hd2
    y = seg(col, NSA_HEADS * HEAD_DIM)
    for c in range(NSA_HEADS // 2):
        slab = _rope(y[:, c * LANES:(c + 1) * LANES], tables) * scale
        qn_ref[:, (2 * c) * LANES:(2 * c + 1) * LANES] = jnp.where(low, slab, 0.0).astype(bf16)
        qn_ref[:, (2 * c + 1) * LANES:(2 * c + 2) * LANES] = jnp.where(
            low, pltpu.roll(slab, HEAD_DIM, 1), 0.0).astype(bf16)
    col += NSA_HEADS * HEAD_DIM

    kc_ref[...] = seg(col, LANES)
    vc_ref[...] = seg(col + LANES, LANES)

    row = lax.broadcasted_iota(jnp.int32, (tm, LANES), 0) + lax.rem(pl.program_id(0) * tm, seq)
    onehot = jnp.where(lane - HEAD_DIM == row // SLC_BLOCK, 1.0, 0.0)
    y = _rope(seg(col + 2 * LANES, LANES), tables)
    ks_ref[0] = jnp.where(low, y, onehot).astype(bf16)
    ks_ref[1] = jnp.where(low, pltpu.roll(y, HEAD_DIM, 1), onehot).astype(bf16)
    vs_ref[...] = seg(col + 3 * LANES, LANES).astype(bf16)
    y = _rope(seg(col + 4 * LANES, LANES), tables)
    kw_ref[0] = jnp.where(low, y, 0.0).astype(bf16)
    kw_ref[1] = jnp.where(low, pltpu.roll(y, HEAD_DIM, 1), 0.0).astype(bf16)
    vw_ref[...] = seg(col + 5 * LANES, LANES).astype(bf16)

    gate = jax.nn.sigmoid(seg(col + 6 * LANES, LANES))
    g_ref[0] = gate
    g_ref[1] = pltpu.roll(gate, LANES - NSA_GQA * 3, 1)


def _proj(x2, pos_col, w_pad, consts, seq):
    n, d_model = x2.shape
    tm = PROJ_ROWS
    hd2 = DIFF_HEADS * 2 * HEAD_DIM
    row_spec = lambda w: pl.BlockSpec((tm, w), lambda i: (i, 0))
    grp_spec = pl.BlockSpec((NSA_KV_GROUPS, tm, LANES), lambda i: (0, i, 0))
    out_shape = (
        jax.ShapeDtypeStruct((n, hd2), bf16), jax.ShapeDtypeStruct((n, hd2), bf16),
        jax.ShapeDtypeStruct((n, hd2), bf16), jax.ShapeDtypeStruct((n, NSA_HEADS * LANES), bf16),
        jax.ShapeDtypeStruct((n, LANES), f32), jax.ShapeDtypeStruct((n, LANES), f32),
        jax.ShapeDtypeStruct((NSA_KV_GROUPS, n, LANES), bf16), jax.ShapeDtypeStruct((n, LANES), bf16),
        jax.ShapeDtypeStruct((NSA_KV_GROUPS, n, LANES), bf16), jax.ShapeDtypeStruct((n, LANES), bf16),
        jax.ShapeDtypeStruct((NSA_KV_GROUPS, n, LANES), f32),
    )
    out_specs = (row_spec(hd2), row_spec(hd2), row_spec(hd2), row_spec(NSA_HEADS * LANES),
                 row_spec(LANES), row_spec(LANES), grp_spec, row_spec(LANES), grp_spec, row_spec(LANES),
                 grp_spec)
    return pl.pallas_call(
        functools.partial(_proj_kernel, seq=seq),
        grid=(n // tm,),
        in_specs=[row_spec(d_model), pl.BlockSpec((tm, 1), lambda i: (i, 0)),
                  _const_spec(w_pad.shape), _const_spec(consts.shape)],
        out_specs=out_specs, out_shape=out_shape,
        compiler_params=_cparams(("parallel",)), name="proj",
    )(x2, pos_col, w_pad, consts)


def _compress_one(c_ref, pea_ref, peb_ref, w1a_ref, w1b_ref, b1_ref, w2_ref):
    ck = c_ref[0]
    n = ck.shape[0]
    hid_a = jnp.dot((ck + pea_ref[...]).astype(bf16), w1a_ref[...], preferred_element_type=f32)
    hid_b = jnp.dot((ck + peb_ref[...]).astype(bf16), w1b_ref[...], preferred_element_type=f32)
    hid = hid_a + pltpu.roll(hid_b, n - 1, 0) + b1_ref[...]
    hid = jax.nn.gelu(hid)
    return jnp.dot(hid.astype(bf16), w2_ref[...], preferred_element_type=f32)


def _compress_kernel(kc_ref, vc_ref, pos_ref, c_ref,
                     pak_ref, pbk_ref, w1ak_ref, w1bk_ref, b1k_ref, w2k_ref,
                     pav_ref, pbv_ref, w1av_ref, w1bv_ref, b1v_ref, w2v_ref,
                     kcmp_ref, vcmp_ref):
    k = _compress_one(kc_ref, pak_ref, pbk_ref, w1ak_ref, w1bk_ref, b1k_ref, w2k_ref)
    k = _rope(k, _rope_tables(pos_ref[0], c_ref))
    low = lax.broadcasted_iota(jnp.int32, k.shape, 1) < HEAD_DIM
    kcmp_ref[0, 0] = jnp.where(low, k, 0.0).astype(bf16)
    kcmp_ref[0, 1] = jnp.where(low, pltpu.roll(k, HEAD_DIM, 1), 0.0).astype(bf16)
    v = _compress_one(vc_ref, pav_ref, pbv_ref, w1av_ref, w1bv_ref, b1v_ref, w2v_ref)
    vcmp_ref[0] = v.astype(bf16)


def _compress_weights(pe, w1, b1, w2):
    r = CMP_BLOCK // CMP_STRIDE
    assert r == 2
    eye = jnp.eye(NSA_KV_GROUPS, dtype=f32)
    w1r = w1.reshape(CMP_BLOCK, HEAD_DIM, CMP_HIDDEN)
    per = CMP_STRIDE * NSA_KV_GROUPS * HEAD_DIM

    def big(part):
        return jnp.einsum('tcm,gh->tgchm', part, eye).reshape(per, NSA_KV_GROUPS * CMP_HIDDEN).astype(bf16)

    def pe_row(part):
        return jnp.broadcast_to(part[:, None, :], (CMP_STRIDE, NSA_KV_GROUPS, HEAD_DIM)).reshape(1, per)

    w2b = jnp.einsum('mc,gh->gmhc', w2, eye).reshape(NSA_KV_GROUPS * CMP_HIDDEN, NSA_KV_GROUPS * HEAD_DIM)
    return (pe_row(pe[:CMP_STRIDE]), pe_row(pe[CMP_STRIDE:]), big(w1r[:CMP_STRIDE]), big(w1r[CMP_STRIDE:]),
            jnp.tile(b1, NSA_KV_GROUPS).reshape(1, -1), w2b.astype(bf16))


def _compress(kc, vc, cmp_pos, consts, wk, wv):
    b, nch, per = kc.shape
    seq_spec = pl.BlockSpec((1, nch, per), lambda i: (i, 0, 0))
    w_specs = [_const_spec(w.shape) for w in wk + wv]
    return pl.pallas_call(
        _compress_kernel,
        grid=(b,),
        in_specs=[seq_spec, seq_spec, pl.BlockSpec((1, nch, 1), lambda i: (i, 0, 0)),
                  _const_spec(consts.shape)] + w_specs,
        out_specs=(pl.BlockSpec((1, NSA_KV_GROUPS, nch, LANES), lambda i: (i, 0, 0, 0)),
                   pl.BlockSpec((1, nch, LANES), lambda i: (i, 0, 0))),
        out_shape=(jax.ShapeDtypeStruct((b, NSA_KV_GROUPS, nch, LANES), bf16),
                   jax.ShapeDtypeStruct((b, nch, LANES), bf16)),
        compiler_params=_cparams(("parallel",)), name="compress",
    )(kc, vc, cmp_pos, consts, *wk, *wv)


def _softmax_init(m_ref, l_ref, acc_ref):
    m_ref[...] = jnp.full(m_ref.shape, NEG_INF, f32)
    l_ref[...] = jnp.zeros(l_ref.shape, f32)
    acc_ref[...] = jnp.zeros(acc_ref.shape, f32)


def _softmax_step(s, v, m_ref, l_ref, acc_ref):
    m_prev = m_ref[...]
    m_new = jnp.maximum(m_prev, jnp.max(s, axis=1, keepdims=True))
    alpha = jnp.exp(m_prev - m_new)
    p = jnp.exp(s - m_new)
    l_ref[...] = alpha * l_ref[...] + jnp.sum(p, axis=1, keepdims=True)
    acc_ref[...] = alpha * acc_ref[...] + jnp.dot(p.astype(bf16), v, preferred_element_type=f32)
    m_ref[...] = m_new


def _scores(q, k):
    return lax.dot_general(q, k, (((1,), (1,)), ((), ())), preferred_element_type=f32)


def _diff_kernel(q_ref, k_ref, v_ref, lam_ref, g_ref, o_ref, m_ref, l_ref, acc_ref, *, lam_init):
    tq = q_ref.shape[1]
    tk = tq
    i = pl.program_id(2)
    q = q_ref[0]
    low = lax.broadcasted_iota(jnp.int32, q.shape, 1) < HEAD_DIM
    zero = jnp.zeros_like(q)
    q2 = jnp.concatenate([jnp.where(low, q, zero), jnp.where(low, zero, q)], axis=0)
    _softmax_init(m_ref, l_ref, acc_ref)

    def tile(j, masked):
        off = pl.multiple_of(j * tk, tk)
        s = _scores(q2, k_ref[0, pl.ds(off, tk), :])
        if masked:
            r = lax.rem(lax.broadcasted_iota(jnp.int32, (2 * tq, tk), 0), tq)
            c = lax.broadcasted_iota(jnp.int32, (2 * tq, tk), 1)
            s = jnp.where(r >= c, s, NEG_INF)
        _softmax_step(s, v_ref[0, pl.ds(off, tk), :], m_ref, l_ref, acc_ref)

    def body(j, carry):
        tile(j, False)
        return carry

    lax.fori_loop(0, i, body, 0)
    tile(i, True)

    lam_v = lam_ref[...]
    lam = (jnp.exp(jnp.sum(lam_v[0:1] * lam_v[1:2], axis=1, keepdims=True))
           - jnp.exp(jnp.sum(lam_v[2:3] * lam_v[3:4], axis=1, keepdims=True)) + lam_init)
    o = acc_ref[...] / l_ref[...]
    od = o[:tq] - lam * o[tq:]
    od = od * lax.rsqrt(jnp.mean(od * od, axis=-1, keepdims=True) + RMS_EPS)
    o_ref[0] = (od * g_ref[...] * (1.0 - lam_init)).astype(o_ref.dtype)


def _diff_attention(qd, kd, vd, lam_vec, diff_g, lam_init):
    b, s, _ = qd.shape
    tq = min(ATTN_TILE, s)
    vdim = vd.shape[2] // DIFF_HEADS
    return pl.pallas_call(
        functools.partial(_diff_kernel, lam_init=lam_init),
        grid=(b, DIFF_HEADS, s // tq),
        in_specs=[pl.BlockSpec((1, tq, LANES), lambda bi, h, i: (bi, i, h)),
                  pl.BlockSpec((1, s, LANES), lambda bi, h, i: (bi, 0, h)),
                  pl.BlockSpec((1, s, vdim), lambda bi, h, i: (bi, 0, h)),
                  _const_spec(lam_vec.shape), _const_spec(diff_g.shape)],
        out_specs=pl.BlockSpec((1, tq, vdim), lambda bi, h, i: (bi, i, h)),
        out_shape=jax.ShapeDtypeStruct((b, s, DIFF_HEADS * vdim), bf16),
        scratch_shapes=[pltpu.VMEM((2 * tq, 1), f32), pltpu.VMEM((2 * tq, 1), f32),
                        pltpu.VMEM((2 * tq, vdim), f32)],
        compiler_params=_cparams(("parallel", "parallel", "arbitrary")), name="diff_attn",
    )(qd, kd, vd, lam_vec, diff_g)


def _nsa_kernel(q_ref, kc_ref, vc_ref, ks_ref, vs_ref, kw_ref, vw_ref, g_ref, ov_ref, o_ref,
                m_ref, l_ref, acc_ref, oc_ref, *, top_k):
    tq = q_ref.shape[1]
    seq = ks_ref.shape[2]
    tk = min(NSA_K_TILE, seq)
    ncp = kc_ref.shape[2]
    nslc = ov_ref.shape[0]
    rep = NSA_GQA
    grp = pl.program_id(1)
    i = pl.program_id(2)
    s0 = i * tq

    q0 = jnp.concatenate([q_ref[0, :, r * LANES:(r + 1) * LANES] for r in range(rep)], axis=0)

    t_col = s0 + lax.broadcasted_iota(jnp.int32, (tq, ncp), 0)
    cmp_end = lax.broadcasted_iota(jnp.int32, (tq, ncp), 1) * CMP_STRIDE + (CMP_BLOCK - 1)
    cmp_ok = cmp_end <= t_col
    kc = kc_ref[0, 0]
    vc = vc_ref[0]
    psum = jnp.zeros((tq, ncp), f32)
    for r in range(rep):
        s = jnp.where(cmp_ok, _scores(q0[r * tq:(r + 1) * tq], kc), NEG_INF)
        p = jnp.where(cmp_ok, jnp.exp(s - jnp.max(s, axis=1, keepdims=True)), 0.0)
        den = jnp.sum(p, axis=1, keepdims=True)
        p = p / jnp.where(den > 0.0, den, 1.0)
        psum = psum + p
        oc_ref[r * tq:(r + 1) * tq, :] = jnp.dot(p.astype(bf16), vc, preferred_element_type=f32)

    p_hi = psum.astype(bf16)
    p_lo = (psum - p_hi.astype(f32)).astype(bf16)
    ov = ov_ref[...]
    imp = _scores(ov, p_hi) + _scores(ov, p_lo)
    blk = lax.broadcasted_iota(jnp.int32, (nslc, tq), 0)
    cur = (s0 + lax.broadcasted_iota(jnp.int32, (nslc, tq), 1)) // SLC_BLOCK
    imp = jnp.where(blk > cur, -1.0, imp)
    imp = jnp.where((blk == 0) | (blk == cur) | (blk == cur - 1), FORCED_SCORE, imp)
    rank = jnp.zeros((nslc, tq), f32)
    for jp in range(nslc):
        other = imp[jp:jp + 1, :]
        ahead = jnp.where(blk > jp, jnp.where(other >= imp, 1.0, 0.0), jnp.where(other > imp, 1.0, 0.0))
        rank = rank + ahead
    bias_t = jnp.where(rank < top_k, 0.0, SEL_BIAS)
    pieces = [jnp.zeros((HEAD_DIM, tq), f32), bias_t]
    if nslc < LANES - HEAD_DIM:
        pieces.append(jnp.zeros((LANES - HEAD_DIM - nslc, tq), f32))
    bias = jnp.concatenate(pieces, axis=0).T.astype(bf16)
    qa = q0 + jnp.concatenate([bias] * rep, axis=0)

    rows = rep * tq
    row_t = s0 + lax.rem(lax.broadcasted_iota(jnp.int32, (rows, tk), 0), tq)

    _softmax_init(m_ref, l_ref, acc_ref)
    n_full = s0 // tk

    def sel_tile(j, masked):
        off = pl.multiple_of(j * tk, tk)
        s = _scores(qa, ks_ref[0, 0, pl.ds(off, tk), :])
        if masked:
            key = off + lax.broadcasted_iota(jnp.int32, (rows, tk), 1)
            s = jnp.where(key <= row_t, s, NEG_INF)
        _softmax_step(s, vs_ref[0, pl.ds(off, tk), :], m_ref, l_ref, acc_ref)

    def sel_body(j, carry):
        sel_tile(j, False)
        return carry

    lax.fori_loop(0, n_full, sel_body, 0)
    sel_tile(n_full, True)
    o_sel = acc_ref[...] / l_ref[...]

    _softmax_init(m_ref, l_ref, acc_ref)
    wt = tq
    row_w = s0 + lax.rem(lax.broadcasted_iota(jnp.int32, (rows, wt), 0), tq)

    def win_tile(j):
        off = pl.multiple_of(j * wt, wt)
        s = _scores(q0, kw_ref[0, 0, pl.ds(off, wt), :])
        key = off + lax.broadcasted_iota(jnp.int32, (rows, wt), 1)
        s = jnp.where((key <= row_w) & (key > row_w - WINDOW), s, NEG_INF)
        _softmax_step(s, vw_ref[0, pl.ds(off, wt), :], m_ref, l_ref, acc_ref)

    def win_body(j, carry):
        win_tile(j)
        return carry

    lax.fori_loop(jnp.maximum(i - WINDOW // wt, 0), i + 1, win_body, 0)
    o_win = acc_ref[...] / l_ref[...]

    gates = g_ref[0, 0]
    lane =lax.broadcasted_iota(jnp.int32, (tq, LANES), 1)
    placed = []
    for r in range(rep):
        rs = slice(r * tq, (r + 1) * tq)
        on = (gates[:, 3 * r:3 * r + 1] * oc_ref[rs, :] + gates[:, 3 * r + 1:3 * r + 2] * o_sel[rs]
              + gates[:, 3 * r + 2:3 * r + 3] * o_win[rs])
        placed.append(jnp.where(grp == r % 2, on, pltpu.roll(on, HEAD_DIM, 1)))
    for c in range(rep // 2):
        o_ref[0, :, c * LANES:(c + 1) * LANES] = jnp.where(
            lane < HEAD_DIM, placed[2 * c], placed[2 * c + 1]).astype(o_ref.dtype)


def _nsa_attention(qn, kcmp, vcmp, ksa, vs, kwa, vw, gates, overlap_t, top_k):
    b, s, _ = qn.shape
    tq = min(NSA_Q_TILE, s)
    ncp = kcmp.shape[2]
    rep = NSA_GQA
    assert WINDOW % tq == 0 and min(NSA_K_TILE, s) % tq == 0
    return pl.pallas_call(
        functools.partial(_nsa_kernel, top_k=top_k),
        grid=(b, NSA_KV_GROUPS, s // tq),
        in_specs=[pl.BlockSpec((1, tq, rep * LANES), lambda bi, g, i: (bi, i, g)),
                  pl.BlockSpec((1, 1, ncp, LANES), lambda bi, g, i: (bi, g, 0, 0)),
                  pl.BlockSpec((1, ncp, LANES), lambda bi, g, i: (bi, 0, 0)),
                  pl.BlockSpec((1, 1, s, LANES), lambda bi, g, i: (g, bi, 0, 0)),
                  pl.BlockSpec((1, s, LANES), lambda bi, g, i: (bi, 0, 0)),
                  pl.BlockSpec((1, 1, s, LANES), lambda bi, g, i: (g, bi, 0, 0)),
                  pl.BlockSpec((1, s, LANES), lambda bi, g, i: (bi, 0, 0)),
                  pl.BlockSpec((1, 1, tq, LANES), lambda bi, g, i: (g, bi, i, 0)),
                  _const_spec(overlap_t.shape)],
        out_specs=pl.BlockSpec((1, tq, rep * HEAD_DIM), lambda bi, g, i: (bi, i, g)),
        out_shape=jax.ShapeDtypeStruct((b, s, NSA_HEADS * HEAD_DIM), bf16),
        scratch_shapes=[pltpu.VMEM((rep * tq, 1), f32), pltpu.VMEM((rep * tq, 1), f32),
                        pltpu.VMEM((rep * tq, LANES), f32), pltpu.VMEM((rep * tq, LANES), f32)],
        compiler_params=_cparams(("parallel", "parallel", "arbitrary")), name="nsa_attn",
    )(qn, kcmp, vcmp, ksa, vs, kwa, vw, gates, overlap_t)


def _outproj_kernel(x_ref, od_ref, on_ref, wa_ref, wb_ref, g_ref, b_ref, o_ref, *, alpha):
    y = (alpha * x_ref[...]
         + jnp.dot(od_ref[...], wa_ref[...], preferred_element_type=f32)
         + jnp.dot(on_ref[...], wb_ref[...], preferred_element_type=f32))
    o_ref[...] = _layer_norm(y, g_ref[...], b_ref[...])


def _outproj(x2, od, on, wa, wb, g, b, alpha):
    n, d_model = x2.shape
    tm = DENSE_ROWS
    row_spec = lambda w: pl.BlockSpec((tm, w), lambda i: (i, 0))
    return pl.pallas_call(
        functools.partial(_outproj_kernel, alpha=alpha),
        grid=(n // tm,),
        in_specs=[row_spec(d_model), row_spec(od.shape[1]), row_spec(on.shape[1]),
                  _const_spec(wa.shape), _const_spec(wb.shape), _const_spec(g.shape), _const_spec(b.shape)],
        out_specs=row_spec(d_model),
        out_shape=jax.ShapeDtypeStruct((n, d_model), f32),
        compiler_params=_cparams(("parallel",)), name="outproj_ln",
    )(x2, od, on, wa, wb, g, b)


def _ffn_kernel(h_ref, halo_ref, wu_ref, cw_ref, cb_ref, wd_ref, g_ref, b_ref, o_ref, u_ref, acc_ref,
                *, alpha, tiles_per_seq):
    tm = h_ref.shape[0]
    tf = FFN_CHUNK
    n_chunks = wd_ref.shape[0] // tf
    h = h_ref[...]
    first = lax.rem(pl.program_id(0), tiles_per_seq) == 0
    halo = jnp.where(first, 0.0, halo_ref[...])
    hb = jnp.concatenate([halo, h], axis=0).astype(bf16)
    for c in range(n_chunks):
        cs = slice(c * 2 * tf, (c + 1) * 2 * tf)
        u_ref[...] = jnp.dot(hb, wu_ref[:, cs], preferred_element_type=f32)
        u = cb_ref[:, cs]
        for k in range(CONV_WIDTH):
            u = u + cw_ref[k:k + 1, cs] * u_ref[pl.ds(HALO - (CONV_WIDTH - 1) + k, tm), :]
        act = (jax.nn.silu(u[:, :tf]) * u[:, tf:]).astype(bf16)
        part = jnp.dot(act, wd_ref[c * tf:(c + 1) * tf, :], preferred_element_type=f32)
        if c == 0:
            acc_ref[...] = part
        else:
            acc_ref[...] += part
    o_ref[...] = _layer_norm(alpha * h + acc_ref[...], g_ref[...], b_ref[...])


def _ffn(h1, wu, cw, cb, wd, g, b, alpha, seq):
    n, d_model = h1.shape
    tm = min(DENSE_ROWS, seq)
    single = dict(pipeline_mode=pl.Buffered(1))
    return pl.pallas_call(
        functools.partial(_ffn_kernel, alpha=alpha, tiles_per_seq=seq // tm),
        grid=(n // tm,),
        in_specs=[pl.BlockSpec((tm, d_model), lambda i: (i, 0)),
                  pl.BlockSpec((HALO, d_model), lambda i: (jnp.maximum(i * (tm // HALO) - 1, 0), 0)),
                  pl.BlockSpec(wu.shape, lambda i: (0, 0), **single),
                  _const_spec(cw.shape), _const_spec(cb.shape),
                  pl.BlockSpec(wd.shape, lambda i: (0, 0), **single),
                  _const_spec(g.shape), _const_spec(b.shape)],
        out_specs=pl.BlockSpec((tm, d_model), lambda i: (i, 0)),
        out_shape=jax.ShapeDtypeStruct((n, d_model), f32),
        scratch_shapes=[pltpu.VMEM((HALO + tm, 2 * FFN_CHUNK), f32), pltpu.VMEM((tm, d_model), f32)],
        compiler_params=_cparams(("parallel",)), name="ffn_ln",
    )(h1, h1, wu, cw, cb, wd, g, b)


def _interleave_gate_value(a, d_ff):
    lead = a.shape[:-1]
    a = a.reshape(lead + (2, d_ff // FFN_CHUNK, FFN_CHUNK))
    return jnp.swapaxes(a, -3, -2).reshape(lead + (2 * d_ff,))


def kernel(x, positions, w_in, lambda_q1, lambda_k1, lambda_q2, lambda_k2, diff_norm_g, cmp_pe_k, cmp_w1_k, cmp_b1_k, cmp_w2_k, cmp_pe_v, cmp_w1_v, cmp_b1_v, cmp_w2_v, w_out, ln1_g, ln1_b, w_up, conv_w, conv_b, w_down, ln2_g, ln2_b):
    b, s, d_model = x.shape
    depth = w_in.shape[0]
    n = b * s
    d_ff = w_down.shape[1]
    assert s % ATTN_TILE == 0 or s < ATTN_TILE
    assert s % SLC_BLOCK == 0 and s // SLC_BLOCK <= LANES - HEAD_DIM and d_ff % FFN_CHUNK == 0
    alpha = (2 * depth) ** 0.25
    consts = _rope_consts()
    pos_col = positions.reshape(n, 1)

    n_chunk = s // CMP_STRIDE
    n_cmp = n_chunk - CMP_BLOCK // CMP_STRIDE + 1
    n_slc = s // SLC_BLOCK
    top_k = min(SLC_TOPK, n_slc)
    cmp_pos = positions[:, CMP_BLOCK - 1::CMP_STRIDE][:, :n_cmp]
    cmp_pos = jnp.pad(cmp_pos, ((0, 0), (0, n_chunk - n_cmp))).reshape(b, n_chunk, 1)
    cs = np.arange(n_chunk)[None, :] * CMP_STRIDE
    ss = np.arange(n_slc)[:, None] * SLC_BLOCK
    ov = np.clip(np.minimum(cs + CMP_BLOCK, ss + SLC_BLOCK) - np.maximum(cs, ss), 0, None) / CMP_BLOCK
    ov[:, n_cmp:] = 0.0
    overlap_t = jnp.asarray(ov, dtype=bf16)

    h = x.reshape(n, d_model)
    for l in range(depth):
        lam_init = 0.8 - 0.6 * math.exp(-0.3 * l)
        d_in = w_in.shape[2]
        w_pad = jnp.pad(w_in[l], ((0, 0), (0, -d_in % LANES))).astype(bf16)
        qd, kd, vd, qn, kc, vc, ksa, vs, kwa, vw, gates = _proj(h, pos_col, w_pad, consts, s)

        per = CMP_STRIDE * NSA_KV_GROUPS * HEAD_DIM
        kcmp, vcmp = _compress(
            kc.reshape(b, n_chunk, per), vc.reshape(b, n_chunk, per), cmp_pos, consts,
            _compress_weights(cmp_pe_k[l], cmp_w1_k[l], cmp_b1_k[l], cmp_w2_k[l]),
            _compress_weights(cmp_pe_v[l], cmp_w1_v[l], cmp_b1_v[l], cmp_w2_v[l]))

        lam_vec = jnp.stack([lambda_q1[l], lambda_k1[l], lambda_q2[l], lambda_k2[l]]).astype(f32)
        od = _diff_attention(qd.reshape(b, s, -1), kd.reshape(b, s, -1), vd.reshape(b, s, -1),
                             lam_vec, diff_norm_g[l].reshape(1, -1).astype(f32), lam_init)
        on = _nsa_attention(qn.reshape(b, s, -1), kcmp, vcmp,
                            ksa.reshape(NSA_KV_GROUPS, b, s, LANES), vs.reshape(b, s, LANES),
                            kwa.reshape(NSA_KV_GROUPS, b, s, LANES), vw.reshape(b, s, LANES),
                            gates.reshape(NSA_KV_GROUPS, b, s, LANES), overlap_t, top_k)

        d_diff = od.shape[2]
        wo = w_out[l].astype(bf16)
        h = _outproj(h, od.reshape(n, -1), on.reshape(n, -1), wo[:d_diff], wo[d_diff:],
                     ln1_g[l].reshape(1, -1), ln1_b[l].reshape(1, -1), alpha)
        h = _ffn(h, _interleave_gate_value(w_up[l], d_ff).astype(bf16),
                 _interleave_gate_value(conv_w[l], d_ff), _interleave_gate_value(conv_b[l], d_ff).reshape(1, -1),
                 w_down[l].astype(bf16), ln2_g[l].reshape(1, -1), ln2_b[l].reshape(1, -1), alpha, s)
    return h.reshape(b, s, d_model)
```

```python
import functools
import math

import jax
import jax.numpy as jnp
import numpy as np
from jax import lax
from jax.experimental import pallas as pl
from jax.experimental.pallas import tpu as pltpu

f32 = jnp.float32
bf16 = jnp.bfloat16

LANES = 128
HEAD_DIM = 64
ROPE_DIM = HEAD_DIM // 4
ROPE_THETA = 500000.0
DIFF_HEADS = 4
NSA_HEADS = 8
NSA_KV_GROUPS = 2
NSA_GQA = NSA_HEADS // NSA_KV_GROUPS
CMP_BLOCK = 32
CMP_STRIDE = 16
CMP_HIDDEN = 2 * HEAD_DIM
SLC_BLOCK = 64
SLC_TOPK = 16
WINDOW = 512
CONV_WIDTH = 3
LN_EPS = 1e-5
RMS_EPS = 1e-5
NEG_INF = -1e30
SEL_BIAS = -1e9
FORCED_SCORE = 1e6
VMEM_LIMIT = 56 * 1024 * 1024

LOG2E = 1.4426950408889634
QK_SCALE = HEAD_DIM ** -0.5 * LOG2E

PROJ_ROWS = 512
ATTN_TILE = 512
NSA_Q_TILE = 256
NSA_K_TILE = 512
SOFTMAX_ROWS = 256
DENSE_ROWS = 512
FFN_CHUNK = 256
HALO = 8


def _cparams(sem):
    return pltpu.CompilerParams(dimension_semantics=sem, vmem_limit_bytes=VMEM_LIMIT)


def _const_spec(shape):
    n = len(shape)
    return pl.BlockSpec(shape, lambda *_: (0,) * n)


def _layer_norm(y, g, b):
    mu = jnp.mean(y, axis=-1, keepdims=True)
    d = y - mu
    var = jnp.mean(d * d, axis=-1, keepdims=True)
    return d * lax.rsqrt(var + LN_EPS) * g + b


def _rope_consts():
    lane = np.arange(LANES)
    in_head = lane % HEAD_DIM
    half = ROPE_DIM // 2
    inv_freq = 1.0 / (ROPE_THETA ** (jnp.arange(half, dtype=f32) / half))
    c = jnp.zeros((8, LANES), f32)
    c = c.at[0].set(jnp.tile(inv_freq, LANES // half))
    c = c.at[1].set(jnp.asarray(in_head < ROPE_DIM, f32))
    c = c.at[2].set(jnp.asarray(np.where(in_head < half, -1.0, np.where(in_head < ROPE_DIM, 1.0, 0.0)), f32))
    c = c.at[3].set(jnp.asarray(in_head < half, f32))
    return c


def _rope_tables(pos_col, c_ref):
    ang = pos_col.astype(f32) * c_ref[0:1, :]
    cos_t = jnp.where(c_ref[1:2, :] > 0.0, jnp.cos(ang), 1.0)
    sin_t = jnp.sin(ang) * c_ref[2:3, :]
    return cos_t, sin_t, c_ref[3:4, :] > 0.0


def _rope(y, tables):
    cos_t, sin_t, first = tables
    half = ROPE_DIM // 2
    partner = jnp.where(first, pltpu.roll(y, LANES - half, 1), pltpu.roll(y, half, 1))
    return y * cos_t + partner * sin_t


def _proj_kernel(x_ref, pos_ref, w_ref, c_ref,
                 qd_ref, kd_ref, vd_ref, qn_ref, kc_ref, vc_ref, ks_ref, vs_ref, kw_ref, vw_ref, g_ref,
                 *, seq):
    tm = x_ref.shape[0]
    xb = x_ref[...].astype(bf16)
    tables = _rope_tables(pos_ref[...], c_ref)
    lane = lax.broadcasted_iota(jnp.int32, (tm, LANES), 1)
    low = lane < HEAD_DIM
    scale = QK_SCALE

    def seg(col, width):
        return jnp.dot(xb, w_ref[:, col:col + width], preferred_element_type=f32)

    hd2 = DIFF_HEADS * 2 * HEAD_DIM
    y = seg(0, hd2)
    for c in range(hd2 // LANES):
        sl = slice(c * LANES, (c + 1) * LANES)
        qd_ref[:, sl] = (_rope(y[:, sl], tables) * scale).astype(bf16)
    y = seg(hd2, hd2)
    for c in range(hd2 // LANES):
        sl = slice(c * LANES, (c + 1) * LANES)
        kd_ref[:, sl] = _rope(y[:, sl], tables).astype(bf16)
    y = seg(2 * hd2, hd2)
    ones = jnp.ones((tm, LANES), bf16)
    for c in range(hd2 // LANES):
        vd_ref[:, (2 * c) * LANES:(2 * c + 1) * LANES] = y[:, c * LANES:(c + 1) * LANES].astype(bf16)
        vd_ref[:, (2 * c + 1) * LANES:(2 * c + 2) * LANES] = ones

    col = 3 * hd2
    y = seg(col, NSA_HEADS * HEAD_DIM)
    for c in range(NSA_HEADS // 2):
        slab = _rope(y[:, c * LANES:(c + 1) * LANES], tables) * scale
        qn_ref[:, (2 * c) * LANES:(2 * c + 1) * LANES] = jnp.where(low, slab, 0.0).astype(bf16)
        qn_ref[:, (2 * c + 1) * LANES:(2 * c + 2) * LANES] = jnp.where(
            low, pltpu.roll(slab, HEAD_DIM, 1), 0.0).astype(bf16)
    col += NSA_HEADS * HEAD_DIM

    kc_ref[...] = seg(col, LANES)
    vc_ref[...] = seg(col + LANES, LANES)

    row = lax.broadcasted_iota(jnp.int32, (tm, LANES), 0) + lax.rem(pl.program_id(0) * tm, seq)
    onehot = jnp.where(lane - HEAD_DIM == row // SLC_BLOCK, 1.0, 0.0)
    y = _rope(seg(col + 2 * LANES, LANES), tables)
    ks_ref[0] = jnp.where(low, y, onehot).astype(bf16)
    ks_ref[1] = jnp.where(low, pltpu.roll(y, HEAD_DIM, 1), onehot).astype(bf16)
    y = seg(col + 3 * LANES, LANES)
    vs_ref[0] = jnp.where(low, y, 1.0).astype(bf16)
    vs_ref[1] = jnp.where(low, pltpu.roll(y, HEAD_DIM, 1), 1.0).astype(bf16)
    y = _rope(seg(col + 4 * LANES, LANES), tables)
    kw_ref[0] = jnp.where(low, y, 0.0).astype(bf16)
    kw_ref[1] = jnp.where(low, pltpu.roll(y, HEAD_DIM, 1), 0.0).astype(bf16)
    y = seg(col + 5 * LANES, LANES)
    vw_ref[0] = jnp.where(low, y, 1.0).astype(bf16)
    vw_ref[1] = jnp.where(low, pltpu.roll(y, HEAD_DIM, 1), 1.0).astype(bf16)

    gate = jax.nn.sigmoid(seg(col + 6 * LANES, LANES))
    g_ref[0] = gate
    g_ref[1] = pltpu.roll(gate, LANES - NSA_GQA * 3, 1)


def _proj(x2, pos_col, w_pad, consts, seq):
    n, d_model = x2.shape
    tm = PROJ_ROWS
    hd2 = DIFF_HEADS * 2 * HEAD_DIM
    row_spec = lambda w: pl.BlockSpec((tm, w), lambda i: (i, 0))
    grp_spec = pl.BlockSpec((NSA_KV_GROUPS, tm, LANES), lambda i: (0, i, 0))
    out_shape = (
        jax.ShapeDtypeStruct((n, hd2), bf16), jax.ShapeDtypeStruct((n, hd2), bf16),
        jax.ShapeDtypeStruct((n, 2 * hd2), bf16), jax.ShapeDtypeStruct((n, NSA_HEADS * LANES), bf16),
        jax.ShapeDtypeStruct((n, LANES), f32), jax.ShapeDtypeStruct((n, LANES), f32),
        jax.ShapeDtypeStruct((NSA_KV_GROUPS, n, LANES), bf16), jax.ShapeDtypeStruct((NSA_KV_GROUPS, n, LANES), bf16),
        jax.ShapeDtypeStruct((NSA_KV_GROUPS, n, LANES), bf16), jax.ShapeDtypeStruct((NSA_KV_GROUPS, n, LANES), bf16),
        jax.ShapeDtypeStruct((NSA_KV_GROUPS, n, LANES), f32),
    )
    out_specs = (row_spec(hd2), row_spec(hd2), row_spec(2 * hd2), row_spec(NSA_HEADS * LANES),
                 row_spec(LANES), row_spec(LANES), grp_spec, grp_spec, grp_spec, grp_spec, grp_spec)
    return pl.pallas_call(
        functools.partial(_proj_kernel, seq=seq),
        grid=(n // tm,),
        in_specs=[row_spec(d_model), pl.BlockSpec((tm, 1), lambda i: (i, 0)),
                  _const_spec(w_pad.shape), _const_spec(consts.shape)],
        out_specs=out_specs, out_shape=out_shape,
        compiler_params=_cparams(("parallel",)), name="proj",
    )(x2, pos_col, w_pad, consts)


def _compress_one(c_ref, pea_ref, peb_ref, w1a_ref, w1b_ref, b1_ref, w2_ref):
    ck = c_ref[0]
    n = ck.shape[0]
    hid_a = jnp.dot((ck + pea_ref[...]).astype(bf16), w1a_ref[...], preferred_element_type=f32)
    hid_b = jnp.dot((ck + peb_ref[...]).astype(bf16), w1b_ref[...], preferred_element_type=f32)
    hid = hid_a + pltpu.roll(hid_b, n - 1, 0) + b1_ref[...]
    hid = jax.nn.gelu(hid)
    return jnp.dot(hid.astype(bf16), w2_ref[...], preferred_element_type=f32)


def _compress_kernel(kc_ref, vc_ref, pos_ref, c_ref,
                     pak_ref, pbk_ref, w1ak_ref, w1bk_ref, b1k_ref, w2k_ref,
                     pav_ref, pbv_ref, w1av_ref, w1bv_ref, b1v_ref, w2v_ref,
                     kcmp_ref, vcmp_ref):
    k = _compress_one(kc_ref, pak_ref, pbk_ref, w1ak_ref, w1bk_ref, b1k_ref, w2k_ref)
    k = _rope(k, _rope_tables(pos_ref[0], c_ref))
    low = lax.broadcasted_iota(jnp.int32, k.shape, 1) < HEAD_DIM
    kcmp_ref[0, 0] = jnp.where(low, k, 0.0).astype(bf16)
    kcmp_ref[0, 1] = jnp.where(low, pltpu.roll(k, HEAD_DIM, 1), 0.0).astype(bf16)
    v = _compress_one(vc_ref, pav_ref, pbv_ref, w1av_ref, w1bv_ref, b1v_ref, w2v_ref)
    vcmp_ref[0, 0] = jnp.where(low, v, 0.0).astype(bf16)
    vcmp_ref[0, 1] = jnp.where(low, pltpu.roll(v, HEAD_DIM, 1), 0.0).astype(bf16)


def _compress_weights(pe, w1, b1, w2):
    r = CMP_BLOCK // CMP_STRIDE
    assert r == 2
    eye = jnp.eye(NSA_KV_GROUPS, dtype=f32)
    w1r = w1.reshape(CMP_BLOCK, HEAD_DIM, CMP_HIDDEN)
    per = CMP_STRIDE * NSA_KV_GROUPS * HEAD_DIM

    def big(part):
        return jnp.einsum('tcm,gh->tgchm', part, eye).reshape(per, NSA_KV_GROUPS * CMP_HIDDEN).astype(bf16)

    def pe_row(part):
        return jnp.broadcast_to(part[:, None, :], (CMP_STRIDE, NSA_KV_GROUPS, HEAD_DIM)).reshape(1, per)

    w2b = jnp.einsum('mc,gh->gmhc', w2, eye).reshape(NSA_KV_GROUPS * CMP_HIDDEN, NSA_KV_GROUPS * HEAD_DIM)
    return (pe_row(pe[:CMP_STRIDE]), pe_row(pe[CMP_STRIDE:]), big(w1r[:CMP_STRIDE]), big(w1r[CMP_STRIDE:]),
            jnp.tile(b1, NSA_KV_GROUPS).reshape(1, -1), w2b.astype(bf16))


def _compress(kc, vc, cmp_pos, consts, wk, wv):
    b, nch, per = kc.shape
    seq_spec = pl.BlockSpec((1, nch, per), lambda i: (i, 0, 0))
    w_specs = [_const_spec(w.shape) for w in wk + wv]
    return pl.pallas_call(
        _compress_kernel,
        grid=(b,),
        in_specs=[seq_spec, seq_spec, pl.BlockSpec((1, nch, 1), lambda i: (i, 0, 0)),
                  _const_spec(consts.shape)] + w_specs,
        out_specs=(pl.BlockSpec((1, NSA_KV_GROUPS, nch, LANES), lambda i: (i, 0, 0, 0)),
                   pl.BlockSpec((1, NSA_KV_GROUPS, nch, LANES), lambda i: (i, 0, 0, 0))),
        out_shape=(jax.ShapeDtypeStruct((b, NSA_KV_GROUPS, nch, LANES), bf16),
                   jax.ShapeDtypeStruct((b, NSA_KV_GROUPS, nch, LANES), bf16)),
        compiler_params=_cparams(("parallel",)), name="compress",
    )(kc, vc, cmp_pos, consts, *wk, *wv)


def _softmax_init(m_ref, acc_ref):
    m_ref[...] = jnp.full(m_ref.shape, NEG_INF, f32)
    acc_ref[...] = jnp.zeros(acc_ref.shape, f32)


def _softmax_step(s, v_ones, m_ref, acc_ref, rs):
    m_prev = m_ref[rs, :]
    m_new = jnp.maximum(m_prev, jnp.max(s, axis=1, keepdims=True))
    alpha = jnp.exp2(m_prev - m_new)
    p = jnp.exp2(s - jnp.concatenate([m_new] * (s.shape[1] // LANES), axis=1))
    pv = jnp.dot(p.astype(bf16), v_ones, preferred_element_type=f32)
    acc_ref[rs, :] = jnp.concatenate([alpha] * (acc_ref.shape[1] // LANES), axis=1) * acc_ref[rs, :] + pv
    m_ref[rs, :] = m_new


def _scores(q, k):
    return lax.dot_general(q, k, (((1,), (1,)), ((), ())), preferred_element_type=f32)


def _diff_kernel(q_ref, k_ref, v_ref, lam_ref, g_ref, o_ref, q2_ref, m_ref, acc_ref, *, lam_init):
    tq = q_ref.shape[1]
    tk = tq
    rc = min(SOFTMAX_ROWS, tq)
    vdim = o_ref.shape[2]
    i = pl.program_id(2)
    q = q_ref[0]
    low = lax.broadcasted_iota(jnp.int32, q.shape, 1) < HEAD_DIM
    zero = jnp.zeros_like(q)
    q2_ref[0:tq, :] = jnp.where(low, q, zero)
    q2_ref[tq:2 * tq, :] = jnp.where(low, zero, q)
    _softmax_init(m_ref, acc_ref)

    def tile(j, masked):
        off = pl.multiple_of(j * tk, tk)
        for c in range(2 * tq // rc):
            rs = slice(c * rc, (c + 1) * rc)
            q_off = (c * rc) % tq
            cols = min(tk, q_off + rc) if masked else tk
            s = _scores(q2_ref[rs, :], k_ref[0, pl.ds(off, cols), :])
            if masked:
                r = q_off + lax.broadcasted_iota(jnp.int32, (rc, cols), 0)
                s = jnp.where(r >= lax.broadcasted_iota(jnp.int32, (rc, cols), 1), s, NEG_INF)
            _softmax_step(s, v_ref[0, pl.ds(off, cols), :], m_ref, acc_ref, rs)

    def body(j, carry):
        tile(j, False)
        return carry

    lax.fori_loop(0, i, body, 0)
    tile(i, True)

    lam_v = lam_ref[...]
    lam = (jnp.exp(jnp.sum(lam_v[0:1] * lam_v[1:2], axis=1, keepdims=True))
           - jnp.exp(jnp.sum(lam_v[2:3] * lam_v[3:4], axis=1, keepdims=True)) + lam_init)
    o = acc_ref[:, :vdim] / acc_ref[:, vdim:]
    od = o[:tq] - lam * o[tq:]
    od = od * lax.rsqrt(jnp.mean(od * od, axis=-1, keepdims=True) + RMS_EPS)
    o_ref[0] = (od * g_ref[...] * (1.0 - lam_init)).astype(o_ref.dtype)


def _diff_attention(qd, kd, vd, lam_vec, diff_g, lam_init):
    b, s, _ = qd.shape
    tq = min(ATTN_TILE, s)
    vdim = vd.shape[2] // DIFF_HEADS // 2
    assert vdim == LANES
    return pl.pallas_call(
        functools.partial(_diff_kernel, lam_init=lam_init),
        grid=(b, DIFF_HEADS, s // tq),
        in_specs=[pl.BlockSpec((1, tq, LANES), lambda bi, h, i: (bi, i, h)),
                  pl.BlockSpec((1, s, LANES), lambda bi, h, i: (bi, 0, h)),
                  pl.BlockSpec((1, s, 2 * vdim), lambda bi, h, i: (bi, 0, h)),
                  _const_spec(lam_vec.shape), _const_spec(diff_g.shape)],
        out_specs=pl.BlockSpec((1, tq, vdim), lambda bi, h, i: (bi, i, h)),
        out_shape=jax.ShapeDtypeStruct((b, s, DIFF_HEADS * vdim), bf16),
        scratch_shapes=[pltpu.VMEM((2 * tq, LANES), bf16), pltpu.VMEM((2 * tq, LANES), f32),
                        pltpu.VMEM((2 * tq, 2 * vdim), f32)],
        compiler_params=_cparams(("parallel", "parallel", "arbitrary")), name="diff_attn",
    )(qd, kd, vd, lam_vec, diff_g)


def _nsa_kernel(q_ref, kc_ref, vc_ref, ks_ref, vs_ref, kw_ref, vw_ref, g_ref, ov_ref, o_ref,
                qa_ref, m_ref, acc_ref, oc_ref, osel_ref, *, top_k):
    tq = q_ref.shape[1]
    seq = ks_ref.shape[2]
    tk = min(NSA_K_TILE, seq)
    ncp = kc_ref.shape[2]
    nslc = ov_ref.shape[0]
    rep = NSA_GQA
    i = pl.program_id(2)
    s0 = i * tq
    heads = [slice(r * tq, (r + 1) * tq) for r in range(rep)]

    t_col = s0 + lax.broadcasted_iota(jnp.int32, (tq, ncp), 0)
    cmp_end = lax.broadcasted_iota(jnp.int32, (tq, ncp), 1) * CMP_STRIDE + (CMP_BLOCK - 1)
    cmp_ok = cmp_end <= t_col
    kc = kc_ref[0, 0]
    vc = vc_ref[0, 0]
    psum = jnp.zeros((tq, ncp), f32)
    for r in range(rep):
        s = jnp.where(cmp_ok, _scores(q_ref[0, :, r * LANES:(r + 1) * LANES], kc), NEG_INF)
        p = jnp.where(cmp_ok, jnp.exp2(s - jnp.max(s, axis=1, keepdims=True)), 0.0)
        den = jnp.sum(p, axis=1, keepdims=True)
        p = p / jnp.where(den > 0.0, den, 1.0)
        psum = psum + p
        oc_ref[heads[r], :] = jnp.dot(p.astype(bf16), vc, preferred_element_type=f32)

    p_hi = psum.astype(bf16)
    p_lo = (psum - p_hi.astype(f32)).astype(bf16)
    ov = ov_ref[...]
    imp = _scores(ov, p_hi) + _scores(ov, p_lo)
    blk = lax.broadcasted_iota(jnp.int32, (nslc, tq), 0)
    cur = (s0 + lax.broadcasted_iota(jnp.int32, (nslc, tq), 1)) // SLC_BLOCK
    imp = jnp.where(blk > cur, -1.0, imp)
    imp = jnp.where((blk == 0) | (blk == cur) | (blk == cur - 1), FORCED_SCORE, imp)
    rank = jnp.zeros((nslc, tq), f32)
    for jp in range(nslc):
        other = imp[jp:jp + 1, :]
        ahead = jnp.where(blk > jp, jnp.where(other >= imp, 1.0, 0.0), jnp.where(other > imp, 1.0, 0.0))
        rank = rank + ahead
    bias_t = jnp.where(rank < top_k, 0.0, SEL_BIAS)
    pieces = [jnp.zeros((HEAD_DIM, tq), f32), bias_t]
    if nslc < LANES - HEAD_DIM:
        pieces.append(jnp.zeros((LANES - HEAD_DIM - nslc, tq), f32))
    bias = jnp.concatenate(pieces, axis=0).T.astype(bf16)
    for r in range(rep):
        qa_ref[heads[r], :] = q_ref[0, :, r * LANES:(r + 1) * LANES] + bias

    row = lax.broadcasted_iota(jnp.int32, (tq, tq), 0)
    col = lax.broadcasted_iota(jnp.int32, (tq, tq), 1)

    def attend(q_of, k_ref, v_ref, off, width, keep):
        k = k_ref[0, 0, pl.ds(off, width), :]
        v = v_ref[0, 0, pl.ds(off, width), :]
        for r in range(rep):
            s = _scores(q_of(r), k)
            if keep is not None:
                s = jnp.where(keep, s, NEG_INF)
            _softmax_step(s, v, m_ref, acc_ref, heads[r])

    def normalized():
        acc = acc_ref[...]
        low = lax.broadcasted_iota(jnp.int32, acc.shape, 1) < HEAD_DIM
        return acc / jnp.where(low, pltpu.roll(acc, HEAD_DIM, 1), 1.0)

    _softmax_init(m_ref, acc_ref)
    sel_q = lambda r: qa_ref[heads[r], :]

    def sel_body(j, carry):
        attend(sel_q, ks_ref, vs_ref, pl.multiple_of(j * tk, tk), tk, None)
        return carry

    lax.fori_loop(0, s0 // tk, sel_body, 0)

    @pl.when(lax.rem(i, tk // tq) == 1)
    def _():
        attend(sel_q, ks_ref, vs_ref, pl.multiple_of(s0 - tq, tq), tq, None)

    attend(sel_q, ks_ref, vs_ref, pl.multiple_of(s0, tq), tq, col <= row)
    osel_ref[...] = normalized()

    _softmax_init(m_ref, acc_ref)
    win_q = lambda r: q_ref[0, :, r * LANES:(r + 1) * LANES]

    @pl.when(i >= 2)
    def _():
        attend(win_q, kw_ref, vw_ref, pl.multiple_of(s0 - 2 * tq, tq), tq, col > row)

    @pl.when(i >= 1)
    def _():
        attend(win_q, kw_ref, vw_ref, pl.multiple_of(s0 - tq, tq), tq, None)

    attend(win_q, kw_ref, vw_ref, pl.multiple_of(s0, tq), tq, col <= row)
    o_win = normalized()

    gates = g_ref[0, 0]
    lane = lax.broadcasted_iota(jnp.int32, (tq, LANES), 1)
    gated = []
    for r in range(rep):
        gated.append(gates[:, 3 * r:3 * r + 1] * oc_ref[heads[r], :]
                     + gates[:, 3 * r + 1:3 * r + 2] * osel_ref[heads[r], :]
                     + gates[:, 3 * r + 2:3 * r + 3] * o_win[heads[r]])
    for c in range(rep // 2):
        o_ref[0, :, c * LANES:(c + 1) * LANES] = jnp.where(
            lane < HEAD_DIM, gated[2 * c], pltpu.roll(gated[2 * c + 1], HEAD_DIM, 1)).astype(o_ref.dtype)


def _nsa_attention(qn, kcmp, vcmp, ksa, vsa, kwa, vwa, gates, overlap_t, top_k):
    b, s, _ = qn.shape
    tq = min(NSA_Q_TILE, s)
    ncp = kcmp.shape[2]
    rep = NSA_GQA
    assert WINDOW == 2 * tq and min(NSA_K_TILE, s) == 2 * tq
    cmp_spec = pl.BlockSpec((1, 1, ncp, LANES), lambda bi, g, i: (bi, g, 0, 0))
    seq_spec = pl.BlockSpec((1, 1, s, LANES), lambda bi, g, i: (g, bi, 0, 0))
    rows = rep * tq
    return pl.pallas_call(
        functools.partial(_nsa_kernel, top_k=top_k),
        grid=(b, NSA_KV_GROUPS, s // tq),
        in_specs=[pl.BlockSpec((1, tq, rep * LANES), lambda bi, g, i: (bi, i, g)),
                  cmp_spec, cmp_spec, seq_spec, seq_spec, seq_spec, seq_spec,
                  pl.BlockSpec((1, 1, tq, LANES), lambda bi, g, i: (g, bi, i, 0)),
                  _const_spec(overlap_t.shape)],
        out_specs=pl.BlockSpec((1, tq, rep * HEAD_DIM), lambda bi, g, i: (bi, i, g)),
        out_shape=jax.ShapeDtypeStruct((b, s, NSA_HEADS * HEAD_DIM), bf16),
        scratch_shapes=[pltpu.VMEM((rows, LANES), bf16), pltpu.VMEM((rows, LANES), f32),
                        pltpu.VMEM((rows, LANES), f32), pltpu.VMEM((rows, LANES), f32),
                        pltpu.VMEM((rows, LANES), f32)],
        compiler_params=_cparams(("parallel", "parallel", "arbitrary")), name="nsa_attn",
    )(qn, kcmp, vcmp, ksa, vsa, kwa, vwa, gates, overlap_t)


def _outproj_kernel(x_ref, od_ref, on_ref, wa_ref, wb_ref, g_ref, b_ref, o_ref, *, alpha):
    y = (alpha * x_ref[...]
         + jnp.dot(od_ref[...], wa_ref[...], preferred_element_type=f32)
         + jnp.dot(on_ref[...], wb_ref[...], preferred_element_type=f32))
    o_ref[...] = _layer_norm(y, g_ref[...], b_ref[...])


def _outproj(x2, od, on, wa, wb, g, b, alpha):
    n, d_model = x2.shape
    tm = DENSE_ROWS
    row_spec = lambda w: pl.BlockSpec((tm, w), lambda i: (i, 0))
    return pl.pallas_call(
        functools.partial(_outproj_kernel, alpha=alpha),
        grid=(n // tm,),
        in_specs=[row_spec(d_model), row_spec(od.shape[1]), row_spec(on.shape[1]),
                  _const_spec(wa.shape), _const_spec(wb.shape), _const_spec(g.shape), _const_spec(b.shape)],
        out_specs=row_spec(d_model),
        out_shape=jax.ShapeDtypeStruct((n, d_model), f32),
        compiler_params=_cparams(("parallel",)), name="outproj_ln",
    )(x2, od, on, wa, wb, g, b)


def _ffn_kernel(h_ref, halo_ref, wu_ref, cw_ref, cb_ref, wd_ref, g_ref, b_ref, o_ref, u_ref, acc_ref,
                *, alpha, tiles_per_seq):
    tm = h_ref.shape[0]
    tf = FFN_CHUNK
    n_chunks = wd_ref.shape[0] // tf
    h = h_ref[...]
    first = lax.rem(pl.program_id(0), tiles_per_seq) == 0
    halo = jnp.where(first, 0.0, halo_ref[...])
    hb = jnp.concatenate([halo, h], axis=0).astype(bf16)
    for c in range(n_chunks):
        cs = slice(c * 2 * tf, (c + 1) * 2 * tf)
        u_ref[...] = jnp.dot(hb, wu_ref[:, cs], preferred_element_type=f32)
        u = cb_ref[:, cs]
        for k in range(CONV_WIDTH):
            u = u + cw_ref[k:k + 1, cs] * u_ref[pl.ds(HALO - (CONV_WIDTH - 1) + k, tm), :]
        act = (jax.nn.silu(u[:, :tf]) * u[:, tf:]).astype(bf16)
        part = jnp.dot(act, wd_ref[c * tf:(c + 1) * tf, :], preferred_element_type=f32)
        if c == 0:
            acc_ref[...] = part
        else:
            acc_ref[...] += part
    o_ref[...] = _layer_norm(alpha * h + acc_ref[...], g_ref[...], b_ref[...])


def _ffn(h1, wu, cw, cb, wd, g, b, alpha, seq):
    n, d_model = h1.shape
    tm = min(DENSE_ROWS, seq)
    single = dict(pipeline_mode=pl.Buffered(1))
    return pl.pallas_call(
        functools.partial(_ffn_kernel, alpha=alpha, tiles_per_seq=seq // tm),
        grid=(n // tm,),
        in_specs=[pl.BlockSpec((tm, d_model), lambda i: (i, 0)),
                  pl.BlockSpec((HALO, d_model), lambda i: (jnp.maximum(i * (tm // HALO) - 1, 0), 0)),
                  pl.BlockSpec(wu.shape, lambda i: (0, 0), **single),
                  _const_spec(cw.shape), _const_spec(cb.shape),
                  pl.BlockSpec(wd.shape, lambda i: (0, 0), **single),
                  _const_spec(g.shape), _const_spec(b.shape)],
        out_specs=pl.BlockSpec((tm, d_model), lambda i: (i, 0)),
        out_shape=jax.ShapeDtypeStruct((n, d_model), f32),
        scratch_shapes=[pltpu.VMEM((HALO + tm, 2 * FFN_CHUNK), f32), pltpu.VMEM((tm, d_model), f32)],
        compiler_params=_cparams(("parallel",)), name="ffn_ln",
    )(h1, h1, wu, cw, cb, wd, g, b)


def _interleave_gate_value(a, d_ff):
    lead = a.shape[:-1]
    a = a.reshape(lead + (2, d_ff // FFN_CHUNK, FFN_CHUNK))
    return jnp.swapaxes(a, -3, -2).reshape(lead + (2 * d_ff,))


def kernel(x, positions, w_in, lambda_q1, lambda_k1, lambda_q2, lambda_k2, diff_norm_g, cmp_pe_k, cmp_w1_k, cmp_b1_k, cmp_w2_k, cmp_pe_v, cmp_w1_v, cmp_b1_v, cmp_w2_v, w_out, ln1_g, ln1_b, w_up, conv_w, conv_b, w_down, ln2_g, ln2_b):
    b, s, d_model = x.shape
    depth = w_in.shape[0]
    n = b * s
    d_ff = w_down.shape[1]
    assert s % ATTN_TILE == 0 or s < ATTN_TILE
    assert s % SLC_BLOCK == 0 and s // SLC_BLOCK <= LANES - HEAD_DIM and d_ff % FFN_CHUNK == 0
    alpha = (2 * depth) ** 0.25
    consts = _rope_consts()
    pos_col = positions.reshape(n, 1)

    n_chunk = s // CMP_STRIDE
    n_cmp = n_chunk - CMP_BLOCK // CMP_STRIDE + 1
    n_slc = s // SLC_BLOCK
    top_k = min(SLC_TOPK, n_slc)
    cmp_pos = positions[:, CMP_BLOCK - 1::CMP_STRIDE][:, :n_cmp]
    cmp_pos = jnp.pad(cmp_pos, ((0, 0), (0, n_chunk - n_cmp))).reshape(b, n_chunk, 1)
    cs = np.arange(n_chunk)[None, :] * CMP_STRIDE
    ss = np.arange(n_slc)[:, None] * SLC_BLOCK
    ov = np.clip(np.minimum(cs + CMP_BLOCK, ss + SLC_BLOCK) - np.maximum(cs, ss), 0, None) / CMP_BLOCK
    ov[:, n_cmp:] = 0.0
    overlap_t = jnp.asarray(ov, dtype=bf16)

    h = x.reshape(n, d_model)
    for l in range(depth):
        lam_init = 0.8 - 0.6 * math.exp(-0.3 * l)
        d_in = w_in.shape[2]
        w_pad = jnp.pad(w_in[l], ((0, 0), (0, -d_in % LANES))).astype(bf16)
        qd, kd, vd, qn, kc, vc, ksa, vsa, kwa, vwa, gates = _proj(h, pos_col, w_pad, consts, s)

        per = CMP_STRIDE * NSA_KV_GROUPS * HEAD_DIM
        kcmp, vcmp = _compress(
            kc.reshape(b, n_chunk, per), vc.reshape(b, n_chunk, per), cmp_pos, consts,
            _compress_weights(cmp_pe_k[l], cmp_w1_k[l], cmp_b1_k[l], cmp_w2_k[l]),
            _compress_weights(cmp_pe_v[l], cmp_w1_v[l], cmp_b1_v[l], cmp_w2_v[l]))

        lam_vec = jnp.stack([lambda_q1[l], lambda_k1[l], lambda_q2[l], lambda_k2[l]]).astype(f32)
        od = _diff_attention(qd.reshape(b, s, -1), kd.reshape(b, s, -1), vd.reshape(b, s, -1),
                             lam_vec, diff_norm_g[l].reshape(1, -1).astype(f32), lam_init)
        grouped = lambda a: a.reshape(NSA_KV_GROUPS, b, s, LANES)
        on = _nsa_attention(qn.reshape(b, s, -1), kcmp, vcmp, grouped(ksa), grouped(vsa), grouped(kwa),
                            grouped(vwa), grouped(gates), overlap_t, top_k)

        d_diff = od.shape[2]
        wo = w_out[l].astype(bf16)
        h = _outproj(h, od.reshape(n, -1), on.reshape(n, -1), wo[:d_diff], wo[d_diff:],
                     ln1_g[l].reshape(1, -1), ln1_b[l].reshape(1, -1), alpha)
        h = _ffn(h, _interleave_gate_value(w_up[l], d_ff).astype(bf16),
                 _interleave_gate_value(conv_w[l], d_ff), _interleave_gate_value(conv_b[l], d_ff).reshape(1, -1),
                 w_down[l].astype(bf16), ln2_g[l].reshape(1, -1), ln2_b[l].reshape(1, -1), alpha, s)
    return h.reshape(b, s, d_model)
```

```python
import functools
import math

import jax
import jax.numpy as jnp
import numpy as np
from jax import lax
from jax.experimental import pallas as pl
from jax.experimental.pallas import tpu as pltpu

f32 = jnp.float32
bf16 = jnp.bfloat16

LANES = 128
HEAD_DIM = 64
ROPE_DIM = HEAD_DIM // 4
ROPE_THETA = 500000.0
DIFF_HEADS = 4
NSA_HEADS = 8
NSA_KV_GROUPS = 2
NSA_GQA = NSA_HEADS // NSA_KV_GROUPS
CMP_BLOCK = 32
CMP_STRIDE = 16
CMP_HIDDEN = 2 * HEAD_DIM
SLC_BLOCK = 64
SLC_TOPK = 16
WINDOW = 512
CONV_WIDTH = 3
LN_EPS = 1e-5
RMS_EPS = 1e-5
NEG_INF = -1e30
SEL_BIAS = -1e9
FORCED_SCORE = 1e6
VMEM_LIMIT = 56 * 1024 * 1024

LOG2E = 1.4426950408889634
QK_SCALE = HEAD_DIM ** -0.5 * LOG2E

PROJ_ROWS = 512
ATTN_TILE = 512
NSA_Q_TILE = 256
NSA_K_TILE = 512
SOFTMAX_ROWS = 256
DENSE_ROWS = 512
FFN_CHUNK = 256
HALO = 8


def _cparams(sem):
    return pltpu.CompilerParams(dimension_semantics=sem, vmem_limit_bytes=VMEM_LIMIT)


def _const_spec(shape):
    n = len(shape)
    return pl.BlockSpec(shape, lambda *_: (0,) * n)


def _layer_norm(y, g, b):
    mu = jnp.mean(y, axis=-1, keepdims=True)
    d = y - mu
    var = jnp.mean(d * d, axis=-1, keepdims=True)
    return d * lax.rsqrt(var + LN_EPS) * g + b


def _rope_consts():
    lane = np.arange(LANES)
    in_head = lane % HEAD_DIM
    half = ROPE_DIM // 2
    inv_freq = 1.0 / (ROPE_THETA ** (jnp.arange(half, dtype=f32) / half))
    c = jnp.zeros((8, LANES), f32)
    c = c.at[0].set(jnp.tile(inv_freq, LANES // half))
    c = c.at[1].set(jnp.asarray(in_head < ROPE_DIM, f32))
    c = c.at[2].set(jnp.asarray(np.where(in_head < half, -1.0, np.where(in_head < ROPE_DIM, 1.0, 0.0)), f32))
    c = c.at[3].set(jnp.asarray(in_head < half, f32))
    return c


def _rope_tables(pos_col, c_ref):
    ang = pos_col.astype(f32) * c_ref[0:1, :]
    cos_t = jnp.where(c_ref[1:2, :] > 0.0, jnp.cos(ang), 1.0)
    sin_t = jnp.sin(ang) * c_ref[2:3, :]
    return cos_t, sin_t, c_ref[3:4, :] > 0.0


def _rope(y, tables):
    cos_t, sin_t, first = tables
    half = ROPE_DIM // 2
    partner = jnp.where(first, pltpu.roll(y, LANES - half, 1), pltpu.roll(y, half, 1))
    return y * cos_t + partner * sin_t


def _proj_kernel(x_ref, pos_ref, w_ref, c_ref,
                 qd_ref, kd_ref, vd_ref, qn_ref, kc_ref, vc_ref, ks_ref, vs_ref, kw_ref, vw_ref, g_ref,
                 *, seq):
    tm = x_ref.shape[0]
    xb = x_ref[...].astype(bf16)
    tables = _rope_tables(pos_ref[...], c_ref)
    lane = lax.broadcasted_iota(jnp.int32, (tm, LANES), 1)
    low = lane < HEAD_DIM
    scale = QK_SCALE

    def seg(col, width):
        return jnp.dot(xb, w_ref[:, col:col + width], preferred_element_type=f32)

    hd2 = DIFF_HEADS * 2 * HEAD_DIM
    y = seg(0, hd2)
    for c in range(hd2 // LANES):
        sl = slice(c * LANES, (c + 1) * LANES)
        qd_ref[:, sl] = (_rope(y[:, sl], tables) * scale).astype(bf16)
    y = seg(hd2, hd2)
    for c in range(hd2 // LANES):
        sl = slice(c * LANES, (c + 1) * LANES)
        kd_ref[:, sl] = _rope(y[:, sl], tables).astype(bf16)
    y = seg(2 * hd2, hd2)
    ones = jnp.ones((tm, LANES), bf16)
    for c in range(hd2 // LANES):
        vd_ref[:, (2 * c) * LANES:(2 * c + 1) * LANES] = y[:, c * LANES:(c + 1) * LANES].astype(bf16)
        vd_ref[:, (2 * c + 1) * LANES:(2 * c + 2) * LANES] = ones

    col = 3 * hd2
    y = seg(col, NSA_HEADS * HEAD_DIM)
    for c in range(NSA_HEADS // 2):
        slab = _rope(y[:, c * LANES:(c + 1) * LANES], tables) * scale
        qn_ref[:, (2 * c) * LANES:(2 * c + 1) * LANES] = jnp.where(low, slab, 0.0).astype(bf16)
        qn_ref[:, (2 * c + 1) * LANES:(2 * c + 2) * LANES] = jnp.where(
            low, pltpu.roll(slab, HEAD_DIM, 1), 0.0).astype(bf16)
    col += NSA_HEADS * HEAD_DIM

    kc_ref[...] = seg(col, LANES)
    vc_ref[...] = seg(col + LANES, LANES)

    row = lax.broadcasted_iota(jnp.int32, (tm, LANES), 0) + lax.rem(pl.program_id(0) * tm, seq)
    onehot = jnp.where(lane - HEAD_DIM == row // SLC_BLOCK, 1.0, 0.0)
    y = _rope(seg(col + 2 * LANES, LANES), tables)
    ks_ref[0] = jnp.where(low, y, onehot).astype(bf16)
    ks_ref[1] = jnp.where(low, pltpu.roll(y, HEAD_DIM, 1), onehot).astype(bf16)
    y = seg(col + 3 * LANES, LANES)
    vs_ref[0] = jnp.where(low, y, 1.0).astype(bf16)
    vs_ref[1] = jnp.where(low, pltpu.roll(y, HEAD_DIM, 1), 1.0).astype(bf16)
    y = _rope(seg(col + 4 * LANES, LANES), tables)
    kw_ref[0] = jnp.where(low, y, 0.0).astype(bf16)
    kw_ref[1] = jnp.where(low, pltpu.roll(y, HEAD_DIM, 1), 0.0).astype(bf16)
    y = seg(col + 5 * LANES, LANES)
    vw_ref[0] = jnp.where(low, y, 1.0).astype(bf16)
    vw_ref[1] = jnp.where(low, pltpu.roll(y, HEAD_DIM, 1), 1.0).astype(bf16)

    gate = jax.nn.sigmoid(seg(col + 6 * LANES, LANES))
    g_ref[0] = gate
    g_ref[1] = pltpu.roll(gate, LANES - NSA_GQA * 3, 1)


def _proj(x2, pos_col, w_pad, consts, seq):
    n, d_model = x2.shape
    tm = PROJ_ROWS
    hd2 = DIFF_HEADS * 2 * HEAD_DIM
    row_spec = lambda w: pl.BlockSpec((tm, w), lambda i: (i, 0))
    grp_spec = pl.BlockSpec((NSA_KV_GROUPS, tm, LANES), lambda i: (0, i, 0))
    out_shape = (
        jax.ShapeDtypeStruct((n, hd2), bf16), jax.ShapeDtypeStruct((n, hd2), bf16),
        jax.ShapeDtypeStruct((n, 2 * hd2), bf16), jax.ShapeDtypeStruct((n, NSA_HEADS * LANES), bf16),
        jax.ShapeDtypeStruct((n, LANES), f32), jax.ShapeDtypeStruct((n, LANES), f32),
        jax.ShapeDtypeStruct((NSA_KV_GROUPS, n, LANES), bf16), jax.ShapeDtypeStruct((NSA_KV_GROUPS, n, LANES), bf16),
        jax.ShapeDtypeStruct((NSA_KV_GROUPS, n, LANES), bf16), jax.ShapeDtypeStruct((NSA_KV_GROUPS, n, LANES), bf16),
        jax.ShapeDtypeStruct((NSA_KV_GROUPS, n, LANES), f32),
    )
    out_specs = (row_spec(hd2), row_spec(hd2), row_spec(2 * hd2), row_spec(NSA_HEADS * LANES),
                 row_spec(LANES), row_spec(LANES), grp_spec, grp_spec, grp_spec, grp_spec, grp_spec)
    return pl.pallas_call(
        functools.partial(_proj_kernel, seq=seq),
        grid=(n // tm,),
        in_specs=[row_spec(d_model), pl.BlockSpec((tm, 1), lambda i: (i, 0)),
                  _const_spec(w_pad.shape), _const_spec(consts.shape)],
        out_specs=out_specs, out_shape=out_shape,
        compiler_params=_cparams(("parallel",)), name="proj",
    )(x2, pos_col, w_pad, consts)


def _compress_one(c_ref, pea_ref, peb_ref, w1a_ref, w1b_ref, b1_ref, w2_ref):
    ck = c_ref[0]
    n = ck.shape[0]
    hid_a = jnp.dot((ck + pea_ref[...]).astype(bf16), w1a_ref[...], preferred_element_type=f32)
    hid_b = jnp.dot((ck + peb_ref[...]).astype(bf16), w1b_ref[...], preferred_element_type=f32)
    hid = hid_a + pltpu.roll(hid_b, n - 1, 0) + b1_ref[...]
    hid = jax.nn.gelu(hid)
    return jnp.dot(hid.astype(bf16), w2_ref[...], preferred_element_type=f32)


def _compress_kernel(kc_ref, vc_ref, pos_ref, c_ref,
                     pak_ref, pbk_ref, w1ak_ref, w1bk_ref, b1k_ref, w2k_ref,
                     pav_ref, pbv_ref, w1av_ref, w1bv_ref, b1v_ref, w2v_ref,
                     kcmp_ref, vcmp_ref):
    k = _compress_one(kc_ref, pak_ref, pbk_ref, w1ak_ref, w1bk_ref, b1k_ref, w2k_ref)
    k = _rope(k, _rope_tables(pos_ref[0], c_ref))
    low = lax.broadcasted_iota(jnp.int32, k.shape, 1) < HEAD_DIM
    kcmp_ref[0, 0] = jnp.where(low, k, 0.0).astype(bf16)
    kcmp_ref[0, 1] = jnp.where(low, pltpu.roll(k, HEAD_DIM, 1), 0.0).astype(bf16)
    v = _compress_one(vc_ref, pav_ref, pbv_ref, w1av_ref, w1bv_ref, b1v_ref, w2v_ref)
    vcmp_ref[0, 0] = jnp.where(low, v, 0.0).astype(bf16)
    vcmp_ref[0, 1] = jnp.where(low, pltpu.roll(v, HEAD_DIM, 1), 0.0).astype(bf16)


def _compress_weights(pe, w1, b1, w2):
    r = CMP_BLOCK // CMP_STRIDE
    assert r == 2
    eye = jnp.eye(NSA_KV_GROUPS, dtype=f32)
    w1r = w1.reshape(CMP_BLOCK, HEAD_DIM, CMP_HIDDEN)
    per = CMP_STRIDE * NSA_KV_GROUPS * HEAD_DIM

    def big(part):
        return jnp.einsum('tcm,gh->tgchm', part, eye).reshape(per, NSA_KV_GROUPS * CMP_HIDDEN).astype(bf16)

    def pe_row(part):
        return jnp.broadcast_to(part[:, None, :], (CMP_STRIDE, NSA_KV_GROUPS, HEAD_DIM)).reshape(1, per)

    w2b = jnp.einsum('mc,gh->gmhc', w2, eye).reshape(NSA_KV_GROUPS * CMP_HIDDEN, NSA_KV_GROUPS * HEAD_DIM)
    return (pe_row(pe[:CMP_STRIDE]), pe_row(pe[CMP_STRIDE:]), big(w1r[:CMP_STRIDE]), big(w1r[CMP_STRIDE:]),
            jnp.tile(b1, NSA_KV_GROUPS).reshape(1, -1), w2b.astype(bf16))


def _compress(kc, vc, cmp_pos, consts, wk, wv):
    b, nch, per = kc.shape
    seq_spec = pl.BlockSpec((1, nch, per), lambda i: (i, 0, 0))
    w_specs = [_const_spec(w.shape) for w in wk + wv]
    return pl.pallas_call(
        _compress_kernel,
        grid=(b,),
        in_specs=[seq_spec, seq_spec, pl.BlockSpec((1, nch, 1), lambda i: (i, 0, 0)),
                  _const_spec(consts.shape)] + w_specs,
        out_specs=(pl.BlockSpec((1, NSA_KV_GROUPS, nch, LANES), lambda i: (i, 0, 0, 0)),
                   pl.BlockSpec((1, NSA_KV_GROUPS, nch, LANES), lambda i: (i, 0, 0, 0))),
        out_shape=(jax.ShapeDtypeStruct((b, NSA_KV_GROUPS, nch, LANES), bf16),
                   jax.ShapeDtypeStruct((b, NSA_KV_GROUPS, nch, LANES), bf16)),
        compiler_params=_cparams(("parallel",)), name="compress",
    )(kc, vc, cmp_pos, consts, *wk, *wv)


def _softmax_init(m_ref, acc_ref):
    m_ref[...] = jnp.full(m_ref.shape, NEG_INF, f32)
    acc_ref[...] = jnp.zeros(acc_ref.shape, f32)


def _softmax_step(s, v_ones, m_ref, acc_ref, rs):
    m_prev = m_ref[rs, :]
    m_new = jnp.maximum(m_prev, jnp.max(s, axis=1, keepdims=True))
    alpha = jnp.exp2(m_prev - m_new)
    p = jnp.exp2((s - jnp.concatenate([m_new] * (s.shape[1] // LANES), axis=1)).astype(bf16))
    pv = jnp.dot(p, v_ones, preferred_element_type=f32)
    acc_ref[rs, :] = jnp.concatenate([alpha] * (acc_ref.shape[1] // LANES), axis=1) * acc_ref[rs, :] + pv
    m_ref[rs, :] = m_new


def _scores(q, k):
    return lax.dot_general(q, k, (((1,), (1,)), ((), ())), preferred_element_type=f32)


def _run_steps(steps, m_ref, acc_ref):
    s_next = steps[0][0]()
    for n, (_, v_ones_fn, rows) in enumerate(steps):
        s = s_next
        if n + 1 < len(steps):
            s_next = steps[n + 1][0]()
        _softmax_step(s, v_ones_fn(), m_ref, acc_ref, rows)


def _diff_kernel(q_ref, k_ref, v_ref, lam_ref, g_ref, o_ref, q2_ref, m_ref, acc_ref, *, lam_init):
    tq = q_ref.shape[1]
    tk = tq
    rc = min(SOFTMAX_ROWS, tq)
    vdim = o_ref.shape[2]
    i = pl.program_id(2)
    q = q_ref[0]
    low = lax.broadcasted_iota(jnp.int32, q.shape, 1) < HEAD_DIM
    zero = jnp.zeros_like(q)
    q2_ref[0:tq, :] = jnp.where(low, q, zero)
    q2_ref[tq:2 * tq, :] = jnp.where(low, zero, q)
    _softmax_init(m_ref, acc_ref)

    def tile_steps(j, masked):
        off = pl.multiple_of(j * tk, tk)
        steps = []
        for c in range(2 * tq // rc):
            q_off = (c * rc) % tq
            cols = min(tk, q_off + rc) if masked else tk

            def score(c=c, q_off=q_off, cols=cols):
                s = _scores(q2_ref[c * rc:(c + 1) * rc, :], k_ref[0, pl.ds(off, cols), :])
                if masked:
                    r = q_off + lax.broadcasted_iota(jnp.int32, (rc, cols), 0)
                    s = jnp.where(r >= lax.broadcasted_iota(jnp.int32, (rc, cols), 1), s, NEG_INF)
                return s

            steps.append((score, lambda cols=cols: v_ref[0, pl.ds(off, cols), :], slice(c * rc, (c + 1) * rc)))
        return steps

    def pair_body(j, carry):
        _run_steps(tile_steps(2 * j, False) + tile_steps(2 * j + 1, False), m_ref, acc_ref)
        return carry

    lax.fori_loop(0, i // 2, pair_body, 0)

    @pl.when(lax.rem(i, 2) == 0)
    def _():
        _run_steps(tile_steps(i, True), m_ref, acc_ref)

    @pl.when(lax.rem(i, 2) == 1)
    def _():
        _run_steps(tile_steps(i - 1, False) + tile_steps(i, True), m_ref, acc_ref)

    lam_v = lam_ref[...]
    lam = (jnp.exp(jnp.sum(lam_v[0:1] * lam_v[1:2], axis=1, keepdims=True))
           - jnp.exp(jnp.sum(lam_v[2:3] * lam_v[3:4], axis=1, keepdims=True)) + lam_init)
    o = acc_ref[:, :vdim] / acc_ref[:, vdim:]
    od = o[:tq] - lam * o[tq:]
    od = od * lax.rsqrt(jnp.mean(od * od, axis=-1, keepdims=True) + RMS_EPS)
    o_ref[0] = (od * g_ref[...] * (1.0 - lam_init)).astype(o_ref.dtype)


def _diff_attention(qd, kd, vd, lam_vec, diff_g, lam_init):
    b, s, _ = qd.shape
    tq = min(ATTN_TILE, s)
    vdim = vd.shape[2] // DIFF_HEADS // 2
    assert vdim == LANES
    return pl.pallas_call(
        functools.partial(_diff_kernel, lam_init=lam_init),
        grid=(b, DIFF_HEADS, s // tq),
        in_specs=[pl.BlockSpec((1, tq, LANES), lambda bi, h, i: (bi, i, h)),
                  pl.BlockSpec((1, s, LANES), lambda bi, h, i: (bi, 0, h)),
                  pl.BlockSpec((1, s, 2 * vdim), lambda bi, h, i: (bi, 0, h)),
                  _const_spec(lam_vec.shape), _const_spec(diff_g.shape)],
        out_specs=pl.BlockSpec((1, tq, vdim), lambda bi, h, i: (bi, i, h)),
        out_shape=jax.ShapeDtypeStruct((b, s, DIFF_HEADS * vdim), bf16),
        scratch_shapes=[pltpu.VMEM((2 * tq, LANES), bf16), pltpu.VMEM((2 * tq, LANES), f32),
                        pltpu.VMEM((2 * tq, 2 * vdim), f32)],
        compiler_params=_cparams(("parallel", "parallel", "arbitrary")), name="diff_attn",
    )(qd, kd, vd, lam_vec, diff_g)


def _nsa_kernel(q_ref, kc_ref, vc_ref, ks_ref, vs_ref, kw_ref, vw_ref, g_ref, ov_ref, o_ref,
                qa_ref, m_ref, acc_ref, oc_ref, osel_ref, imp_ref, rank_ref, gate_ref, *, top_k):
    tq = q_ref.shape[1]
    seq = ks_ref.shape[2]
    tk = min(NSA_K_TILE, seq)
    ncp = kc_ref.shape[2]
    nslc = ov_ref.shape[0]
    rep = NSA_GQA
    i = pl.program_id(2)
    s0 = i * tq
    heads = [slice(r * tq, (r + 1) * tq) for r in range(rep)]

    def compressed_branch(width):
        t_col = s0 + lax.broadcasted_iota(jnp.int32, (tq, width), 0)
        cmp_end = lax.broadcasted_iota(jnp.int32, (tq, width), 1) * CMP_STRIDE + (CMP_BLOCK - 1)
        cmp_ok = cmp_end <= t_col
        kc = kc_ref[0, 0, 0:width, :]
        vc = vc_ref[0, 0, 0:width, :]
        psum = jnp.zeros((tq, width), f32)
        for r in range(rep):
            s = jnp.where(cmp_ok, _scores(q_ref[0, :, r * LANES:(r + 1) * LANES], kc), NEG_INF)
            p = jnp.where(cmp_ok, jnp.exp2(s - jnp.max(s, axis=1, keepdims=True)), 0.0)
            den = jnp.sum(p, axis=1, keepdims=True)
            p = p / jnp.where(den > 0.0, den, 1.0)
            psum = psum + p
            oc_ref[heads[r], :] = jnp.dot(p.astype(bf16), vc, preferred_element_type=f32)
        p_hi = psum.astype(bf16)
        p_lo = (psum - p_hi.astype(f32)).astype(bf16)
        ov = ov_ref[:, 0:width]
        imp_ref[...] = _scores(ov, p_hi) + _scores(ov, p_lo)
        gates = g_ref[0, 0]
        for c in range(3 * rep):
            gate_ref[c] = jnp.broadcast_to(gates[:, c:c + 1], (tq, LANES))

    half = ncp // 2
    if half % LANES == 0:
        fits_half = (i + 1) * (tq // CMP_STRIDE) <= half

        @pl.when(fits_half)
        def _():
            compressed_branch(half)

        @pl.when(jnp.logical_not(fits_half))
        def _():
            compressed_branch(ncp)
    else:
        compressed_branch(ncp)

    blk = lax.broadcasted_iota(jnp.int32, (nslc, tq), 0)
    cur = (s0 + lax.broadcasted_iota(jnp.int32, (nslc, tq), 1)) // SLC_BLOCK
    imp = jnp.where(blk > cur, -1.0, imp_ref[...])
    imp = jnp.where((blk == 0) | (blk == cur) | (blk == cur - 1), FORCED_SCORE, imp)
    rank_ref[...] = jnp.zeros((nslc, tq), f32)
    per_tile = tq // SLC_BLOCK
    for first in range(0, nslc, per_tile):

        @pl.when(first <= i * per_tile)
        def _():
            rank = rank_ref[...]
            for jp in range(first, min(first + per_tile, nslc)):
                other = imp[jp:jp + 1, :]
                ahead = jnp.where(blk > jp, jnp.where(other >= imp, 1.0, 0.0), jnp.where(other > imp, 1.0, 0.0))
                rank = rank + ahead
            rank_ref[...] = rank

    bias_t = jnp.where(rank_ref[...] < top_k, 0.0, SEL_BIAS)
    pieces = [jnp.zeros((HEAD_DIM, tq), f32), bias_t]
    if nslc < LANES - HEAD_DIM:
        pieces.append(jnp.zeros((LANES - HEAD_DIM - nslc, tq), f32))
    bias = jnp.concatenate(pieces, axis=0).T.astype(bf16)
    for r in range(rep):
        qa_ref[heads[r], :] = q_ref[0, :, r * LANES:(r + 1) * LANES] + bias

    def tile_steps(q_of, k_ref, v_ref, back, width, lo=None, hi=None):
        off = pl.multiple_of(s0 - back, tq)
        steps = []
        for r in range(rep):

            def score(r=r):
                s = _scores(q_of(r), k_ref[0, 0, pl.ds(off, width), :])
                if lo is None and hi is None:
                    return s
                d = (lax.broadcasted_iota(jnp.int32, (tq, width), 1)
                     - lax.broadcasted_iota(jnp.int32, (tq, width), 0))
                if hi is not None:
                    s = jnp.where(d <= hi, s, NEG_INF)
                if lo is not None:
                    s = jnp.where(d > lo, s, NEG_INF)
                return s

            steps.append((score, lambda: v_ref[0, 0, pl.ds(off, width), :], heads[r]))
        return steps

    def normalized():
        acc = acc_ref[...]
        low = lax.broadcasted_iota(jnp.int32, acc.shape, 1) < HEAD_DIM
        return acc / jnp.where(low, pltpu.roll(acc, HEAD_DIM, 1), 1.0)

    _softmax_init(m_ref, acc_ref)
    sel = functools.partial(tile_steps, lambda r: qa_ref[heads[r], :], ks_ref, vs_ref)

    def sel_body(j, carry):
        back = s0 - j * (2 * tk)
        _run_steps(sel(back, tk) + sel(back - tk, tk), m_ref, acc_ref)
        return carry

    lax.fori_loop(0, s0 // (2 * tk), sel_body, 0)
    tail = lax.rem(i, 2 * tk // tq)
    tails = [sel(0, tq, hi=0),
             sel(tq, tk, hi=tq),
             sel(tk, tk) + sel(0, tq, hi=0),
             sel(tk + tq, tk) + sel(tq, tk, hi=tq)]
    for n, steps in enumerate(tails):

        @pl.when(tail == n)
        def _():
            _run_steps(steps, m_ref, acc_ref)

    osel_ref[...] = normalized()

    _softmax_init(m_ref, acc_ref)
    win = functools.partial(tile_steps, lambda r: q_ref[0, :, r * LANES:(r + 1) * LANES], kw_ref, vw_ref)
    n_back = WINDOW // tq

    @pl.when(i >= n_back)
    def _():
        _run_steps(win(WINDOW, WINDOW + tq, lo=0, hi=WINDOW), m_ref, acc_ref)

    for n in range(n_back):

        @pl.when(i == n)
        def _():
            _run_steps(win(n * tq, (n + 1) * tq, hi=n * tq), m_ref, acc_ref)

    o_win = normalized()

    lane = lax.broadcasted_iota(jnp.int32, (tq, LANES), 1)
    gated = []
    for r in range(rep):
        gated.append(gate_ref[3 * r] * oc_ref[heads[r], :] + gate_ref[3 * r + 1] * osel_ref[heads[r], :]
                     + gate_ref[3 * r + 2] * o_win[heads[r]])
    for c in range(rep // 2):
        o_ref[0, :, c * LANES:(c + 1) * LANES] = jnp.where(
            lane < HEAD_DIM, gated[2 * c], pltpu.roll(gated[2 * c + 1], HEAD_DIM, 1)).astype(o_ref.dtype)


def _nsa_attention(qn, kcmp, vcmp, ksa, vsa, kwa, vwa, gates, overlap_t, top_k):
    b, s, _ = qn.shape
    tq = min(NSA_Q_TILE, s)
    ncp = kcmp.shape[2]
    rep = NSA_GQA
    assert WINDOW == 2 * tq and min(NSA_K_TILE, s) == 2 * tq
    cmp_spec = pl.BlockSpec((1, 1, ncp, LANES), lambda bi, g, i: (bi, g, 0, 0))
    seq_spec = pl.BlockSpec((1, 1, s, LANES), lambda bi, g, i: (g, bi, 0, 0))
    rows = rep * tq
    return pl.pallas_call(
        functools.partial(_nsa_kernel, top_k=top_k),
        grid=(b, NSA_KV_GROUPS, s // tq),
        in_specs=[pl.BlockSpec((1, tq, rep * LANES), lambda bi, g, i: (bi, i, g)),
                  cmp_spec, cmp_spec, seq_spec, seq_spec, seq_spec, seq_spec,
                  pl.BlockSpec((1, 1, tq, LANES), lambda bi, g, i: (g, bi, i, 0)),
                  _const_spec(overlap_t.shape)],
        out_specs=pl.BlockSpec((1, tq, rep * HEAD_DIM), lambda bi, g, i: (bi, i, g)),
        out_shape=jax.ShapeDtypeStruct((b, s, NSA_HEADS * HEAD_DIM), bf16),
        scratch_shapes=[pltpu.VMEM((rows, LANES), bf16), pltpu.VMEM((rows, LANES), f32),
                        pltpu.VMEM((rows, LANES), f32), pltpu.VMEM((rows, LANES), f32),
                        pltpu.VMEM((rows, LANES), f32), pltpu.VMEM(overlap_t.shape[:1] + (tq,), f32),
                        pltpu.VMEM(overlap_t.shape[:1] + (tq,), f32), pltpu.VMEM((3 * rep, tq, LANES), f32)],
        compiler_params=_cparams(("parallel", "parallel", "arbitrary")), name="nsa_attn",
    )(qn, kcmp, vcmp, ksa, vsa, kwa, vwa, gates, overlap_t)


def _outproj_kernel(x_ref, od_ref, on_ref, wa_ref, wb_ref, g_ref, b_ref, o_ref, *, alpha):
    y = (alpha * x_ref[...]
         + jnp.dot(od_ref[...], wa_ref[...], preferred_element_type=f32)
         + jnp.dot(on_ref[...], wb_ref[...], preferred_element_type=f32))
    o_ref[...] = _layer_norm(y, g_ref[...], b_ref[...])


def _outproj(x2, od, on, wa, wb, g, b, alpha):
    n, d_model = x2.shape
    tm = DENSE_ROWS
    row_spec = lambda w: pl.BlockSpec((tm, w), lambda i: (i, 0))
    return pl.pallas_call(
        functools.partial(_outproj_kernel, alpha=alpha),
        grid=(n // tm,),
        in_specs=[row_spec(d_model), row_spec(od.shape[1]), row_spec(on.shape[1]),
                  _const_spec(wa.shape), _const_spec(wb.shape), _const_spec(g.shape), _const_spec(b.shape)],
        out_specs=row_spec(d_model),
        out_shape=jax.ShapeDtypeStruct((n, d_model), f32),
        compiler_params=_cparams(("parallel",)), name="outproj_ln",
    )(x2, od, on, wa, wb, g, b)


def _ffn_kernel(h_ref, halo_ref, wu_ref, cw_ref, cb_ref, wd_ref, g_ref, b_ref, o_ref, u_ref, acc_ref,
                *, alpha, tiles_per_seq):
    tm = h_ref.shape[0]
    tf = FFN_CHUNK
    n_chunks = wd_ref.shape[0] // tf
    h = h_ref[...]
    first = lax.rem(pl.program_id(0), tiles_per_seq) == 0
    halo = jnp.where(first, 0.0, halo_ref[...])
    hb = jnp.concatenate([halo, h], axis=0).astype(bf16)
    for c in range(n_chunks):
        cs = slice(c * 2 * tf, (c + 1) * 2 * tf)
        u_ref[...] = jnp.dot(hb, wu_ref[:, cs], preferred_element_type=f32)
        u = cb_ref[:, cs]
        for k in range(CONV_WIDTH):
            u = u + cw_ref[k:k + 1, cs] * u_ref[pl.ds(HALO - (CONV_WIDTH - 1) + k, tm), :]
        act = (jax.nn.silu(u[:, :tf]) * u[:, tf:]).astype(bf16)
        part = jnp.dot(act, wd_ref[c * tf:(c + 1) * tf, :], preferred_element_type=f32)
        if c == 0:
            acc_ref[...] = part
        else:
            acc_ref[...] += part
    o_ref[...] = _layer_norm(alpha * h + acc_ref[...], g_ref[...], b_ref[...])


def _ffn(h1, wu, cw, cb, wd, g, b, alpha, seq):
    n, d_model = h1.shape
    tm = min(DENSE_ROWS, seq)
    single = dict(pipeline_mode=pl.Buffered(1))
    return pl.pallas_call(
        functools.partial(_ffn_kernel, alpha=alpha, tiles_per_seq=seq // tm),
        grid=(n // tm,),
        in_specs=[pl.BlockSpec((tm, d_model), lambda i: (i, 0)),
                  pl.BlockSpec((HALO, d_model), lambda i: (jnp.maximum(i * (tm // HALO) - 1, 0), 0)),
                  pl.BlockSpec(wu.shape, lambda i: (0, 0), **single),
                  _const_spec(cw.shape), _const_spec(cb.shape),
                  pl.BlockSpec(wd.shape, lambda i: (0, 0), **single),
                  _const_spec(g.shape), _const_spec(b.shape)],
        out_specs=pl.BlockSpec((tm, d_model), lambda i: (i, 0)),
        out_shape=jax.ShapeDtypeStruct((n, d_model), f32),
        scratch_shapes=[pltpu.VMEM((HALO + tm, 2 * FFN_CHUNK), f32), pltpu.VMEM((tm, d_model), f32)],
        compiler_params=_cparams(("parallel",)), name="ffn_ln",
    )(h1, h1, wu, cw, cb, wd, g, b)


def _interleave_gate_value(a, d_ff):
    lead = a.shape[:-1]
    a = a.reshape(lead + (2, d_ff // FFN_CHUNK, FFN_CHUNK))
    return jnp.swapaxes(a, -3, -2).reshape(lead + (2 * d_ff,))


def kernel(x, positions, w_in, lambda_q1, lambda_k1, lambda_q2, lambda_k2, diff_norm_g, cmp_pe_k, cmp_w1_k, cmp_b1_k, cmp_w2_k, cmp_pe_v, cmp_w1_v, cmp_b1_v, cmp_w2_v, w_out, ln1_g, ln1_b, w_up, conv_w, conv_b, w_down, ln2_g, ln2_b):
    b, s, d_model = x.shape
    depth = w_in.shape[0]
    n = b * s
    d_ff = w_down.shape[1]
    assert s % ATTN_TILE == 0 or s < ATTN_TILE
    assert s % SLC_BLOCK == 0 and s // SLC_BLOCK <= LANES - HEAD_DIM and d_ff % FFN_CHUNK == 0
    alpha = (2 * depth) ** 0.25
    consts = _rope_consts()
    pos_col = positions.reshape(n, 1)

    n_chunk = s // CMP_STRIDE
    n_cmp = n_chunk - CMP_BLOCK // CMP_STRIDE + 1
    n_slc = s // SLC_BLOCK
    top_k = min(SLC_TOPK, n_slc)
    cmp_pos = positions[:, CMP_BLOCK - 1::CMP_STRIDE][:, :n_cmp]
    cmp_pos = jnp.pad(cmp_pos, ((0, 0), (0, n_chunk - n_cmp))).reshape(b, n_chunk, 1)
    cs = np.arange(n_chunk)[None, :] * CMP_STRIDE
    ss = np.arange(n_slc)[:, None] * SLC_BLOCK
    ov = np.clip(np.minimum(cs + CMP_BLOCK, ss + SLC_BLOCK) - np.maximum(cs, ss), 0, None) / CMP_BLOCK
    ov[:, n_cmp:] = 0.0
    overlap_t = jnp.asarray(ov, dtype=bf16)

    h = x.reshape(n, d_model)
    for l in range(depth):
        lam_init = 0.8 - 0.6 * math.exp(-0.3 * l)
        d_in = w_in.shape[2]
        w_pad = jnp.pad(w_in[l], ((0, 0), (0, -d_in % LANES))).astype(bf16)
        qd, kd, vd, qn, kc, vc, ksa, vsa, kwa, vwa, gates = _proj(h, pos_col, w_pad, consts, s)

        per = CMP_STRIDE * NSA_KV_GROUPS * HEAD_DIM
        kcmp, vcmp = _compress(
            kc.reshape(b, n_chunk, per), vc.reshape(b, n_chunk, per), cmp_pos, consts,
            _compress_weights(cmp_pe_k[l], cmp_w1_k[l], cmp_b1_k[l], cmp_w2_k[l]),
            _compress_weights(cmp_pe_v[l], cmp_w1_v[l], cmp_b1_v[l], cmp_w2_v[l]))

        lam_vec = jnp.stack([lambda_q1[l], lambda_k1[l], lambda_q2[l], lambda_k2[l]]).astype(f32)
        od = _diff_attention(qd.reshape(b, s, -1), kd.reshape(b, s, -1), vd.reshape(b, s, -1),
                             lam_vec, diff_norm_g[l].reshape(1, -1).astype(f32), lam_init)
        grouped = lambda a: a.reshape(NSA_KV_GROUPS, b, s, LANES)
        on = _nsa_attention(qn.reshape(b, s, -1), kcmp, vcmp, grouped(ksa), grouped(vsa), grouped(kwa),
                            grouped(vwa), grouped(gates), overlap_t, top_k)

        d_diff = od.shape[2]
        wo = w_out[l].astype(bf16)
        h = _outproj(h, od.reshape(n, -1), on.reshape(n, -1), wo[:d_diff], wo[d_diff:],
                     ln1_g[l].reshape(1, -1), ln1_b[l].reshape(1, -1), alpha)
        h = _ffn(h, _interleave_gate_value(w_up[l], d_ff).astype(bf16),
                 _interleave_gate_value(conv_w[l], d_ff), _interleave_gate_value(conv_b[l], d_ff).reshape(1, -1),
                 w_down[l].astype(bf16), ln2_g[l].reshape(1, -1), ln2_b[l].reshape(1, -1), alpha, s)
    return h.reshape(b, s, d_model)
```

```python
import functools
import math

import jax
import jax.numpy as jnp
import numpy as np
from jax import lax
from jax.experimental import pallas as pl
from jax.experimental.pallas import tpu as pltpu

f32 = jnp.float32
bf16 = jnp.bfloat16

LANES = 128
HEAD_DIM = 64
ROPE_DIM = HEAD_DIM // 4
ROPE_THETA = 500000.0
DIFF_HEADS = 4
NSA_HEADS = 8
NSA_KV_GROUPS = 2
NSA_GQA = NSA_HEADS // NSA_KV_GROUPS
CMP_BLOCK = 32
CMP_STRIDE = 16
CMP_HIDDEN = 2 * HEAD_DIM
SLC_BLOCK = 64
SLC_TOPK = 16
WINDOW = 512
CONV_WIDTH = 3
LN_EPS = 1e-5
RMS_EPS = 1e-5
NEG_INF = -1e30
SEL_BIAS = -1e9
FORCED_SCORE = 1e6
VMEM_LIMIT = 56 * 1024 * 1024

LOG2E = 1.4426950408889634
QK_SCALE = HEAD_DIM ** -0.5 * LOG2E

PROJ_ROWS = 512
ATTN_TILE = 1024
DIFF_K_TILE = 512
NSA_Q_TILE = 256
NSA_K_TILE = 512
SOFTMAX_ROWS = 256
DENSE_ROWS = 512
FFN_CHUNK = 256
HALO = 8


def _cparams(sem):
    return pltpu.CompilerParams(dimension_semantics=sem, vmem_limit_bytes=VMEM_LIMIT)


def _const_spec(shape):
    n = len(shape)
    return pl.BlockSpec(shape, lambda *_: (0,) * n)


def _layer_norm(y, g, b):
    mu = jnp.mean(y, axis=-1, keepdims=True)
    d = y - mu
    var = jnp.mean(d * d, axis=-1, keepdims=True)
    return d * lax.rsqrt(var + LN_EPS) * g + b


def _rope_consts():
    lane = np.arange(LANES)
    in_head = lane % HEAD_DIM
    half = ROPE_DIM // 2
    inv_freq = 1.0 / (ROPE_THETA ** (jnp.arange(half, dtype=f32) / half))
    c = jnp.zeros((8, LANES), f32)
    c = c.at[0].set(jnp.tile(inv_freq, LANES // half))
    c = c.at[1].set(jnp.asarray(in_head < ROPE_DIM, f32))
    c = c.at[2].set(jnp.asarray(np.where(in_head < half, -1.0, np.where(in_head < ROPE_DIM, 1.0, 0.0)), f32))
    c = c.at[3].set(jnp.asarray(in_head < half, f32))
    return c


def _rope_tables(pos_col, c_ref):
    ang = pos_col.astype(f32) * c_ref[0:1, :]
    cos_t = jnp.where(c_ref[1:2, :] > 0.0, jnp.cos(ang), 1.0)
    sin_t = jnp.sin(ang) * c_ref[2:3, :]
    return cos_t, sin_t, c_ref[3:4, :] > 0.0


def _rope(y, tables):
    cos_t, sin_t, first = tables
    half = ROPE_DIM // 2
    partner = jnp.where(first, pltpu.roll(y, LANES - half, 1), pltpu.roll(y, half, 1))
    return y * cos_t + partner * sin_t


def _proj_kernel(x_ref, pos_ref, w_ref, c_ref,
                 qd_ref, kd_ref, vd_ref, qn_ref, kc_ref, vc_ref, ks_ref, vs_ref, kw_ref, vw_ref, g_ref,
                 *, seq):
    tm = x_ref.shape[0]
    xb = x_ref[...].astype(bf16)
    tables = _rope_tables(pos_ref[...], c_ref)
    lane = lax.broadcasted_iota(jnp.int32, (tm, LANES), 1)
    low = lane < HEAD_DIM
    scale = QK_SCALE

    hd2 = DIFF_HEADS * 2 * HEAD_DIM

    def diff_q(y):
        for c in range(hd2 // LANES):
            sl = slice(c * LANES, (c + 1) * LANES)
            qd_ref[:, sl] = (_rope(y[:, sl], tables) * scale).astype(bf16)

    def diff_k(y):
        for c in range(hd2 // LANES):
            sl = slice(c * LANES, (c + 1) * LANES)
            kd_ref[:, sl] = _rope(y[:, sl], tables).astype(bf16)

    def diff_v(y):
        ones = jnp.ones((tm, LANES), bf16)
        for c in range(hd2 // LANES):
            vd_ref[:, (2 * c) * LANES:(2 * c + 1) * LANES] = y[:, c * LANES:(c + 1) * LANES].astype(bf16)
            vd_ref[:, (2 * c + 1) * LANES:(2 * c + 2) * LANES] = ones

    def nsa_q(y):
        for c in range(NSA_HEADS // 2):
            slab = _rope(y[:, c * LANES:(c + 1) * LANES], tables) * scale
            qn_ref[:, (2 * c) * LANES:(2 * c + 1) * LANES] = jnp.where(low, slab, 0.0).astype(bf16)
            qn_ref[:, (2 * c + 1) * LANES:(2 * c + 2) * LANES] = jnp.where(
                low, pltpu.roll(slab, HEAD_DIM, 1), 0.0).astype(bf16)

    def cmp_kv(y):
        kc_ref[...] = y[:, :LANES]
        vc_ref[...] = y[:, LANES:]

    def grouped(ref, y, fill):
        ref[0] = jnp.where(low, y, fill).astype(ref.dtype)
        ref[1] = jnp.where(low, pltpu.roll(y, HEAD_DIM, 1), fill).astype(ref.dtype)

    def sel_kv(y):
        row = lax.broadcasted_iota(jnp.int32, (tm, LANES), 0) + lax.rem(pl.program_id(0) * tm, seq)
        onehot = jnp.where(lane - HEAD_DIM == row // SLC_BLOCK, 1.0, 0.0)
        grouped(ks_ref, _rope(y[:, :LANES], tables), onehot)
        grouped(vs_ref, y[:, LANES:], 1.0)

    def win_kv(y):
        grouped(kw_ref, _rope(y[:, :LANES], tables), 0.0)
        grouped(vw_ref, y[:, LANES:], 1.0)

    def gate(y):
        sig = jax.nn.sigmoid(y)
        g_ref[0] = sig
        g_ref[1] = pltpu.roll(sig, LANES - NSA_GQA * 3, 1)

    widths = [hd2, hd2, hd2, NSA_HEADS * HEAD_DIM, 2 * LANES, 2 * LANES, 2 * LANES, LANES]
    epilogues = [diff_q, diff_k, diff_v, nsa_q, cmp_kv, sel_kv, win_kv, gate]
    cols = [sum(widths[:n]) for n in range(len(widths))]

    def seg(n):
        return jnp.dot(xb, w_ref[:, cols[n]:cols[n] + widths[n]], preferred_element_type=f32)

    y_next = seg(0)
    for n, epilogue in enumerate(epilogues):
        y = y_next
        if n + 1 < len(epilogues):
            y_next = seg(n + 1)
        epilogue(y)


def _proj(x2, pos_col, w_pad, consts, seq):
    n, d_model = x2.shape
    tm = PROJ_ROWS
    hd2 = DIFF_HEADS * 2 * HEAD_DIM
    row_spec = lambda w: pl.BlockSpec((tm, w), lambda i: (i, 0))
    grp_spec = pl.BlockSpec((NSA_KV_GROUPS, tm, LANES), lambda i: (0, i, 0))
    out_shape = (
        jax.ShapeDtypeStruct((n, hd2), bf16), jax.ShapeDtypeStruct((n, hd2), bf16),
        jax.ShapeDtypeStruct((n, 2 * hd2), bf16), jax.ShapeDtypeStruct((n, NSA_HEADS * LANES), bf16),
        jax.ShapeDtypeStruct((n, LANES), f32), jax.ShapeDtypeStruct((n, LANES), f32),
        jax.ShapeDtypeStruct((NSA_KV_GROUPS, n, LANES), bf16), jax.ShapeDtypeStruct((NSA_KV_GROUPS, n, LANES), bf16),
        jax.ShapeDtypeStruct((NSA_KV_GROUPS, n, LANES), bf16), jax.ShapeDtypeStruct((NSA_KV_GROUPS, n, LANES), bf16),
        jax.ShapeDtypeStruct((NSA_KV_GROUPS, n, LANES), f32),
    )
    out_specs = (row_spec(hd2), row_spec(hd2), row_spec(2 * hd2), row_spec(NSA_HEADS * LANES),
                 row_spec(LANES), row_spec(LANES), grp_spec, grp_spec, grp_spec, grp_spec, grp_spec)
    return pl.pallas_call(
        functools.partial(_proj_kernel, seq=seq),
        grid=(n // tm,),
        in_specs=[row_spec(d_model), pl.BlockSpec((tm, 1), lambda i: (i, 0)),
                  _const_spec(w_pad.shape), _const_spec(consts.shape)],
        out_specs=out_specs, out_shape=out_shape,
        compiler_params=_cparams(("parallel",)), name="proj",
    )(x2, pos_col, w_pad, consts)


def _compress_one(c_ref, pea_ref, peb_ref, w1a_ref, w1b_ref, b1_ref, w2_ref):
    ck = c_ref[0]
    n = ck.shape[0]
    hid_a = jnp.dot((ck + pea_ref[...]).astype(bf16), w1a_ref[...], preferred_element_type=f32)
    hid_b = jnp.dot((ck + peb_ref[...]).astype(bf16), w1b_ref[...], preferred_element_type=f32)
    hid = hid_a + pltpu.roll(hid_b, n - 1, 0) + b1_ref[...]
    hid = jax.nn.gelu(hid)
    return jnp.dot(hid.astype(bf16), w2_ref[...], preferred_element_type=f32)


def _compress_kernel(kc_ref, vc_ref, pos_ref, c_ref,
                     pak_ref, pbk_ref, w1ak_ref, w1bk_ref, b1k_ref, w2k_ref,
                     pav_ref, pbv_ref, w1av_ref, w1bv_ref, b1v_ref, w2v_ref,
                     kcmp_ref, vcmp_ref):
    k = _compress_one(kc_ref, pak_ref, pbk_ref, w1ak_ref, w1bk_ref, b1k_ref, w2k_ref)
    k = _rope(k, _rope_tables(pos_ref[0], c_ref))
    low = lax.broadcasted_iota(jnp.int32, k.shape, 1) < HEAD_DIM
    kcmp_ref[0, 0] = jnp.where(low, k, 0.0).astype(bf16)
    kcmp_ref[0, 1] = jnp.where(low, pltpu.roll(k, HEAD_DIM, 1), 0.0).astype(bf16)
    v = _compress_one(vc_ref, pav_ref, pbv_ref, w1av_ref, w1bv_ref, b1v_ref, w2v_ref)
    vcmp_ref[0, 0] = jnp.where(low, v, 0.0).astype(bf16)
    vcmp_ref[0, 1] = jnp.where(low, pltpu.roll(v, HEAD_DIM, 1), 0.0).astype(bf16)


def _compress_weights(pe, w1, b1, w2):
    r = CMP_BLOCK // CMP_STRIDE
    assert r == 2
    eye = jnp.eye(NSA_KV_GROUPS, dtype=f32)
    w1r = w1.reshape(CMP_BLOCK, HEAD_DIM, CMP_HIDDEN)
    per = CMP_STRIDE * NSA_KV_GROUPS * HEAD_DIM

    def big(part):
        return jnp.einsum('tcm,gh->tgchm', part, eye).reshape(per, NSA_KV_GROUPS * CMP_HIDDEN).astype(bf16)

    def pe_row(part):
        return jnp.broadcast_to(part[:, None, :], (CMP_STRIDE, NSA_KV_GROUPS, HEAD_DIM)).reshape(1, per)

    w2b = jnp.einsum('mc,gh->gmhc', w2, eye).reshape(NSA_KV_GROUPS * CMP_HIDDEN, NSA_KV_GROUPS * HEAD_DIM)
    return (pe_row(pe[:CMP_STRIDE]), pe_row(pe[CMP_STRIDE:]), big(w1r[:CMP_STRIDE]), big(w1r[CMP_STRIDE:]),
            jnp.tile(b1, NSA_KV_GROUPS).reshape(1, -1), w2b.astype(bf16))


def _compress(kc, vc, cmp_pos, consts, wk, wv):
    b, nch, per = kc.shape
    seq_spec = pl.BlockSpec((1, nch, per), lambda i: (i, 0, 0))
    w_specs = [_const_spec(w.shape) for w in wk + wv]
    return pl.pallas_call(
        _compress_kernel,
        grid=(b,),
        in_specs=[seq_spec, seq_spec, pl.BlockSpec((1, nch, 1), lambda i: (i, 0, 0)),
                  _const_spec(consts.shape)] + w_specs,
        out_specs=(pl.BlockSpec((1, NSA_KV_GROUPS, nch, LANES), lambda i: (i, 0, 0, 0)),
                   pl.BlockSpec((1, NSA_KV_GROUPS, nch, LANES), lambda i: (i, 0, 0, 0))),
        out_shape=(jax.ShapeDtypeStruct((b, NSA_KV_GROUPS, nch, LANES), bf16),
                   jax.ShapeDtypeStruct((b, NSA_KV_GROUPS, nch, LANES), bf16)),
        compiler_params=_cparams(("parallel",)), name="compress",
    )(kc, vc, cmp_pos, consts, *wk, *wv)


def _softmax_init(m_ref, acc_ref):
    m_ref[...] = jnp.full(m_ref.shape, NEG_INF, f32)
    acc_ref[...] = jnp.zeros(acc_ref.shape, f32)


def _softmax_step(s, v_ones, m_ref, acc_ref, rs):
    m_prev = m_ref[rs, :]
    m_new = jnp.maximum(m_prev, jnp.max(s, axis=1, keepdims=True))
    alpha = jnp.exp2(m_prev - m_new)
    p = jnp.exp2((s - jnp.concatenate([m_new] * (s.shape[1] // LANES), axis=1)).astype(bf16))
    pv = jnp.dot(p, v_ones, preferred_element_type=f32)
    acc_ref[rs, :] = jnp.concatenate([alpha] * (acc_ref.shape[1] // LANES), axis=1) * acc_ref[rs, :] + pv
    m_ref[rs, :] = m_new


def _scores(q, k):
    return lax.dot_general(q, k, (((1,), (1,)), ((), ())), preferred_element_type=f32)


def _run_steps(steps, m_ref, acc_ref):
    s_next = steps[0][0]()
    for n, (_, v_ones_fn, rows) in enumerate(steps):
        s = s_next
        if n + 1 < len(steps):
            s_next = steps[n + 1][0]()
        _softmax_step(s, v_ones_fn(), m_ref, acc_ref, rows)


def _diff_kernel(q_ref, k_ref, v_ref, lam_ref, g_ref, o_ref, q2_ref, m_ref, acc_ref, *, lam_init):
    tq = q_ref.shape[1]
    tk = min(DIFF_K_TILE, tq)
    per_q = tq // tk
    rc = min(SOFTMAX_ROWS, tq)
    vdim = o_ref.shape[2]
    i = pl.program_id(2)
    q = q_ref[0]
    low = lax.broadcasted_iota(jnp.int32, q.shape, 1) < HEAD_DIM
    zero = jnp.zeros_like(q)
    q2_ref[0:tq, :] = jnp.where(low, q, zero)
    q2_ref[tq:2 * tq, :] = jnp.where(low, zero, q)
    _softmax_init(m_ref, acc_ref)

    def tile_steps(j, k_off=None):
        off = pl.multiple_of(j * tk, tk)
        steps = []
        for c in range(2 * tq // rc):
            q_off = (c * rc) % tq
            cols = tk if k_off is None else min(tk, q_off + rc - k_off)
            if cols <= 0:
                continue
            masked = k_off is not None and q_off - k_off + 1 < cols

            def score(c=c, q_off=q_off, cols=cols, masked=masked):
                s = _scores(q2_ref[c * rc:(c + 1) * rc, :], k_ref[0, pl.ds(off, cols), :])
                if masked:
                    r = q_off + lax.broadcasted_iota(jnp.int32, (rc, cols), 0)
                    s = jnp.where(r >= k_off + lax.broadcasted_iota(jnp.int32, (rc, cols), 1), s, NEG_INF)
                return s

            steps.append((score, lambda cols=cols: v_ref[0, pl.ds(off, cols), :], slice(c * rc, (c + 1) * rc)))
        return steps

    group = 2 if per_q % 2 == 0 else 1

    def body(j, carry):
        _run_steps(sum((tile_steps(group * j + t) for t in range(group)), []), m_ref, acc_ref)
        return carry

    lax.fori_loop(0, i * (per_q // group), body, 0)
    _run_steps(sum((tile_steps(i * per_q + t, t * tk) for t in range(per_q)), []), m_ref, acc_ref)

    lam_v = lam_ref[...]
    lam = (jnp.exp(jnp.sum(lam_v[0:1] * lam_v[1:2], axis=1, keepdims=True))
           - jnp.exp(jnp.sum(lam_v[2:3] * lam_v[3:4], axis=1, keepdims=True)) + lam_init)
    o = acc_ref[:, :vdim] / acc_ref[:, vdim:]
    od = o[:tq] - lam * o[tq:]
    od = od * lax.rsqrt(jnp.mean(od * od, axis=-1, keepdims=True) + RMS_EPS)
    o_ref[0] = (od * g_ref[...] * (1.0 - lam_init)).astype(o_ref.dtype)


def _diff_attention(qd, kd, vd, lam_vec, diff_g, lam_init):
    b, s, _ = qd.shape
    tq = min(ATTN_TILE, s)
    vdim = vd.shape[2] // DIFF_HEADS // 2
    assert vdim == LANES
    return pl.pallas_call(
        functools.partial(_diff_kernel, lam_init=lam_init),
        grid=(b, DIFF_HEADS, s // tq),
        in_specs=[pl.BlockSpec((1, tq, LANES), lambda bi, h, i: (bi, i, h)),
                  pl.BlockSpec((1, s, LANES), lambda bi, h, i: (bi, 0, h)),
                  pl.BlockSpec((1, s, 2 * vdim), lambda bi, h, i: (bi, 0, h)),
                  _const_spec(lam_vec.shape), _const_spec(diff_g.shape)],
        out_specs=pl.BlockSpec((1, tq, vdim), lambda bi, h, i: (bi, i, h)),
        out_shape=jax.ShapeDtypeStruct((b, s, DIFF_HEADS * vdim), bf16),
        scratch_shapes=[pltpu.VMEM((2 * tq, LANES), bf16), pltpu.VMEM((2 * tq, LANES), f32),
                        pltpu.VMEM((2 * tq, 2 * vdim), f32)],
        compiler_params=_cparams(("parallel", "parallel", "arbitrary")), name="diff_attn",
    )(qd, kd, vd, lam_vec, diff_g)


def _nsa_kernel(q_ref, kc_ref, vc_ref, ks_ref, vs_ref, kw_ref, vw_ref, g_ref, ov_ref, o_ref,
                qa_ref, m_ref, acc_ref, mw_ref, accw_ref, oc_ref, imp_ref, rank_ref, gate_ref, *, top_k):
    tq = q_ref.shape[1]
    seq = ks_ref.shape[2]
    tk = min(NSA_K_TILE, seq)
    ncp = kc_ref.shape[2]
    nslc = ov_ref.shape[0]
    rep = NSA_GQA
    i = pl.program_id(2)
    s0 = i * tq
    heads = [slice(r * tq, (r + 1) * tq) for r in range(rep)]

    def tile_steps(q_of, k_ref, v_ref, back, width, lo=None, hi=None):
        off = pl.multiple_of(s0 - back, tq)
        steps = []
        for r in range(rep):

            def score(r=r):
                s = _scores(q_of(r), k_ref[0, 0, pl.ds(off, width), :])
                if lo is None and hi is None:
                    return s
                d = (lax.broadcasted_iota(jnp.int32, (tq, width), 1)
                     - lax.broadcasted_iota(jnp.int32, (tq, width), 0))
                if hi is not None:
                    s = jnp.where(d <= hi, s, NEG_INF)
                if lo is not None:
                    s = jnp.where(d > lo, s, NEG_INF)
                return s

            steps.append((score, lambda: v_ref[0, 0, pl.ds(off, width), :], heads[r]))
        return steps

    def compressed_branch(width, window_steps):
        t_col = s0 + lax.broadcasted_iota(jnp.int32, (tq, width), 0)
        cmp_end = lax.broadcasted_iota(jnp.int32, (tq, width), 1) * CMP_STRIDE + (CMP_BLOCK - 1)
        cmp_ok = cmp_end <= t_col
        kc = kc_ref[0, 0, 0:width, :]
        vc = vc_ref[0, 0, 0:width, :]
        raw = [_scores(q_ref[0, :, r * LANES:(r + 1) * LANES], kc) for r in range(rep)]
        _softmax_init(mw_ref, accw_ref)
        _run_steps(window_steps, mw_ref, accw_ref)
        psum = jnp.zeros((tq, width), f32)
        for r in range(rep):
            s = jnp.where(cmp_ok, raw[r], NEG_INF)
            p = jnp.where(cmp_ok, jnp.exp2(s - jnp.max(s, axis=1, keepdims=True)), 0.0)
            den = jnp.sum(p, axis=1, keepdims=True)
            p = p / jnp.where(den > 0.0, den, 1.0)
            psum = psum + p
            oc_ref[heads[r], :] = jnp.dot(p.astype(bf16), vc, preferred_element_type=f32)
        p_hi = psum.astype(bf16)
        p_lo = (psum - p_hi.astype(f32)).astype(bf16)
        ov = ov_ref[:, 0:width]
        imp_ref[...] = _scores(ov, p_hi) + _scores(ov, p_lo)
        gates = g_ref[0, 0]
        for c in range(3 * rep):
            gate_ref[c] = jnp.broadcast_to(gates[:, c:c + 1], (tq, LANES))

    win = functools.partial(tile_steps, lambda r: q_ref[0, :, r * LANES:(r + 1) * LANES], kw_ref, vw_ref)
    n_back = WINDOW // tq
    windows = [lambda n=n: win(n * tq, (n + 1) * tq, hi=n * tq) for n in range(n_back)]
    windows.append(lambda: win(WINDOW, WINDOW + tq, lo=0, hi=WINDOW))
    half = ncp // 2
    has_half = half % LANES == 0
    last_half_tile = half // (tq // CMP_STRIDE) - 1 if has_half else -1
    assert not has_half or last_half_tile >= n_back
    for n, window in enumerate(windows):
        this_window = (i == n) if n < n_back else (i >= n_back)
        if not has_half:
            pl.when(this_window)(functools.partial(lambda w: compressed_branch(ncp, w()), window))
        elif n < n_back:
            pl.when(this_window)(functools.partial(lambda w: compressed_branch(half, w()), window))
        else:
            pl.when(this_window & (i <= last_half_tile))(
                functools.partial(lambda w: compressed_branch(half, w()), window))
            pl.when(i > last_half_tile)(functools.partial(lambda w: compressed_branch(ncp, w()), window))

    blk = lax.broadcasted_iota(jnp.int32, (nslc, tq), 0)
    cur = (s0 + lax.broadcasted_iota(jnp.int32, (nslc, tq), 1)) // SLC_BLOCK
    imp = jnp.where(blk > cur, -1.0, imp_ref[...])
    imp = jnp.where((blk == 0) | (blk == cur) | (blk == cur - 1), FORCED_SCORE, imp)
    rank_ref[...] = jnp.zeros((nslc, tq), f32)
    per_tile = tq // SLC_BLOCK
    for first in range(0, nslc, per_tile):

        @pl.when(first <= i * per_tile)
        def _():
            rank = rank_ref[...]
            for jp in range(first, min(first + per_tile, nslc)):
                other = imp[jp:jp + 1, :]
                ahead = jnp.where(blk > jp, jnp.where(other >= imp, 1.0, 0.0), jnp.where(other > imp, 1.0, 0.0))
                rank = rank + ahead
            rank_ref[...] = rank

    bias_t = jnp.where(rank_ref[...] < top_k, 0.0, SEL_BIAS)
    pieces = [jnp.zeros((HEAD_DIM, tq), f32), bias_t]
    if nslc < LANES - HEAD_DIM:
        pieces.append(jnp.zeros((LANES - HEAD_DIM - nslc, tq), f32))
    bias = jnp.concatenate(pieces, axis=0).T.astype(bf16)
    for r in range(rep):
        qa_ref[heads[r], :] = q_ref[0, :, r * LANES:(r + 1) * LANES] + bias

    def normalized(ref):
        acc = ref[...]
        low = lax.broadcasted_iota(jnp.int32, acc.shape, 1) < HEAD_DIM
        return acc / jnp.where(low, pltpu.roll(acc, HEAD_DIM, 1), 1.0)

    _softmax_init(m_ref, acc_ref)
    sel = functools.partial(tile_steps, lambda r: qa_ref[heads[r], :], ks_ref, vs_ref)

    def sel_body(j, carry):
        back = s0 - j * (2 * tk)
        _run_steps(sel(back, tk) + sel(back - tk, tk), m_ref, acc_ref)
        return carry

    lax.fori_loop(0, s0 // (2 * tk), sel_body, 0)
    tail = lax.rem(i, 2 * tk // tq)
    tails = [sel(0, tq, hi=0),
             sel(tq, tk, hi=tq),
             sel(tk, tk) + sel(0, tq, hi=0),
             sel(tk + tq, tk) + sel(tq, tk, hi=tq)]
    for n, steps in enumerate(tails):

        @pl.when(tail == n)
        def _():
            _run_steps(steps, m_ref, acc_ref)

    o_sel = normalized(acc_ref)
    o_win = normalized(accw_ref)

    lane = lax.broadcasted_iota(jnp.int32, (tq, LANES), 1)
    gated = []
    for r in range(rep):
        gated.append(gate_ref[3 * r] * oc_ref[heads[r], :] + gate_ref[3 * r + 1] * o_sel[heads[r]]
                     + gate_ref[3 * r + 2] * o_win[heads[r]])
    for c in range(rep // 2):
        o_ref[0, :, c * LANES:(c + 1) * LANES] = jnp.where(
            lane < HEAD_DIM, gated[2 * c], pltpu.roll(gated[2 * c + 1], HEAD_DIM, 1)).astype(o_ref.dtype)


def _nsa_attention(qn, kcmp, vcmp, ksa, vsa, kwa, vwa, gates, overlap_t, top_k):
    b, s, _ = qn.shape
    tq = min(NSA_Q_TILE, s)
    ncp = kcmp.shape[2]
    rep = NSA_GQA
    assert WINDOW == 2 * tq and min(NSA_K_TILE, s) == 2 * tq
    cmp_spec = pl.BlockSpec((1, 1, ncp, LANES), lambda bi, g, i: (bi, g, 0, 0))
    seq_spec = pl.BlockSpec((1, 1, s, LANES), lambda bi, g, i: (g, bi, 0, 0))
    rows = rep * tq
    return pl.pallas_call(
        functools.partial(_nsa_kernel, top_k=top_k),
        grid=(b, NSA_KV_GROUPS, s // tq),
        in_specs=[pl.BlockSpec((1, tq, rep * LANES), lambda bi, g, i: (bi, i, g)),
                  cmp_spec, cmp_spec, seq_spec, seq_spec, seq_spec, seq_spec,
                  pl.BlockSpec((1, 1, tq, LANES), lambda bi, g, i: (g, bi, i, 0)),
                  _const_spec(overlap_t.shape)],
        out_specs=pl.BlockSpec((1, tq, rep * HEAD_DIM), lambda bi, g, i: (bi, i, g)),
        out_shape=jax.ShapeDtypeStruct((b, s, NSA_HEADS * HEAD_DIM), bf16),
        scratch_shapes=[pltpu.VMEM((rows, LANES), bf16)] + [pltpu.VMEM((rows, LANES), f32)] * 5 + [
                        pltpu.VMEM(overlap_t.shape[:1] + (tq,), f32),
                        pltpu.VMEM(overlap_t.shape[:1] + (tq,), f32), pltpu.VMEM((3 * rep, tq, LANES), f32)],
        compiler_params=_cparams(("parallel", "parallel", "arbitrary")), name="nsa_attn",
    )(qn, kcmp, vcmp, ksa, vsa, kwa, vwa, gates, overlap_t)


def _outproj_kernel(x_ref, od_ref, on_ref, wa_ref, wb_ref, g_ref, b_ref, o_ref, *, alpha):
    parts = 2
    rows = x_ref.shape[0] // parts

    def mix(r):
        rs = slice(r * rows, (r + 1) * rows)
        return (jnp.dot(od_ref[rs, :], wa_ref[...], preferred_element_type=f32)
                + jnp.dot(on_ref[rs, :], wb_ref[...], preferred_element_type=f32))

    y_next = mix(0)
    for r in range(parts):
        rs = slice(r * rows, (r + 1) * rows)
        y = y_next
        if r + 1 < parts:
            y_next = mix(r + 1)
        o_ref[rs, :] = _layer_norm(alpha * x_ref[rs, :] + y, g_ref[...], b_ref[...])


def _outproj(x2, od, on, wa, wb, g, b, alpha):
    n, d_model = x2.shape
    tm = 2 * DENSE_ROWS
    row_spec = lambda w: pl.BlockSpec((tm, w), lambda i: (i, 0))
    return pl.pallas_call(
        functools.partial(_outproj_kernel, alpha=alpha),
        grid=(n // tm,),
        in_specs=[row_spec(d_model), row_spec(od.shape[1]), row_spec(on.shape[1]),
                  _const_spec(wa.shape), _const_spec(wb.shape), _const_spec(g.shape), _const_spec(b.shape)],
        out_specs=row_spec(d_model),
        out_shape=jax.ShapeDtypeStruct((n, d_model), f32),
        compiler_params=_cparams(("parallel",)), name="outproj_ln",
    )(x2, od, on, wa, wb, g, b)


def _ffn_kernel(h_ref, halo_ref, wu_ref, cw_ref, cb_ref, wd_ref, g_ref, b_ref, o_ref, hb_ref, u_ref, act_ref,
                *, alpha, tiles_per_seq):
    tm = h_ref.shape[0]
    tf = FFN_CHUNK
    n_chunks = wd_ref.shape[0] // tf
    h = h_ref[...]
    first = lax.rem(pl.program_id(0), tiles_per_seq) == 0
    halo = jnp.where(first, 0.0, halo_ref[...])
    d_ff = wd_ref.shape[0]
    hb_ref[...] = jnp.concatenate([halo, h], axis=0).astype(bf16)

    def up(c):
        for part, col in enumerate((c * tf, d_ff + c * tf)):
            u_ref[c % 2, :, part * tf:(part + 1) * tf] = jnp.dot(
                hb_ref[...], wu_ref[:, col:col + tf], preferred_element_type=f32)

    def conv(c, part):
        col = part * d_ff + c * tf
        u = cb_ref[:, col:col + tf]
        for k in range(CONV_WIDTH):
            u = u + cw_ref[k:k + 1, col:col + tf] * u_ref[
                c % 2, pl.ds(HALO - (CONV_WIDTH - 1) + k, tm), part * tf:(part + 1) * tf]
        return u

    up(0)
    for c in range(n_chunks):
        if c + 1 < n_chunks:
            up(c + 1)
        act_ref[:, c * tf:(c + 1) * tf] = (jax.nn.silu(conv(c, 0)) * conv(c, 1)).astype(bf16)
    down = jnp.dot(act_ref[...], wd_ref[...], preferred_element_type=f32)
    o_ref[...] = _layer_norm(alpha * h + down, g_ref[...], b_ref[...])


def _ffn(h1, wu, cw, cb, wd, g, b, alpha, seq):
    n, d_model = h1.shape
    tm = min(DENSE_ROWS, seq)
    single = dict(pipeline_mode=pl.Buffered(1))
    return pl.pallas_call(
        functools.partial(_ffn_kernel, alpha=alpha, tiles_per_seq=seq // tm),
        grid=(n // tm,),
        in_specs=[pl.BlockSpec((tm, d_model), lambda i: (i, 0)),
                  pl.BlockSpec((HALO, d_model), lambda i: (jnp.maximum(i * (tm // HALO) - 1, 0), 0)),
                  pl.BlockSpec(wu.shape, lambda i: (0, 0), **single),
                  _const_spec(cw.shape), _const_spec(cb.shape),
                  pl.BlockSpec(wd.shape, lambda i: (0, 0), **single),
                  _const_spec(g.shape), _const_spec(b.shape)],
        out_specs=pl.BlockSpec((tm, d_model), lambda i: (i, 0)),
        out_shape=jax.ShapeDtypeStruct((n, d_model), f32),
        scratch_shapes=[pltpu.VMEM((HALO + tm, d_model), bf16), pltpu.VMEM((2, HALO + tm, 2 * FFN_CHUNK), f32),
                        pltpu.VMEM((tm, wd.shape[0]), bf16)],
        compiler_params=_cparams(("parallel",)), name="ffn_ln",
    )(h1, h1, wu, cw, cb, wd, g, b)


def kernel(x, positions, w_in, lambda_q1, lambda_k1, lambda_q2, lambda_k2, diff_norm_g, cmp_pe_k, cmp_w1_k, cmp_b1_k, cmp_w2_k, cmp_pe_v, cmp_w1_v, cmp_b1_v, cmp_w2_v, w_out, ln1_g, ln1_b, w_up, conv_w, conv_b, w_down, ln2_g, ln2_b):
    b, s, d_model = x.shape
    depth = w_in.shape[0]
    n = b * s
    d_ff = w_down.shape[1]
    assert s % ATTN_TILE == 0 or s < ATTN_TILE
    assert s % SLC_BLOCK == 0 and s // SLC_BLOCK <= LANES - HEAD_DIM and d_ff % FFN_CHUNK == 0
    alpha = (2 * depth) ** 0.25
    consts = _rope_consts()
    pos_col = positions.reshape(n, 1)

    n_chunk = s // CMP_STRIDE
    n_cmp = n_chunk - CMP_BLOCK // CMP_STRIDE + 1
    n_slc = s // SLC_BLOCK
    top_k = min(SLC_TOPK, n_slc)
    cmp_pos = positions[:, CMP_BLOCK - 1::CMP_STRIDE][:, :n_cmp]
    cmp_pos = jnp.pad(cmp_pos, ((0, 0), (0, n_chunk - n_cmp))).reshape(b, n_chunk, 1)
    cs = np.arange(n_chunk)[None, :] * CMP_STRIDE
    ss = np.arange(n_slc)[:, None] * SLC_BLOCK
    ov = np.clip(np.minimum(cs + CMP_BLOCK, ss + SLC_BLOCK) - np.maximum(cs, ss), 0, None) / CMP_BLOCK
    ov[:, n_cmp:] = 0.0
    overlap_t = jnp.asarray(ov, dtype=bf16)

    h = x.reshape(n, d_model)
    for l in range(depth):
        lam_init = 0.8 - 0.6 * math.exp(-0.3 * l)
        d_in = w_in.shape[2]
        w_pad = jnp.pad(w_in[l], ((0, 0), (0, -d_in % LANES))).astype(bf16)
        qd, kd, vd, qn, kc, vc, ksa, vsa, kwa, vwa, gates = _proj(h, pos_col, w_pad, consts, s)

        per = CMP_STRIDE * NSA_KV_GROUPS * HEAD_DIM
        kcmp, vcmp = _compress(
            kc.reshape(b, n_chunk, per), vc.reshape(b, n_chunk, per), cmp_pos, consts,
            _compress_weights(cmp_pe_k[l], cmp_w1_k[l], cmp_b1_k[l], cmp_w2_k[l]),
            _compress_weights(cmp_pe_v[l], cmp_w1_v[l], cmp_b1_v[l], cmp_w2_v[l]))

        lam_vec = jnp.stack([lambda_q1[l], lambda_k1[l], lambda_q2[l], lambda_k2[l]]).astype(f32)
        od = _diff_attention(qd.reshape(b, s, -1), kd.reshape(b, s, -1), vd.reshape(b, s, -1),
                             lam_vec, diff_norm_g[l].reshape(1, -1).astype(f32), lam_init)
        grouped = lambda a: a.reshape(NSA_KV_GROUPS, b, s, LANES)
        on = _nsa_attention(qn.reshape(b, s, -1), kcmp, vcmp, grouped(ksa), grouped(vsa), grouped(kwa),
                            grouped(vwa), grouped(gates), overlap_t, top_k)

        d_diff = od.shape[2]
        wo = w_out[l].astype(bf16)
        h = _outproj(h, od.reshape(n, -1), on.reshape(n, -1), wo[:d_diff], wo[d_diff:],
                     ln1_g[l].reshape(1, -1), ln1_b[l].reshape(1, -1), alpha)
        h = _ffn(h, w_up[l].astype(bf16), conv_w[l], conv_b[l].reshape(1, -1),
                 w_down[l].astype(bf16), ln2_g[l].reshape(1, -1), ln2_b[l].reshape(1, -1), alpha, s)
    return h.reshape(b, s, d_model)
```

```python
import functools
import math

import jax
import jax.numpy as jnp
import numpy as np
from jax import lax
from jax.experimental import pallas as pl
from jax.experimental.pallas import tpu as pltpu

f32 = jnp.float32
bf16 = jnp.bfloat16

LANES = 128
HEAD_DIM = 64
ROPE_DIM = HEAD_DIM // 4
ROPE_THETA = 500000.0
DIFF_HEADS = 4
NSA_HEADS = 8
NSA_KV_GROUPS = 2
NSA_GQA = NSA_HEADS // NSA_KV_GROUPS
CMP_BLOCK = 32
CMP_STRIDE = 16
CMP_HIDDEN = 2 * HEAD_DIM
SLC_BLOCK = 64
SLC_TOPK = 16
WINDOW = 512
CONV_WIDTH = 3
LN_EPS = 1e-5
RMS_EPS = 1e-5
NEG_INF = -1e30
SEL_BIAS = -1e9
FORCED_SCORE = 1e6
VMEM_LIMIT = 56 * 1024 * 1024

LOG2E = 1.4426950408889634
QK_SCALE = HEAD_DIM ** -0.5 * LOG2E

PROJ_ROWS = 512
ATTN_TILE = 4096
DIFF_K_TILE = 512
NSA_Q_TILE = 256
NSA_K_TILE = 512
NSA_TRIP_TILES = 4
SOFTMAX_ROWS = 256
DENSE_ROWS = 512
FFN_CHUNK = 256
HALO = 8


def _cparams(sem):
    return pltpu.CompilerParams(dimension_semantics=sem, vmem_limit_bytes=VMEM_LIMIT)


def _const_spec(shape):
    n = len(shape)
    return pl.BlockSpec(shape, lambda *_: (0,) * n)


def _layer_norm(y, g, b):
    mu = jnp.mean(y, axis=-1, keepdims=True)
    d = y - mu
    var = jnp.mean(d * d, axis=-1, keepdims=True)
    return d * lax.rsqrt(var + LN_EPS) * g + b


def _rope_consts():
    lane = np.arange(LANES)
    in_head = lane % HEAD_DIM
    half = ROPE_DIM // 2
    inv_freq = 1.0 / (ROPE_THETA ** (jnp.arange(half, dtype=f32) / half))
    c = jnp.zeros((8, LANES), f32)
    c = c.at[0].set(jnp.tile(inv_freq, LANES // half))
    c = c.at[1].set(jnp.asarray(in_head < ROPE_DIM, f32))
    c = c.at[2].set(jnp.asarray(np.where(in_head < half, -1.0, np.where(in_head < ROPE_DIM, 1.0, 0.0)), f32))
    c = c.at[3].set(jnp.asarray(in_head < half, f32))
    return c


def _rope_tables(pos_col, c_ref):
    ang = pos_col.astype(f32) * c_ref[0:1, :]
    cos_t = jnp.where(c_ref[1:2, :] > 0.0, jnp.cos(ang), 1.0)
    sin_t = jnp.sin(ang) * c_ref[2:3, :]
    return cos_t, sin_t, c_ref[3:4, :] > 0.0


def _rope(y, tables):
    cos_t, sin_t, first = tables
    half = ROPE_DIM // 2
    partner = jnp.where(first, pltpu.roll(y, LANES - half, 1), pltpu.roll(y, half, 1))
    return y * cos_t + partner * sin_t


def _proj_kernel(x_ref, pos_ref, w_ref, c_ref,
                 qd_ref, kd_ref, vd_ref, qn_ref, kc_ref, vc_ref, ks_ref, vs_ref, kw_ref, vw_ref, g_ref,
                 *, seq):
    tm = x_ref.shape[0]
    xb = x_ref[...].astype(bf16)
    tables = _rope_tables(pos_ref[...], c_ref)
    lane = lax.broadcasted_iota(jnp.int32, (tm, LANES), 1)
    low = lane < HEAD_DIM
    scale = QK_SCALE

    hd2 = DIFF_HEADS * 2 * HEAD_DIM

    def diff_q(y):
        for c in range(hd2 // LANES):
            sl = slice(c * LANES, (c + 1) * LANES)
            qd_ref[:, sl] = (_rope(y[:, sl], tables) * scale).astype(bf16)

    def diff_k(y):
        for c in range(hd2 // LANES):
            sl = slice(c * LANES, (c + 1) * LANES)
            kd_ref[:, sl] = _rope(y[:, sl], tables).astype(bf16)

    def diff_v(y):
        ones = jnp.ones((tm, LANES), bf16)
        for c in range(hd2 // LANES):
            vd_ref[:, (2 * c) * LANES:(2 * c + 1) * LANES] = y[:, c * LANES:(c + 1) * LANES].astype(bf16)
            vd_ref[:, (2 * c + 1) * LANES:(2 * c + 2) * LANES] = ones

    def nsa_q(y):
        for c in range(NSA_HEADS // 2):
            slab = _rope(y[:, c * LANES:(c + 1) * LANES], tables) * scale
            qn_ref[:, (2 * c) * LANES:(2 * c + 1) * LANES] = jnp.where(low, slab, 0.0).astype(bf16)
            qn_ref[:, (2 * c + 1) * LANES:(2 * c + 2) * LANES] = jnp.where(
                low, pltpu.roll(slab, HEAD_DIM, 1), 0.0).astype(bf16)

    def cmp_kv(y):
        kc_ref[...] = y[:, :LANES]
        vc_ref[...] = y[:, LANES:]

    def grouped(ref, y, fill):
        ref[0] = jnp.where(low, y, fill).astype(ref.dtype)
        ref[1] = jnp.where(low, pltpu.roll(y, HEAD_DIM, 1), fill).astype(ref.dtype)

    def sel_kv(y):
        row = lax.broadcasted_iota(jnp.int32, (tm, LANES), 0) + lax.rem(pl.program_id(0) * tm, seq)
        onehot = jnp.where(lane - HEAD_DIM == row // SLC_BLOCK, 1.0, 0.0)
        grouped(ks_ref, _rope(y[:, :LANES], tables), onehot)
        grouped(vs_ref, y[:, LANES:], 1.0)

    def win_kv(y):
        grouped(kw_ref, _rope(y[:, :LANES], tables), 0.0)
        grouped(vw_ref, y[:, LANES:], 1.0)

    def gate(y):
        sig = jax.nn.sigmoid(y)
        g_ref[0] = sig
        g_ref[1] = pltpu.roll(sig, LANES - NSA_GQA * 3, 1)

    widths = [hd2, hd2, hd2, NSA_HEADS * HEAD_DIM, 2 * LANES, 2 * LANES, 2 * LANES, LANES]
    epilogues = [diff_q, diff_k, diff_v, nsa_q, cmp_kv, sel_kv, win_kv, gate]
    cols = [sum(widths[:n]) for n in range(len(widths))]

    def seg(n):
        return jnp.dot(xb, w_ref[:, cols[n]:cols[n] + widths[n]], preferred_element_type=f32)

    y_next = seg(0)
    for n, epilogue in enumerate(epilogues):
        y = y_next
        if n + 1 < len(epilogues):
            y_next = seg(n + 1)
        epilogue(y)


def _proj(x2, pos_col, w_pad, consts, seq):
    n, d_model = x2.shape
    tm = PROJ_ROWS
    hd2 = DIFF_HEADS * 2 * HEAD_DIM
    row_spec = lambda w: pl.BlockSpec((tm, w), lambda i: (i, 0))
    grp_spec = pl.BlockSpec((NSA_KV_GROUPS, tm, LANES), lambda i: (0, i, 0))
    out_shape = (
        jax.ShapeDtypeStruct((n, hd2), bf16), jax.ShapeDtypeStruct((n, hd2), bf16),
        jax.ShapeDtypeStruct((n, 2 * hd2), bf16), jax.ShapeDtypeStruct((n, NSA_HEADS * LANES), bf16),
        jax.ShapeDtypeStruct((n, LANES), f32), jax.ShapeDtypeStruct((n, LANES), f32),
        jax.ShapeDtypeStruct((NSA_KV_GROUPS, n, LANES), bf16), jax.ShapeDtypeStruct((NSA_KV_GROUPS, n, LANES), bf16),
        jax.ShapeDtypeStruct((NSA_KV_GROUPS, n, LANES), bf16), jax.ShapeDtypeStruct((NSA_KV_GROUPS, n, LANES), bf16),
        jax.ShapeDtypeStruct((NSA_KV_GROUPS, n, LANES), f32),
    )
    out_specs = (row_spec(hd2), row_spec(hd2), row_spec(2 * hd2), row_spec(NSA_HEADS * LANES),
                 row_spec(LANES), row_spec(LANES), grp_spec, grp_spec, grp_spec, grp_spec, grp_spec)
    return pl.pallas_call(
        functools.partial(_proj_kernel, seq=seq),
        grid=(n // tm,),
        in_specs=[row_spec(d_model), pl.BlockSpec((tm, 1), lambda i: (i, 0)),
                  _const_spec(w_pad.shape), _const_spec(consts.shape)],
        out_specs=out_specs, out_shape=out_shape,
        compiler_params=_cparams(("parallel",)), name="proj",
    )(x2, pos_col, w_pad, consts)


def _compress_one(c_ref, pea_ref, peb_ref, w1a_ref, w1b_ref, b1_ref, w2_ref):
    n = c_ref.shape[1] // CMP_STRIDE
    ck = jnp.concatenate([c_ref[0, pl.ds(t, n, stride=CMP_STRIDE), :] for t in range(CMP_STRIDE)], axis=1)
    hid_a = jnp.dot((ck + pea_ref[...]).astype(bf16), w1a_ref[...], preferred_element_type=f32)
    hid_b = jnp.dot((ck + peb_ref[...]).astype(bf16), w1b_ref[...], preferred_element_type=f32)
    hid = hid_a + pltpu.roll(hid_b, n - 1, 0) + b1_ref[...]
    hid = jax.nn.gelu(hid)
    return jnp.dot(hid.astype(bf16), w2_ref[...], preferred_element_type=f32)


def _compress_kernel(kc_ref, vc_ref, pos_ref, c_ref,
                     pak_ref, pbk_ref, w1ak_ref, w1bk_ref, b1k_ref, w2k_ref,
                     pav_ref, pbv_ref, w1av_ref, w1bv_ref, b1v_ref, w2v_ref,
                     kcmp_ref, vcmp_ref):
    k = _compress_one(kc_ref, pak_ref, pbk_ref, w1ak_ref, w1bk_ref, b1k_ref, w2k_ref)
    k = _rope(k, _rope_tables(pos_ref[0], c_ref))
    low = lax.broadcasted_iota(jnp.int32, k.shape, 1) < HEAD_DIM
    kcmp_ref[0, 0] = jnp.where(low, k, 0.0).astype(bf16)
    kcmp_ref[0, 1] = jnp.where(low, pltpu.roll(k, HEAD_DIM, 1), 0.0).astype(bf16)
    v = _compress_one(vc_ref, pav_ref, pbv_ref, w1av_ref, w1bv_ref, b1v_ref, w2v_ref)
    vcmp_ref[0, 0] = jnp.where(low, v, 0.0).astype(bf16)
    vcmp_ref[0, 1] = jnp.where(low, pltpu.roll(v, HEAD_DIM, 1), 0.0).astype(bf16)


def _compress_weights(pe, w1, b1, w2):
    r = CMP_BLOCK // CMP_STRIDE
    assert r == 2
    eye = jnp.eye(NSA_KV_GROUPS, dtype=f32)
    w1r = w1.reshape(CMP_BLOCK, HEAD_DIM, CMP_HIDDEN)
    per = CMP_STRIDE * NSA_KV_GROUPS * HEAD_DIM

    def big(part):
        return jnp.einsum('tcm,gh->tgchm', part, eye).reshape(per, NSA_KV_GROUPS * CMP_HIDDEN).astype(bf16)

    def pe_row(part):
        return jnp.broadcast_to(part[:, None, :], (CMP_STRIDE, NSA_KV_GROUPS, HEAD_DIM)).reshape(1, per)

    w2b = jnp.einsum('mc,gh->gmhc', w2, eye).reshape(NSA_KV_GROUPS * CMP_HIDDEN, NSA_KV_GROUPS * HEAD_DIM)
    return (pe_row(pe[:CMP_STRIDE]), pe_row(pe[CMP_STRIDE:]), big(w1r[:CMP_STRIDE]), big(w1r[CMP_STRIDE:]),
            jnp.tile(b1, NSA_KV_GROUPS).reshape(1, -1), w2b.astype(bf16))


def _compress(kc, vc, cmp_pos, consts, wk, wv):
    b, s, width = kc.shape
    nch = s // CMP_STRIDE
    seq_spec = pl.BlockSpec((1, s, width), lambda i: (i, 0, 0))
    w_specs = [_const_spec(w.shape) for w in wk + wv]
    return pl.pallas_call(
        _compress_kernel,
        grid=(b,),
        in_specs=[seq_spec, seq_spec, pl.BlockSpec((1, nch, 1), lambda i: (i, 0, 0)),
                  _const_spec(consts.shape)] + w_specs,
        out_specs=(pl.BlockSpec((1, NSA_KV_GROUPS, nch, LANES), lambda i: (i, 0, 0, 0)),
                   pl.BlockSpec((1, NSA_KV_GROUPS, nch, LANES), lambda i: (i, 0, 0, 0))),
        out_shape=(jax.ShapeDtypeStruct((b, NSA_KV_GROUPS, nch, LANES), bf16),
                   jax.ShapeDtypeStruct((b, NSA_KV_GROUPS, nch, LANES), bf16)),
        compiler_params=_cparams(("parallel",)), name="compress",
    )(kc, vc, cmp_pos, consts, *wk, *wv)


def _softmax_init(m_ref, acc_ref):
    m_ref[...] = jnp.full(m_ref.shape, NEG_INF, f32)
    acc_ref[...] = jnp.zeros(acc_ref.shape, f32)


def _softmax_step(s, v_ones, m_ref, acc_ref, rs):
    m_prev = m_ref[rs, :]
    m_new = jnp.maximum(m_prev, jnp.max(s, axis=1, keepdims=True))
    alpha = jnp.exp2(m_prev - m_new)
    p = jnp.exp2((s - jnp.concatenate([m_new] * (s.shape[1] // LANES), axis=1)).astype(bf16))
    pv = jnp.dot(p, v_ones, preferred_element_type=f32)
    acc_ref[rs, :] = jnp.concatenate([alpha] * (acc_ref.shape[1] // LANES), axis=1) * acc_ref[rs, :] + pv
    m_ref[rs, :] = m_new


def _scores(q, k):
    return lax.dot_general(q, k, (((1,), (1,)), ((), ())), preferred_element_type=f32)


def _run_steps(steps, m_ref, acc_ref):
    s_next = steps[0][0]()
    for n, (_, v_ones_fn, rows) in enumerate(steps):
        s = s_next
        if n + 1 < len(steps):
            s_next = steps[n + 1][0]()
        _softmax_step(s, v_ones_fn(), m_ref, acc_ref, rows)


def _diff_kernel(q_ref, k_ref, v_ref, lam_ref, g_ref, o_ref, q2_ref, m_ref, acc_ref, *, lam_init):
    tq = q_ref.shape[1]
    tk = min(DIFF_K_TILE, tq)
    per_q = tq // tk
    rc = min(SOFTMAX_ROWS, tq)
    vdim = o_ref.shape[2]
    i = pl.program_id(2)
    q = q_ref[0]
    low = lax.broadcasted_iota(jnp.int32, q.shape, 1) < HEAD_DIM
    zero = jnp.zeros_like(q)
    q2_ref[0:tq, :] = jnp.where(low, q, zero)
    q2_ref[tq:2 * tq, :] = jnp.where(low, zero, q)
    _softmax_init(m_ref, acc_ref)

    def tile_steps(j, k_off=None):
        off = pl.multiple_of(j * tk, tk)
        steps = []
        for c in range(2 * tq // rc):
            q_off = (c * rc) % tq
            cols = tk if k_off is None else min(tk, q_off + rc - k_off)
            if cols <= 0:
                continue
            masked = k_off is not None and q_off - k_off + 1 < cols

            def score(c=c, q_off=q_off, cols=cols, masked=masked):
                s = _scores(q2_ref[c * rc:(c + 1) * rc, :], k_ref[0, pl.ds(off, cols), :])
                if masked:
                    r = q_off + lax.broadcasted_iota(jnp.int32, (rc, cols), 0)
                    s = jnp.where(r >= k_off + lax.broadcasted_iota(jnp.int32, (rc, cols), 1), s, NEG_INF)
                return s

            steps.append((score, lambda cols=cols: v_ref[0, pl.ds(off, cols), :], slice(c * rc, (c + 1) * rc)))
        return steps

    group = 2 if per_q % 2 == 0 else 1

    def body(j, carry):
        _run_steps(sum((tile_steps(group * j + t) for t in range(group)), []), m_ref, acc_ref)
        return carry

    lax.fori_loop(0, i * (per_q // group), body, 0)
    _run_steps(sum((tile_steps(i * per_q + t, t * tk) for t in range(per_q)), []), m_ref, acc_ref)

    lam_v = lam_ref[...]
    lam = (jnp.exp(jnp.sum(lam_v[0:1] * lam_v[1:2], axis=1, keepdims=True))
           - jnp.exp(jnp.sum(lam_v[2:3] * lam_v[3:4], axis=1, keepdims=True)) + lam_init)
    o = acc_ref[:, :vdim] / acc_ref[:, vdim:]
    od = o[:tq] - lam * o[tq:]
    od = od * lax.rsqrt(jnp.mean(od * od, axis=-1, keepdims=True) + RMS_EPS)
    o_ref[0] = (od * g_ref[...] * (1.0 - lam_init)).astype(o_ref.dtype)


def _diff_attention(qd, kd, vd, lam_vec, diff_g, lam_init):
    b, s, _ = qd.shape
    tq = min(ATTN_TILE, s)
    vdim = vd.shape[2] // DIFF_HEADS // 2
    assert vdim == LANES
    return pl.pallas_call(
        functools.partial(_diff_kernel, lam_init=lam_init),
        grid=(b, DIFF_HEADS, s // tq),
        in_specs=[pl.BlockSpec((1, tq, LANES), lambda bi, h, i: (bi, i, h)),
                  pl.BlockSpec((1, s, LANES), lambda bi, h, i: (bi, 0, h)),
                  pl.BlockSpec((1, s, 2 * vdim), lambda bi, h, i: (bi, 0, h)),
                  _const_spec(lam_vec.shape), _const_spec(diff_g.shape)],
        out_specs=pl.BlockSpec((1, tq, vdim), lambda bi, h, i: (bi, i, h)),
        out_shape=jax.ShapeDtypeStruct((b, s, DIFF_HEADS * vdim), bf16),
        scratch_shapes=[pltpu.VMEM((2 * tq, LANES), bf16), pltpu.VMEM((2 * tq, LANES), f32),
                        pltpu.VMEM((2 * tq, 2 * vdim), f32)],
        compiler_params=_cparams(("parallel", "parallel", "arbitrary")), name="diff_attn",
    )(qd, kd, vd, lam_vec, diff_g)


def _nsa_kernel(q_ref, kc_ref, vc_ref, ks_ref, vs_ref, kw_ref, vw_ref, g_ref, ov_ref, o_ref,
                qa_ref, m_ref, acc_ref, mw_ref, accw_ref, oc_ref, imp_ref, rank_ref, gate_ref, *, top_k):
    tq = q_ref.shape[1]
    seq = ks_ref.shape[2]
    tk = min(NSA_K_TILE, seq)
    ncp = kc_ref.shape[2]
    nslc = ov_ref.shape[0]
    rep = NSA_GQA
    i = pl.program_id(2)
    s0 = i * tq
    heads = [slice(r * tq, (r + 1) * tq) for r in range(rep)]

    def tile_steps(q_of, k_ref, v_ref, back, width, lo=None, hi=None):
        off = pl.multiple_of(s0 - back, tq)
        steps = []
        for r in range(rep):

            def score(r=r):
                s = _scores(q_of(r), k_ref[0, 0, pl.ds(off, width), :])
                if lo is None and hi is None:
                    return s
                d = (lax.broadcasted_iota(jnp.int32, (tq, width), 1)
                     - lax.broadcasted_iota(jnp.int32, (tq, width), 0))
                if hi is not None:
                    s = jnp.where(d <= hi, s, NEG_INF)
                if lo is not None:
                    s = jnp.where(d > lo, s, NEG_INF)
                return s

            steps.append((score, lambda: v_ref[0, 0, pl.ds(off, width), :], heads[r]))
        return steps

    def normalized(ref):
        acc = ref[...]
        low = lax.broadcasted_iota(jnp.int32, acc.shape, 1) < HEAD_DIM
        return acc / jnp.where(low, pltpu.roll(acc, HEAD_DIM, 1), 1.0)

    def compressed_branch(width, window_steps):
        t_col = s0 + lax.broadcasted_iota(jnp.int32, (tq, width), 0)
        cmp_end = lax.broadcasted_iota(jnp.int32, (tq, width), 1) * CMP_STRIDE + (CMP_BLOCK - 1)
        cmp_ok = cmp_end <= t_col
        kc = kc_ref[0, 0, 0:width, :]
        vc = vc_ref[0, 0, 0:width, :]
        raw = [_scores(q_ref[0, :, r * LANES:(r + 1) * LANES], kc) for r in range(rep)]
        _softmax_init(mw_ref, accw_ref)
        _run_steps(window_steps, mw_ref, accw_ref)
        psum = jnp.zeros((tq, width), f32)
        for r in range(rep):
            s = jnp.where(cmp_ok, raw[r], NEG_INF)
            p = jnp.where(cmp_ok, jnp.exp2(s - jnp.max(s, axis=1, keepdims=True)), 0.0)
            den = jnp.sum(p, axis=1, keepdims=True)
            p = p / jnp.where(den > 0.0, den, 1.0)
            psum = psum + p
            oc_ref[heads[r], :] = jnp.dot(p.astype(bf16), vc, preferred_element_type=f32)
        p_hi = psum.astype(bf16)
        p_lo = (psum - p_hi.astype(f32)).astype(bf16)
        ov = ov_ref[:, 0:width]
        imp_ref[...] = _scores(ov, p_hi) + _scores(ov, p_lo)
        gates = g_ref[0, 0]
        for c in range(3 * rep):
            gate_ref[c] = jnp.broadcast_to(gates[:, c:c + 1], (tq, LANES))

    win = functools.partial(tile_steps, lambda r: q_ref[0, :, r * LANES:(r + 1) * LANES], kw_ref, vw_ref)
    n_back = WINDOW // tq
    windows = [lambda n=n: win(n * tq, (n + 1) * tq, hi=n * tq) for n in range(n_back)]
    windows.append(lambda: win(WINDOW, WINDOW + tq, lo=0, hi=WINDOW))
    half = ncp // 2
    has_half = half % LANES == 0
    last_half_tile = half // (tq // CMP_STRIDE) - 1 if has_half else -1
    assert not has_half or last_half_tile >= n_back
    for n, window in enumerate(windows):
        this_window = (i == n) if n < n_back else (i >= n_back)
        if not has_half:
            pl.when(this_window)(functools.partial(lambda w: compressed_branch(ncp, w()), window))
        elif n < n_back:
            pl.when(this_window)(functools.partial(lambda w: compressed_branch(half, w()), window))
        else:
            pl.when(this_window & (i <= last_half_tile))(
                functools.partial(lambda w: compressed_branch(half, w()), window))
            pl.when(i > last_half_tile)(functools.partial(lambda w: compressed_branch(ncp, w()), window))

    blk = lax.broadcasted_iota(jnp.int32, (nslc, tq), 0)
    cur = (s0 + lax.broadcasted_iota(jnp.int32, (nslc, tq), 1)) // SLC_BLOCK
    imp = jnp.where(blk > cur, -1.0, imp_ref[...])
    imp = jnp.where((blk == 0) | (blk == cur) | (blk == cur - 1), FORCED_SCORE, imp)
    rank_ref[...] = jnp.zeros((nslc, tq), f32)
    per_tile = tq // SLC_BLOCK
    for first in range(0, nslc, per_tile):

        @pl.when(first <= i * per_tile)
        def _():
            rank = rank_ref[...]
            for jp in range(first, min(first + per_tile, nslc)):
                other = imp[jp:jp + 1, :]
                ahead = jnp.where(blk > jp, jnp.where(other >= imp, 1.0, 0.0), jnp.where(other > imp, 1.0, 0.0))
                rank = rank + ahead
            rank_ref[...] = rank

    bias_t = jnp.where(rank_ref[...] < top_k, 0.0, SEL_BIAS)
    pieces = [jnp.zeros((HEAD_DIM, tq), f32), bias_t]
    if nslc < LANES - HEAD_DIM:
        pieces.append(jnp.zeros((LANES - HEAD_DIM - nslc, tq), f32))
    bias = jnp.concatenate(pieces, axis=0).T.astype(bf16)
    for r in range(rep):
        qa_ref[heads[r], :] = q_ref[0, :, r * LANES:(r + 1) * LANES] + bias


    _softmax_init(m_ref, acc_ref)
    sel = functools.partial(tile_steps, lambda r: qa_ref[heads[r], :], ks_ref, vs_ref)

    trip = NSA_TRIP_TILES * tk

    def sel_body(j, carry):
        back = s0 - j * trip
        _run_steps(sum((sel(back - t * tk, tk) for t in range(NSA_TRIP_TILES)), []), m_ref, acc_ref)
        return carry

    lax.fori_loop(0, s0 // trip, sel_body, 0)
    tail = lax.rem(i, trip // tq)
    for n in range(trip // tq):
        steps = sum((sel(n * tq - t * tk, tk) for t in range(n // 2)), [])
        steps += sel(tq, tk, hi=tq) if n % 2 else sel(0, tq, hi=0)
        pl.when(tail == n)(functools.partial(_run_steps, steps, m_ref, acc_ref))

    o_sel = normalized(acc_ref)
    o_win = normalized(accw_ref)

    lane = lax.broadcasted_iota(jnp.int32, (tq, LANES), 1)
    gated = []
    for r in range(rep):
        gated.append(gate_ref[3 * r] * oc_ref[heads[r], :] + gate_ref[3 * r + 1] * o_sel[heads[r]]
                     + gate_ref[3 * r + 2] * o_win[heads[r]])
    for c in range(rep // 2):
        o_ref[0, :, c * LANES:(c + 1) * LANES] = jnp.where(
            lane < HEAD_DIM, gated[2 * c], pltpu.roll(gated[2 * c + 1], HEAD_DIM, 1)).astype(o_ref.dtype)


def _nsa_attention(qn, kcmp, vcmp, ksa, vsa, kwa, vwa, gates, overlap_t, top_k):
    b, s, _ = qn.shape
    tq = min(NSA_Q_TILE, s)
    ncp = kcmp.shape[2]
    rep = NSA_GQA
    assert WINDOW == 2 * tq and min(NSA_K_TILE, s) == 2 * tq
    cmp_spec = pl.BlockSpec((1, 1, ncp, LANES), lambda bi, g, i: (bi, g, 0, 0))
    seq_spec = pl.BlockSpec((1, 1, s, LANES), lambda bi, g, i: (g, bi, 0, 0))
    rows = rep * tq
    return pl.pallas_call(
        functools.partial(_nsa_kernel, top_k=top_k),
        grid=(b, NSA_KV_GROUPS, s // tq),
        in_specs=[pl.BlockSpec((1, tq, rep * LANES), lambda bi, g, i: (bi, i, g)),
                  cmp_spec, cmp_spec, seq_spec, seq_spec, seq_spec, seq_spec,
                  pl.BlockSpec((1, 1, tq, LANES), lambda bi, g, i: (g, bi, i, 0)),
                  _const_spec(overlap_t.shape)],
        out_specs=pl.BlockSpec((1, tq, rep * HEAD_DIM), lambda bi, g, i: (bi, i, g)),
        out_shape=jax.ShapeDtypeStruct((b, s, NSA_HEADS * HEAD_DIM), bf16),
        scratch_shapes=[pltpu.VMEM((rows, LANES), bf16)] + [pltpu.VMEM((rows, LANES), f32)] * 5 + [
                        pltpu.VMEM(overlap_t.shape[:1] + (tq,), f32),
                        pltpu.VMEM(overlap_t.shape[:1] + (tq,), f32), pltpu.VMEM((3 * rep, tq, LANES), f32)],
        compiler_params=_cparams(("parallel", "parallel", "arbitrary")), name="nsa_attn",
    )(qn, kcmp, vcmp, ksa, vsa, kwa, vwa, gates, overlap_t)


def _outproj_kernel(x_ref, od_ref, on_ref, wa_ref, wb_ref, g_ref, b_ref, o_ref, *, alpha):
    parts = 2
    rows = x_ref.shape[0] // parts

    def mix(r):
        rs = slice(r * rows, (r + 1) * rows)
        return (jnp.dot(od_ref[rs, :], wa_ref[...], preferred_element_type=f32)
                + jnp.dot(on_ref[rs, :], wb_ref[...], preferred_element_type=f32))

    y_next = mix(0)
    for r in range(parts):
        rs = slice(r * rows, (r + 1) * rows)
        y = y_next
        if r + 1 < parts:
            y_next = mix(r + 1)
        o_ref[rs, :] = _layer_norm(alpha * x_ref[rs, :] + y, g_ref[...], b_ref[...])


def _outproj(x2, od, on, wa, wb, g, b, alpha):
    n, d_model = x2.shape
    tm = 2 * DENSE_ROWS
    row_spec = lambda w: pl.BlockSpec((tm, w), lambda i: (i, 0))
    return pl.pallas_call(
        functools.partial(_outproj_kernel, alpha=alpha),
        grid=(n // tm,),
        in_specs=[row_spec(d_model), row_spec(od.shape[1]), row_spec(on.shape[1]),
                  _const_spec(wa.shape), _const_spec(wb.shape), _const_spec(g.shape), _const_spec(b.shape)],
        out_specs=row_spec(d_model),
        out_shape=jax.ShapeDtypeStruct((n, d_model), f32),
        compiler_params=_cparams(("parallel",)), name="outproj_ln",
    )(x2, od, on, wa, wb, g, b)


def _ffn_kernel(h_ref, halo_ref, wu_ref, cw_ref, cb_ref, wd_ref, g_ref, b_ref, o_ref, hb_ref, u_ref, act_ref,
                *, alpha, tiles_per_seq):
    tm = h_ref.shape[0]
    tf = FFN_CHUNK
    n_chunks = wd_ref.shape[0] // tf
    h = h_ref[...]
    first = lax.rem(pl.program_id(0), tiles_per_seq) == 0
    halo = jnp.where(first, 0.0, halo_ref[...])
    d_ff = wd_ref.shape[0]
    hb_ref[...] = jnp.concatenate([halo, h], axis=0).astype(bf16)

    def up(c):
        for part, col in enumerate((c * tf, d_ff + c * tf)):
            u_ref[c % 2, :, part * tf:(part + 1) * tf] = jnp.dot(
                hb_ref[...], wu_ref[:, col:col + tf], preferred_element_type=f32)

    def conv(c, part):
        col = part * d_ff + c * tf
        u = cb_ref[:, col:col + tf]
        for k in range(CONV_WIDTH):
            u = u + cw_ref[k:k + 1, col:col + tf] * u_ref[
                c % 2, pl.ds(HALO - (CONV_WIDTH - 1) + k, tm), part * tf:(part + 1) * tf]
        return u

    up(0)
    for c in range(n_chunks):
        if c + 1 < n_chunks:
            up(c + 1)
        act_ref[:, c * tf:(c + 1) * tf] = (jax.nn.silu(conv(c, 0)) * conv(c, 1)).astype(bf16)
    down = jnp.dot(act_ref[...], wd_ref[...], preferred_element_type=f32)
    o_ref[...] = _layer_norm(alpha * h + down, g_ref[...], b_ref[...])


def _ffn(h1, wu, cw, cb, wd, g, b, alpha, seq):
    n, d_model = h1.shape
    tm = min(DENSE_ROWS, seq)
    single = dict(pipeline_mode=pl.Buffered(1))
    return pl.pallas_call(
        functools.partial(_ffn_kernel, alpha=alpha, tiles_per_seq=seq // tm),
        grid=(n // tm,),
        in_specs=[pl.BlockSpec((tm, d_model), lambda i: (i, 0)),
                  pl.BlockSpec((HALO, d_model), lambda i: (jnp.maximum(i * (tm // HALO) - 1, 0), 0)),
                  pl.BlockSpec(wu.shape, lambda i: (0, 0), **single),
                  _const_spec(cw.shape), _const_spec(cb.shape),
                  pl.BlockSpec(wd.shape, lambda i: (0, 0), **single),
                  _const_spec(g.shape), _const_spec(b.shape)],
        out_specs=pl.BlockSpec((tm, d_model), lambda i: (i, 0)),
        out_shape=jax.ShapeDtypeStruct((n, d_model), f32),
        scratch_shapes=[pltpu.VMEM((HALO + tm, d_model), bf16), pltpu.VMEM((2, HALO + tm, 2 * FFN_CHUNK), f32),
                        pltpu.VMEM((tm, wd.shape[0]), bf16)],
        compiler_params=_cparams(("parallel",)), name="ffn_ln",
    )(h1, h1, wu, cw, cb, wd, g, b)


def kernel(x, positions, w_in, lambda_q1, lambda_k1, lambda_q2, lambda_k2, diff_norm_g, cmp_pe_k, cmp_w1_k, cmp_b1_k, cmp_w2_k, cmp_pe_v, cmp_w1_v, cmp_b1_v, cmp_w2_v, w_out, ln1_g, ln1_b, w_up, conv_w, conv_b, w_down, ln2_g, ln2_b):
    b, s, d_model = x.shape
    depth = w_in.shape[0]
    n = b * s
    d_ff = w_down.shape[1]
    assert s % ATTN_TILE == 0 or s < ATTN_TILE
    assert s % SLC_BLOCK == 0 and s // SLC_BLOCK <= LANES - HEAD_DIM and d_ff % FFN_CHUNK == 0
    alpha = (2 * depth) ** 0.25
    consts = _rope_consts()
    pos_col = positions.reshape(n, 1)

    n_chunk = s // CMP_STRIDE
    n_cmp = n_chunk - CMP_BLOCK // CMP_STRIDE + 1
    n_slc = s // SLC_BLOCK
    top_k = min(SLC_TOPK, n_slc)
    cmp_pos = positions[:, CMP_BLOCK - 1::CMP_STRIDE][:, :n_cmp]
    cmp_pos = jnp.pad(cmp_pos, ((0, 0), (0, n_chunk - n_cmp))).reshape(b, n_chunk, 1)
    cs = np.arange(n_chunk)[None, :] * CMP_STRIDE
    ss = np.arange(n_slc)[:, None] * SLC_BLOCK
    ov = np.clip(np.minimum(cs + CMP_BLOCK, ss + SLC_BLOCK) - np.maximum(cs, ss), 0, None) / CMP_BLOCK
    ov[:, n_cmp:] = 0.0
    overlap_t = jnp.asarray(ov, dtype=bf16)

    h = x.reshape(n, d_model)
    for l in range(depth):
        lam_init = 0.8 - 0.6 * math.exp(-0.3 * l)
        d_in = w_in.shape[2]
        w_pad = jnp.pad(w_in[l], ((0, 0), (0, -d_in % LANES))).astype(bf16)
        qd, kd, vd, qn, kc, vc, ksa, vsa, kwa, vwa, gates = _proj(h, pos_col, w_pad, consts, s)

        kcmp, vcmp = _compress(
            kc.reshape(b, s, LANES), vc.reshape(b, s, LANES), cmp_pos, consts,
            _compress_weights(cmp_pe_k[l], cmp_w1_k[l], cmp_b1_k[l], cmp_w2_k[l]),
            _compress_weights(cmp_pe_v[l], cmp_w1_v[l], cmp_b1_v[l], cmp_w2_v[l]))

        lam_vec = jnp.stack([lambda_q1[l], lambda_k1[l], lambda_q2[l], lambda_k2[l]]).astype(f32)
        od = _diff_attention(qd.reshape(b, s, -1), kd.reshape(b, s, -1), vd.reshape(b, s, -1),
                             lam_vec, diff_norm_g[l].reshape(1, -1).astype(f32), lam_init)
        grouped = lambda a: a.reshape(NSA_KV_GROUPS, b, s, LANES)
        on = _nsa_attention(qn.reshape(b, s, -1), kcmp, vcmp, grouped(ksa), grouped(vsa), grouped(kwa),
                            grouped(vwa), grouped(gates), overlap_t, top_k)

        d_diff = od.shape[2]
        wo = w_out[l].astype(bf16)
        h = _outproj(h, od.reshape(n, -1), on.reshape(n, -1), wo[:d_diff], wo[d_diff:],
                     ln1_g[l].reshape(1, -1), ln1_b[l].reshape(1, -1), alpha)
        h = _ffn(h, w_up[l].astype(bf16), conv_w[l], conv_b[l].reshape(1, -1),
                 w_down[l].astype(bf16), ln2_g[l].reshape(1, -1), ln2_b[l].reshape(1, -1), alpha, s)
    return h.reshape(b, s, d_model)
```

```python
import functools
import math

import jax
import jax.numpy as jnp
import numpy as np
from jax import lax
from jax.experimental import pallas as pl
from jax.experimental.pallas import tpu as pltpu

f32 = jnp.float32
bf16 = jnp.bfloat16

LANES = 128
HEAD_DIM = 64
ROPE_DIM = HEAD_DIM // 4
ROPE_THETA = 500000.0
DIFF_HEADS = 4
NSA_HEADS = 8
NSA_KV_GROUPS = 2
NSA_GQA = NSA_HEADS // NSA_KV_GROUPS
CMP_BLOCK = 32
CMP_STRIDE = 16
CMP_HIDDEN = 2 * HEAD_DIM
SLC_BLOCK = 64
SLC_TOPK = 16
WINDOW = 512
CONV_WIDTH = 3
LN_EPS = 1e-5
RMS_EPS = 1e-5
NEG_INF = -1e30
SEL_BIAS = -1e9
FORCED_SCORE = 1e6
VMEM_LIMIT = 56 * 1024 * 1024

LOG2E = 1.4426950408889634
QK_SCALE = HEAD_DIM ** -0.5 * LOG2E

PROJ_ROWS = 512
ATTN_TILE = 4096
DIFF_K_TILE = 512
NSA_Q_TILE = 256
NSA_K_TILE = 512
NSA_TILES_PER_STEP = 4
SOFTMAX_ROWS = 256
DENSE_ROWS = 512
FFN_CHUNK = 256
HALO = 8


def _cparams(sem):
    return pltpu.CompilerParams(dimension_semantics=sem, vmem_limit_bytes=VMEM_LIMIT)


def _const_spec(shape):
    n = len(shape)
    return pl.BlockSpec(shape, lambda *_: (0,) * n)


def _layer_norm(y, g, b):
    mu = jnp.mean(y, axis=-1, keepdims=True)
    d = y - mu
    var = jnp.mean(d * d, axis=-1, keepdims=True)
    return d * lax.rsqrt(var + LN_EPS) * g + b


def _rope_consts():
    lane = np.arange(LANES)
    in_head = lane % HEAD_DIM
    half = ROPE_DIM // 2
    inv_freq = 1.0 / (ROPE_THETA ** (jnp.arange(half, dtype=f32) / half))
    c = jnp.zeros((8, LANES), f32)
    c = c.at[0].set(jnp.tile(inv_freq, LANES // half))
    c = c.at[1].set(jnp.asarray(in_head < ROPE_DIM, f32))
    c = c.at[2].set(jnp.asarray(np.where(in_head < half, -1.0, np.where(in_head < ROPE_DIM, 1.0, 0.0)), f32))
    c = c.at[3].set(jnp.asarray(in_head < half, f32))
    return c


def _rope_tables(pos_col, c_ref):
    ang = pos_col.astype(f32) * c_ref[0:1, :]
    cos_t = jnp.where(c_ref[1:2, :] > 0.0, jnp.cos(ang), 1.0)
    sin_t = jnp.sin(ang) * c_ref[2:3, :]
    return cos_t, sin_t, c_ref[3:4, :] > 0.0


def _rope(y, tables):
    cos_t, sin_t, first = tables
    half = ROPE_DIM // 2
    partner = jnp.where(first, pltpu.roll(y, LANES - half, 1), pltpu.roll(y, half, 1))
    return y * cos_t + partner * sin_t


def _proj_kernel(x_ref, pos_ref, w_ref, c_ref,
                 qd_ref, kd_ref, vd_ref, qn_ref, kc_ref, vc_ref, ks_ref, vs_ref, kw_ref, vw_ref, g_ref,
                 *, seq):
    tm = x_ref.shape[0]
    xb = x_ref[...].astype(bf16)
    tables = _rope_tables(pos_ref[...], c_ref)
    lane = lax.broadcasted_iota(jnp.int32, (tm, LANES), 1)
    low = lane < HEAD_DIM
    scale = QK_SCALE

    hd2 = DIFF_HEADS * 2 * HEAD_DIM

    def diff_q(y):
        for c in range(hd2 // LANES):
            sl = slice(c * LANES, (c + 1) * LANES)
            qd_ref[:, sl] = (_rope(y[:, sl], tables) * scale).astype(bf16)

    def diff_k(y):
        for c in range(hd2 // LANES):
            sl = slice(c * LANES, (c + 1) * LANES)
            kd_ref[:, sl] = _rope(y[:, sl], tables).astype(bf16)

    def diff_v(y):
        ones = jnp.ones((tm, LANES), bf16)
        for c in range(hd2 // LANES):
            vd_ref[:, (2 * c) * LANES:(2 * c + 1) * LANES] = y[:, c * LANES:(c + 1) * LANES].astype(bf16)
            vd_ref[:, (2 * c + 1) * LANES:(2 * c + 2) * LANES] = ones

    def nsa_q(y):
        for c in range(NSA_HEADS // 2):
            slab = _rope(y[:, c * LANES:(c + 1) * LANES], tables) * scale
            qn_ref[:, (2 * c) * LANES:(2 * c + 1) * LANES] = jnp.where(low, slab, 0.0).astype(bf16)
            qn_ref[:, (2 * c + 1) * LANES:(2 * c + 2) * LANES] = jnp.where(
                low, pltpu.roll(slab, HEAD_DIM, 1), 0.0).astype(bf16)

    def cmp_kv(y):
        kc_ref[...] = y[:, :LANES]
        vc_ref[...] = y[:, LANES:]

    def grouped(ref, y, fill):
        ref[0] = jnp.where(low, y, fill).astype(ref.dtype)
        ref[1] = jnp.where(low, pltpu.roll(y, HEAD_DIM, 1), fill).astype(ref.dtype)

    def sel_kv(y):
        row = lax.broadcasted_iota(jnp.int32, (tm, LANES), 0) + lax.rem(pl.program_id(0) * tm, seq)
        onehot = jnp.where(lane - HEAD_DIM == row // SLC_BLOCK, 1.0, 0.0)
        grouped(ks_ref, _rope(y[:, :LANES], tables), onehot)
        grouped(vs_ref, y[:, LANES:], 1.0)

    def win_kv(y):
        grouped(kw_ref, _rope(y[:, :LANES], tables), 0.0)
        grouped(vw_ref, y[:, LANES:], 1.0)

    def gate(y):
        sig = jax.nn.sigmoid(y)
        g_ref[0] = sig
        g_ref[1] = pltpu.roll(sig, LANES - NSA_GQA * 3, 1)

    widths = [hd2, hd2, hd2, NSA_HEADS * HEAD_DIM, 2 * LANES, 2 * LANES, 2 * LANES, LANES]
    epilogues = [diff_q, diff_k, diff_v, nsa_q, cmp_kv, sel_kv, win_kv, gate]
    cols = [sum(widths[:n]) for n in range(len(widths))]

    def seg(n):
        return jnp.dot(xb, w_ref[:, cols[n]:cols[n] + widths[n]], preferred_element_type=f32)

    y_next = seg(0)
    for n, epilogue in enumerate(epilogues):
        y = y_next
        if n + 1 < len(epilogues):
            y_next = seg(n + 1)
        epilogue(y)


def _proj(x2, pos_col, w_pad, consts, seq):
    n, d_model = x2.shape
    tm = PROJ_ROWS
    hd2 = DIFF_HEADS * 2 * HEAD_DIM
    row_spec = lambda w: pl.BlockSpec((tm, w), lambda i: (i, 0))
    grp_spec = pl.BlockSpec((NSA_KV_GROUPS, tm, LANES), lambda i: (0, i, 0))
    out_shape = (
        jax.ShapeDtypeStruct((n, hd2), bf16), jax.ShapeDtypeStruct((n, hd2), bf16),
        jax.ShapeDtypeStruct((n, 2 * hd2), bf16), jax.ShapeDtypeStruct((n, NSA_HEADS * LANES), bf16),
        jax.ShapeDtypeStruct((n, LANES), f32), jax.ShapeDtypeStruct((n, LANES), f32),
        jax.ShapeDtypeStruct((NSA_KV_GROUPS, n, LANES), bf16), jax.ShapeDtypeStruct((NSA_KV_GROUPS, n, LANES), bf16),
        jax.ShapeDtypeStruct((NSA_KV_GROUPS, n, LANES), bf16), jax.ShapeDtypeStruct((NSA_KV_GROUPS, n, LANES), bf16),
        jax.ShapeDtypeStruct((NSA_KV_GROUPS, n, LANES), f32),
    )
    out_specs = (row_spec(hd2), row_spec(hd2), row_spec(2 * hd2), row_spec(NSA_HEADS * LANES),
                 row_spec(LANES), row_spec(LANES), grp_spec, grp_spec, grp_spec, grp_spec, grp_spec)
    return pl.pallas_call(
        functools.partial(_proj_kernel, seq=seq),
        grid=(n // tm,),
        in_specs=[row_spec(d_model), pl.BlockSpec((tm, 1), lambda i: (i, 0)),
                  _const_spec(w_pad.shape), _const_spec(consts.shape)],
        out_specs=out_specs, out_shape=out_shape,
        compiler_params=_cparams(("parallel",)), name="proj",
    )(x2, pos_col, w_pad, consts)


def _compress_one(c_ref, pea_ref, peb_ref, w1a_ref, w1b_ref, b1_ref, w2_ref):
    n = c_ref.shape[1] // CMP_STRIDE
    ck = jnp.concatenate([c_ref[0, pl.ds(t, n, stride=CMP_STRIDE), :] for t in range(CMP_STRIDE)], axis=1)
    hid_a = jnp.dot((ck + pea_ref[...]).astype(bf16), w1a_ref[...], preferred_element_type=f32)
    hid_b = jnp.dot((ck + peb_ref[...]).astype(bf16), w1b_ref[...], preferred_element_type=f32)
    hid = hid_a + pltpu.roll(hid_b, n - 1, 0) + b1_ref[...]
    hid = jax.nn.gelu(hid)
    return jnp.dot(hid.astype(bf16), w2_ref[...], preferred_element_type=f32)


def _compress_kernel(kc_ref, vc_ref, pos_ref, c_ref,
                     pak_ref, pbk_ref, w1ak_ref, w1bk_ref, b1k_ref, w2k_ref,
                     pav_ref, pbv_ref, w1av_ref, w1bv_ref, b1v_ref, w2v_ref,
                     kcmp_ref, vcmp_ref):
    k = _compress_one(kc_ref, pak_ref, pbk_ref, w1ak_ref, w1bk_ref, b1k_ref, w2k_ref)
    k = _rope(k, _rope_tables(pos_ref[0], c_ref))
    low = lax.broadcasted_iota(jnp.int32, k.shape, 1) < HEAD_DIM
    kcmp_ref[0, 0] = jnp.where(low, k, 0.0).astype(bf16)
    kcmp_ref[0, 1] = jnp.where(low, pltpu.roll(k, HEAD_DIM, 1), 0.0).astype(bf16)
    v = _compress_one(vc_ref, pav_ref, pbv_ref, w1av_ref, w1bv_ref, b1v_ref, w2v_ref)
    vcmp_ref[0, 0] = jnp.where(low, v, 0.0).astype(bf16)
    vcmp_ref[0, 1] = jnp.where(low, pltpu.roll(v, HEAD_DIM, 1), 0.0).astype(bf16)


def _compress_weights(pe, w1, b1, w2):
    r = CMP_BLOCK // CMP_STRIDE
    assert r == 2
    eye = jnp.eye(NSA_KV_GROUPS, dtype=f32)
    w1r = w1.reshape(CMP_BLOCK, HEAD_DIM, CMP_HIDDEN)
    per = CMP_STRIDE * NSA_KV_GROUPS * HEAD_DIM

    def big(part):
        return jnp.einsum('tcm,gh->tgchm', part, eye).reshape(per, NSA_KV_GROUPS * CMP_HIDDEN).astype(bf16)

    def pe_row(part):
        return jnp.broadcast_to(part[:, None, :], (CMP_STRIDE, NSA_KV_GROUPS, HEAD_DIM)).reshape(1, per)

    w2b = jnp.einsum('mc,gh->gmhc', w2, eye).reshape(NSA_KV_GROUPS * CMP_HIDDEN, NSA_KV_GROUPS * HEAD_DIM)
    return (pe_row(pe[:CMP_STRIDE]), pe_row(pe[CMP_STRIDE:]), big(w1r[:CMP_STRIDE]), big(w1r[CMP_STRIDE:]),
            jnp.tile(b1, NSA_KV_GROUPS).reshape(1, -1), w2b.astype(bf16))


def _compress(kc, vc, cmp_pos, consts, wk, wv):
    b, s, width = kc.shape
    nch = s // CMP_STRIDE
    seq_spec = pl.BlockSpec((1, s, width), lambda i: (i, 0, 0))
    w_specs = [_const_spec(w.shape) for w in wk + wv]
    return pl.pallas_call(
        _compress_kernel,
        grid=(b,),
        in_specs=[seq_spec, seq_spec, pl.BlockSpec((1, nch, 1), lambda i: (i, 0, 0)),
                  _const_spec(consts.shape)] + w_specs,
        out_specs=(pl.BlockSpec((1, NSA_KV_GROUPS, nch, LANES), lambda i: (i, 0, 0, 0)),
                   pl.BlockSpec((1, NSA_KV_GROUPS, nch, LANES), lambda i: (i, 0, 0, 0))),
        out_shape=(jax.ShapeDtypeStruct((b, NSA_KV_GROUPS, nch, LANES), bf16),
                   jax.ShapeDtypeStruct((b, NSA_KV_GROUPS, nch, LANES), bf16)),
        compiler_params=_cparams(("parallel",)), name="compress",
    )(kc, vc, cmp_pos, consts, *wk, *wv)


def _softmax_init(m_ref, acc_ref):
    m_ref[...] = jnp.full(m_ref.shape, NEG_INF, f32)
    acc_ref[...] = jnp.zeros(acc_ref.shape, f32)


def _softmax_step(s, v_ones, m_ref, acc_ref, rs):
    m_prev = m_ref[rs, :]
    m_new = jnp.maximum(m_prev, jnp.max(s, axis=1, keepdims=True))
    alpha = jnp.exp2(m_prev - m_new)
    p = jnp.exp2((s - jnp.concatenate([m_new] * (s.shape[1] // LANES), axis=1)).astype(bf16))
    pv = jnp.dot(p, v_ones, preferred_element_type=f32)
    acc_ref[rs, :] = jnp.concatenate([alpha] * (acc_ref.shape[1] // LANES), axis=1) * acc_ref[rs, :] + pv
    m_ref[rs, :] = m_new


def _scores(q, k):
    return lax.dot_general(q, k, (((1,), (1,)), ((), ())), preferred_element_type=f32)


def _run_steps(steps, m_ref, acc_ref):
    s_next = steps[0][0]()
    for n, (_, v_ones_fn, rows) in enumerate(steps):
        s = s_next
        if n + 1 < len(steps):
            s_next = steps[n + 1][0]()
        _softmax_step(s, v_ones_fn(), m_ref, acc_ref, rows)


def _diff_kernel(q_ref, k_ref, v_ref, lam_ref, g_ref, o_ref, q2_ref, m_ref, acc_ref, *, lam_init):
    tq = q_ref.shape[1]
    tk = min(DIFF_K_TILE, tq)
    per_q = tq // tk
    rc = min(SOFTMAX_ROWS, tq)
    vdim = o_ref.shape[2]
    i = pl.program_id(2)
    q = q_ref[0]
    low = lax.broadcasted_iota(jnp.int32, q.shape, 1) < HEAD_DIM
    zero = jnp.zeros_like(q)
    q2_ref[0:tq, :] = jnp.where(low, q, zero)
    q2_ref[tq:2 * tq, :] = jnp.where(low, zero, q)
    _softmax_init(m_ref, acc_ref)

    def tile_steps(j, k_off=None):
        off = pl.multiple_of(j * tk, tk)
        steps = []
        for c in range(2 * tq // rc):
            q_off = (c * rc) % tq
            cols = tk if k_off is None else min(tk, q_off + rc - k_off)
            if cols <= 0:
                continue
            masked = k_off is not None and q_off - k_off + 1 < cols

            def score(c=c, q_off=q_off, cols=cols, masked=masked):
                s = _scores(q2_ref[c * rc:(c + 1) * rc, :], k_ref[0, pl.ds(off, cols), :])
                if masked:
                    r = q_off + lax.broadcasted_iota(jnp.int32, (rc, cols), 0)
                    s = jnp.where(r >= k_off + lax.broadcasted_iota(jnp.int32, (rc, cols), 1), s, NEG_INF)
                return s

            steps.append((score, lambda cols=cols: v_ref[0, pl.ds(off, cols), :], slice(c * rc, (c + 1) * rc)))
        return steps

    group = 2 if per_q % 2 == 0 else 1

    def body(j, carry):
        _run_steps(sum((tile_steps(group * j + t) for t in range(group)), []), m_ref, acc_ref)
        return carry

    lax.fori_loop(0, i * (per_q // group), body, 0)
    _run_steps(sum((tile_steps(i * per_q + t, t * tk) for t in range(per_q)), []), m_ref, acc_ref)

    lam_v = lam_ref[...]
    lam = (jnp.exp(jnp.sum(lam_v[0:1] * lam_v[1:2], axis=1, keepdims=True))
           - jnp.exp(jnp.sum(lam_v[2:3] * lam_v[3:4], axis=1, keepdims=True)) + lam_init)
    o = acc_ref[:, :vdim] / acc_ref[:, vdim:]
    od = o[:tq] - lam * o[tq:]
    od = od * lax.rsqrt(jnp.mean(od * od, axis=-1, keepdims=True) + RMS_EPS)
    o_ref[0] = (od * g_ref[...] * (1.0 - lam_init)).astype(o_ref.dtype)


def _diff_attention(qd, kd, vd, lam_vec, diff_g, lam_init):
    b, s, _ = qd.shape
    tq = min(ATTN_TILE, s)
    vdim = vd.shape[2] // DIFF_HEADS // 2
    assert vdim == LANES
    return pl.pallas_call(
        functools.partial(_diff_kernel, lam_init=lam_init),
        grid=(b, DIFF_HEADS, s // tq),
        in_specs=[pl.BlockSpec((1, tq, LANES), lambda bi, h, i: (bi, i, h)),
                  pl.BlockSpec((1, s, LANES), lambda bi, h, i: (bi, 0, h)),
                  pl.BlockSpec((1, s, 2 * vdim), lambda bi, h, i: (bi, 0, h)),
                  _const_spec(lam_vec.shape), _const_spec(diff_g.shape)],
        out_specs=pl.BlockSpec((1, tq, vdim), lambda bi, h, i: (bi, i, h)),
        out_shape=jax.ShapeDtypeStruct((b, s, DIFF_HEADS * vdim), bf16),
        scratch_shapes=[pltpu.VMEM((2 * tq, LANES), bf16), pltpu.VMEM((2 * tq, LANES), f32),
                        pltpu.VMEM((2 * tq, 2 * vdim), f32)],
        compiler_params=_cparams(("parallel", "parallel", "arbitrary")), name="diff_attn",
    )(qd, kd, vd, lam_vec, diff_g)


def _nsa_kernel(q_ref, kc_ref, vc_ref, ks_ref, vs_ref, kw_ref, vw_ref, g_ref, ov_ref, o_ref,
                qa_ref, m_ref, acc_ref, mw_ref, accw_ref, oc_ref, imp_ref, gate_ref, *, top_k, tq):
    seq = ks_ref.shape[2]
    tk = min(NSA_K_TILE, seq)
    ncp = kc_ref.shape[2]
    nslc = ov_ref.shape[0]
    rep = NSA_GQA
    per_step = q_ref.shape[1] // tq
    heads = [slice(r * tq, (r + 1) * tq) for r in range(rep)]
    per_tile = tq // SLC_BLOCK
    half = ncp // 2

    def q_head(c, r):
        return q_ref[0, c * tq:(c + 1) * tq, r * LANES:(r + 1) * LANES]

    def key_steps(q_of, k_ref, v_ref, s0, start, width, windowed=False):
        causal = start + width - 1 > s0
        lower = windowed and start <= s0 + tq - 1 - WINDOW
        steps = []
        for r in range(rep):

            def score(r=r):
                s = _scores(q_of(r), k_ref[0, 0, start:start + width, :])
                if causal or lower:
                    d = (lax.broadcasted_iota(jnp.int32, (tq, width), 1)
                         - lax.broadcasted_iota(jnp.int32, (tq, width), 0))
                    if causal:
                        s = jnp.where(d <= s0 - start, s, NEG_INF)
                    if lower:
                        s = jnp.where(d > s0 - start - WINDOW, s, NEG_INF)
                return s

            steps.append((score, lambda: v_ref[0, 0, start:start + width, :], heads[r]))
        return steps

    def normalized(acc):
        low = lax.broadcasted_iota(jnp.int32, acc.shape, 1) < HEAD_DIM
        return acc / jnp.where(low, pltpu.roll(acc, HEAD_DIM, 1), 1.0)

    def compressed_and_window(i, c, slot):
        s0 = i * tq
        width = half if half % LANES == 0 and (i + 1) * (tq // CMP_STRIDE) <= half else ncp
        t_col = s0 + lax.broadcasted_iota(jnp.int32, (tq, width), 0)
        cmp_end = lax.broadcasted_iota(jnp.int32, (tq, width), 1) * CMP_STRIDE + (CMP_BLOCK - 1)
        cmp_ok = cmp_end <= t_col
        kc = kc_ref[0, 0, 0:width, :]
        vc = vc_ref[0, 0, 0:width, :]
        raw = [_scores(q_head(c, r), kc) for r in range(rep)]
        mw, accw = mw_ref.at[slot], accw_ref.at[slot]
        _softmax_init(mw, accw)
        w_start = max(0, s0 - WINDOW)
        _run_steps(key_steps(functools.partial(q_head, c), kw_ref, vw_ref, s0, w_start, s0 + tq - w_start,
                             windowed=True), mw, accw)
        psum = jnp.zeros((tq, width), f32)
        for r in range(rep):
            s = jnp.where(cmp_ok, raw[r], NEG_INF)
            p = jnp.where(cmp_ok, jnp.exp2(s - jnp.max(s, axis=1, keepdims=True)), 0.0)
            den = jnp.sum(p, axis=1, keepdims=True)
            p = p / jnp.where(den > 0.0, den, 1.0)
            psum = psum + p
            oc_ref[slot, heads[r], :] = jnp.dot(p.astype(bf16), vc, preferred_element_type=f32)
        p_hi = psum.astype(bf16)
        p_lo = (psum - p_hi.astype(f32)).astype(bf16)
        ov = ov_ref[:, 0:width]
        imp_ref[slot] = _scores(ov, p_hi) + _scores(ov, p_lo)
        gates = g_ref[0, 0, c * tq:(c + 1) * tq, :]
        for col in range(3 * rep):
            gate_ref[slot, col] = jnp.broadcast_to(gates[:, col:col + 1], (tq, LANES))

    def select_blocks(i, c, slot):
        s0 = i * tq
        reach = min(nslc, (i + 1) * per_tile)
        rows = min(nslc, -(-reach // 8) * 8)
        blk = lax.broadcasted_iota(jnp.int32, (rows, tq), 0)
        cur = (s0 + lax.broadcasted_iota(jnp.int32, (rows, tq), 1)) // SLC_BLOCK
        imp = jnp.where(blk > cur, -1.0, imp_ref[slot, 0:rows, :])
        imp = jnp.where((blk == 0) | (blk == cur) | (blk == cur - 1), FORCED_SCORE, imp)
        rank = jnp.zeros((rows, tq), f32)
        for jp in range(reach):
            other = imp[jp:jp + 1, :]
            ahead = jnp.where(blk > jp, jnp.where(other >= imp, 1.0, 0.0), jnp.where(other > imp, 1.0, 0.0))
            rank = rank + ahead
        bias_t = jnp.where(rank < top_k, 0.0, SEL_BIAS)
        pieces = [jnp.zeros((HEAD_DIM, tq), f32), bias_t]
        if rows < LANES - HEAD_DIM:
            pieces.append(jnp.zeros((LANES - HEAD_DIM - rows, tq), f32))
        bias = jnp.concatenate(pieces, axis=0).T.astype(bf16)
        for r in range(rep):
            qa_ref[heads[r], :] = q_head(c, r) + bias

    def selected_and_output(i, c, slot):
        s0 = i * tq
        sel = functools.partial(key_steps, lambda r: qa_ref[heads[r], :], ks_ref, vs_ref, s0)
        steps = []
        start = 0
        while start < s0 + tq:
            width = tk if start + tk <= s0 + tq else tq
            steps += sel(start, width)
            start += width
        _softmax_init(m_ref, acc_ref)
        _run_steps(steps, m_ref, acc_ref)
        o_sel = normalized(acc_ref[...])
        o_win = normalized(accw_ref[slot])
        lane = lax.broadcasted_iota(jnp.int32, (tq, LANES), 1)
        gated = []
        for r in range(rep):
            gated.append(gate_ref[slot, 3 * r] * oc_ref[slot, heads[r], :]
                         + gate_ref[slot, 3 * r + 1] * o_sel[heads[r]]
                         + gate_ref[slot, 3 * r + 2] * o_win[heads[r]])
        for pair in range(rep // 2):
            o_ref[0, c * tq:(c + 1) * tq, pair * LANES:(pair + 1) * LANES] = jnp.where(
                lane < HEAD_DIM, gated[2 * pair], pltpu.roll(gated[2 * pair + 1], HEAD_DIM, 1)).astype(o_ref.dtype)

    def run_tiles(first):
        compressed_and_window(first, 0, 0)
        for c in range(per_step):
            if c + 1 < per_step:
                compressed_and_window(first + c + 1, c + 1, (c + 1) % 2)
            select_blocks(first + c, c, c % 2)
            selected_and_output(first + c, c, c % 2)

    n_steps = seq // (tq * per_step)
    for step in range(n_steps):
        pl.when(pl.program_id(2) == step)(functools.partial(run_tiles, step * per_step))


def _nsa_attention(qn, kcmp, vcmp, ksa, vsa, kwa, vwa, gates, overlap_t, top_k):
    b, s, _ = qn.shape
    tq = min(NSA_Q_TILE, s)
    per_step = min(NSA_TILES_PER_STEP, s // tq)
    tqs = tq * per_step
    ncp = kcmp.shape[2]
    rep = NSA_GQA
    assert WINDOW % tq == 0 and min(NSA_K_TILE, s) % tq == 0 and s % tqs == 0
    cmp_spec = pl.BlockSpec((1, 1, ncp, LANES), lambda bi, g, i: (bi, g, 0, 0))
    seq_spec = pl.BlockSpec((1, 1, s, LANES), lambda bi, g, i: (g, bi, 0, 0))
    rows = rep * tq
    nslc = overlap_t.shape[0]
    return pl.pallas_call(
        functools.partial(_nsa_kernel, top_k=top_k, tq=tq),
        grid=(b, NSA_KV_GROUPS, s // tqs),
        in_specs=[pl.BlockSpec((1, tqs, rep * LANES), lambda bi, g, i: (bi, i, g)),
                  cmp_spec, cmp_spec, seq_spec, seq_spec, seq_spec, seq_spec,
                  pl.BlockSpec((1, 1, tqs, LANES), lambda bi, g, i: (g, bi, i, 0)),
                  _const_spec(overlap_t.shape)],
        out_specs=pl.BlockSpec((1, tqs, rep * HEAD_DIM), lambda bi, g, i: (bi, i, g)),
        out_shape=jax.ShapeDtypeStruct((b, s, NSA_HEADS * HEAD_DIM), bf16),
        scratch_shapes=[pltpu.VMEM((rows, LANES), bf16), pltpu.VMEM((rows, LANES), f32),
                        pltpu.VMEM((rows, LANES), f32), pltpu.VMEM((2, rows, LANES), f32),
                        pltpu.VMEM((2, rows, LANES), f32), pltpu.VMEM((2, rows, LANES), f32),
                        pltpu.VMEM((2, nslc, tq), f32), pltpu.VMEM((2, 3 * rep, tq, LANES), f32)],
        compiler_params=_cparams(("parallel", "parallel", "arbitrary")), name="nsa_attn",
    )(qn, kcmp, vcmp, ksa, vsa, kwa, vwa, gates, overlap_t)


def _outproj_kernel(x_ref, od_ref, on_ref, wa_ref, wb_ref, g_ref, b_ref, o_ref, *, alpha):
    parts = 2
    rows = x_ref.shape[0] // parts

    def mix(r):
        rs = slice(r * rows, (r + 1) * rows)
        return (jnp.dot(od_ref[rs, :], wa_ref[...], preferred_element_type=f32)
                + jnp.dot(on_ref[rs, :], wb_ref[...], preferred_element_type=f32))

    y_next = mix(0)
    for r in range(parts):
        rs = slice(r * rows, (r + 1) * rows)
        y = y_next
        if r + 1 < parts:
            y_next = mix(r + 1)
        o_ref[rs, :] = _layer_norm(alpha * x_ref[rs, :] + y, g_ref[...], b_ref[...])


def _outproj(x2, od, on, wa, wb, g, b, alpha):
    n, d_model = x2.shape
    tm = 2 * DENSE_ROWS
    row_spec = lambda w: pl.BlockSpec((tm, w), lambda i: (i, 0))
    return pl.pallas_call(
        functools.partial(_outproj_kernel, alpha=alpha),
        grid=(n // tm,),
        in_specs=[row_spec(d_model), row_spec(od.shape[1]), row_spec(on.shape[1]),
                  _const_spec(wa.shape), _const_spec(wb.shape), _const_spec(g.shape), _const_spec(b.shape)],
        out_specs=row_spec(d_model),
        out_shape=jax.ShapeDtypeStruct((n, d_model), f32),
        compiler_params=_cparams(("parallel",)), name="outproj_ln",
    )(x2, od, on, wa, wb, g, b)


def _ffn_kernel(h_ref, halo_ref, wu_ref, cw_ref, cb_ref, wd_ref, g_ref, b_ref, o_ref, hb_ref, u_ref, act_ref,
                *, alpha, tiles_per_seq):
    tm = h_ref.shape[0]
    tf = FFN_CHUNK
    n_chunks = wd_ref.shape[0] // tf
    h = h_ref[...]
    first = lax.rem(pl.program_id(0), tiles_per_seq) == 0
    halo = jnp.where(first, 0.0, halo_ref[...])
    d_ff = wd_ref.shape[0]
    hb_ref[...] = jnp.concatenate([halo, h], axis=0).astype(bf16)

    def up(c):
        for part, col in enumerate((c * tf, d_ff + c * tf)):
            u_ref[c % 2, :, part * tf:(part + 1) * tf] = jnp.dot(
                hb_ref[...], wu_ref[:, col:col + tf], preferred_element_type=f32)

    def conv(c, part):
        col = part * d_ff + c * tf
        u = cb_ref[:, col:col + tf]
        for k in range(CONV_WIDTH):
            u = u + cw_ref[k:k + 1, col:col + tf] * u_ref[
                c % 2, pl.ds(HALO - (CONV_WIDTH - 1) + k, tm), part * tf:(part + 1) * tf]
        return u

    up(0)
    for c in range(n_chunks):
        if c + 1 < n_chunks:
            up(c + 1)
        act_ref[:, c * tf:(c + 1) * tf] = (jax.nn.silu(conv(c, 0)) * conv(c, 1)).astype(bf16)
    down = jnp.dot(act_ref[...], wd_ref[...], preferred_element_type=f32)
    o_ref[...] = _layer_norm(alpha * h + down, g_ref[...], b_ref[...])


def _ffn(h1, wu, cw, cb, wd, g, b, alpha, seq):
    n, d_model = h1.shape
    tm = min(DENSE_ROWS, seq)
    single = dict(pipeline_mode=pl.Buffered(1))
    return pl.pallas_call(
        functools.partial(_ffn_kernel, alpha=alpha, tiles_per_seq=seq // tm),
        grid=(n // tm,),
        in_specs=[pl.BlockSpec((tm, d_model), lambda i: (i, 0)),
                  pl.BlockSpec((HALO, d_model), lambda i: (jnp.maximum(i * (tm // HALO) - 1, 0), 0)),
                  pl.BlockSpec(wu.shape, lambda i: (0, 0), **single),
                  _const_spec(cw.shape), _const_spec(cb.shape),
                  pl.BlockSpec(wd.shape, lambda i: (0, 0), **single),
                  _const_spec(g.shape), _const_spec(b.shape)],
        out_specs=pl.BlockSpec((tm, d_model), lambda i: (i, 0)),
        out_shape=jax.ShapeDtypeStruct((n, d_model), f32),
        scratch_shapes=[pltpu.VMEM((HALO + tm, d_model), bf16), pltpu.VMEM((2, HALO + tm, 2 * FFN_CHUNK), f32),
                        pltpu.VMEM((tm, wd.shape[0]), bf16)],
        compiler_params=_cparams(("parallel",)), name="ffn_ln",
    )(h1, h1, wu, cw, cb, wd, g, b)


def kernel(x, positions, w_in, lambda_q1, lambda_k1, lambda_q2, lambda_k2, diff_norm_g, cmp_pe_k, cmp_w1_k, cmp_b1_k, cmp_w2_k, cmp_pe_v, cmp_w1_v, cmp_b1_v, cmp_w2_v, w_out, ln1_g, ln1_b, w_up, conv_w, conv_b, w_down, ln2_g, ln2_b):
    b, s, d_model = x.shape
    depth = w_in.shape[0]
    n = b * s
    d_ff = w_down.shape[1]
    assert s % ATTN_TILE == 0 or s < ATTN_TILE
    assert s % SLC_BLOCK == 0 and s // SLC_BLOCK <= LANES - HEAD_DIM and d_ff % FFN_CHUNK == 0
    alpha = (2 * depth) ** 0.25
    consts = _rope_consts()
    pos_col = positions.reshape(n, 1)

    n_chunk = s // CMP_STRIDE
    n_cmp = n_chunk - CMP_BLOCK // CMP_STRIDE + 1
    n_slc = s // SLC_BLOCK
    top_k = min(SLC_TOPK, n_slc)
    cmp_pos = positions[:, CMP_BLOCK - 1::CMP_STRIDE][:, :n_cmp]
    cmp_pos = jnp.pad(cmp_pos, ((0, 0), (0, n_chunk - n_cmp))).reshape(b, n_chunk, 1)
    cs = np.arange(n_chunk)[None, :] * CMP_STRIDE
    ss = np.arange(n_slc)[:, None] * SLC_BLOCK
    ov = np.clip(np.minimum(cs + CMP_BLOCK, ss + SLC_BLOCK) - np.maximum(cs, ss), 0, None) / CMP_BLOCK
    ov[:, n_cmp:] = 0.0
    overlap_t = jnp.asarray(ov, dtype=bf16)

    h = x.reshape(n, d_model)
    for l in range(depth):
        lam_init = 0.8 - 0.6 * math.exp(-0.3 * l)
        d_in = w_in.shape[2]
        w_pad = jnp.pad(w_in[l], ((0, 0), (0, -d_in % LANES))).astype(bf16)
        qd, kd, vd, qn, kc, vc, ksa, vsa, kwa, vwa, gates = _proj(h, pos_col, w_pad, consts, s)

        kcmp, vcmp = _compress(
            kc.reshape(b, s, LANES), vc.reshape(b, s, LANES), cmp_pos, consts,
            _compress_weights(cmp_pe_k[l], cmp_w1_k[l], cmp_b1_k[l], cmp_w2_k[l]),
            _compress_weights(cmp_pe_v[l], cmp_w1_v[l], cmp_b1_v[l], cmp_w2_v[l]))

        lam_vec = jnp.stack([lambda_q1[l], lambda_k1[l], lambda_q2[l], lambda_k2[l]]).astype(f32)
        od = _diff_attention(qd.reshape(b, s, -1), kd.reshape(b, s, -1), vd.reshape(b, s, -1),
                             lam_vec, diff_norm_g[l].reshape(1, -1).astype(f32), lam_init)
        grouped = lambda a: a.reshape(NSA_KV_GROUPS, b, s, LANES)
        on = _nsa_attention(qn.reshape(b, s, -1), kcmp, vcmp, grouped(ksa), grouped(vsa), grouped(kwa),
                            grouped(vwa), grouped(gates), overlap_t, top_k)

        d_diff = od.shape[2]
        wo = w_out[l].astype(bf16)
        h = _outproj(h, od.reshape(n, -1), on.reshape(n, -1), wo[:d_diff], wo[d_diff:],
                     ln1_g[l].reshape(1, -1), ln1_b[l].reshape(1, -1), alpha)
        h = _ffn(h, w_up[l].astype(bf16), conv_w[l], conv_b[l].reshape(1, -1),
                 w_down[l].astype(bf16), ln2_g[l].reshape(1, -1), ln2_b[l].reshape(1, -1), alpha, s)
    return h.reshape(b, s, d_model)
```

```python
import functools
import math

import jax
import jax.numpy as jnp
import numpy as np
from jax import lax
from jax.experimental import pallas as pl
from jax.experimental.pallas import tpu as pltpu

f32 = jnp.float32
bf16 = jnp.bfloat16

LANES = 128
HEAD_DIM = 64
ROPE_DIM = HEAD_DIM // 4
ROPE_THETA = 500000.0
DIFF_HEADS = 4
NSA_HEADS = 8
NSA_KV_GROUPS = 2
NSA_GQA = NSA_HEADS // NSA_KV_GROUPS
CMP_BLOCK = 32
CMP_STRIDE = 16
CMP_HIDDEN = 2 * HEAD_DIM
SLC_BLOCK = 64
SLC_TOPK = 16
WINDOW = 512
CONV_WIDTH = 3
LN_EPS = 1e-5
RMS_EPS = 1e-5
NEG_INF = -1e30
SEL_BIAS = -1e9
FORCED_SCORE = 1e6
VMEM_LIMIT = 56 * 1024 * 1024

LOG2E = 1.4426950408889634
QK_SCALE = HEAD_DIM ** -0.5 * LOG2E

PROJ_ROWS = 512
ATTN_TILE = 4096
DIFF_K_TILE = 512
NSA_Q_TILE = 256
NSA_K_TILE = 512
NSA_TRIP_TILES = 4
SOFTMAX_ROWS = 256
DENSE_ROWS = 512
FFN_CHUNK = 256
HALO = 8


def _cparams(sem):
    return pltpu.CompilerParams(dimension_semantics=sem, vmem_limit_bytes=VMEM_LIMIT)


def _const_spec(shape):
    n = len(shape)
    return pl.BlockSpec(shape, lambda *_: (0,) * n)


def _layer_norm(y, g, b):
    mu = jnp.mean(y, axis=-1, keepdims=True)
    d = y - mu
    var = jnp.mean(d * d, axis=-1, keepdims=True)
    return d * lax.rsqrt(var + LN_EPS) * g + b


def _rope_consts():
    lane = np.arange(LANES)
    in_head = lane % HEAD_DIM
    half = ROPE_DIM // 2
    inv_freq = 1.0 / (ROPE_THETA ** (jnp.arange(half, dtype=f32) / half))
    c = jnp.zeros((8, LANES), f32)
    c = c.at[0].set(jnp.tile(inv_freq, LANES // half))
    c = c.at[1].set(jnp.asarray(in_head < ROPE_DIM, f32))
    c = c.at[2].set(jnp.asarray(np.where(in_head < half, -1.0, np.where(in_head < ROPE_DIM, 1.0, 0.0)), f32))
    c = c.at[3].set(jnp.asarray(in_head < half, f32))
    return c


def _rope_tables(pos_col, c_ref):
    ang = pos_col.astype(f32) * c_ref[0:1, :]
    cos_t = jnp.where(c_ref[1:2, :] > 0.0, jnp.cos(ang), 1.0)
    sin_t = jnp.sin(ang) * c_ref[2:3, :]
    return cos_t, sin_t, c_ref[3:4, :] > 0.0


def _rope(y, tables):
    cos_t, sin_t, first = tables
    half = ROPE_DIM // 2
    partner = jnp.where(first, pltpu.roll(y, LANES - half, 1), pltpu.roll(y, half, 1))
    return y * cos_t + partner * sin_t


def _proj_kernel(x_ref, pos_ref, w_ref, c_ref,
                 qd_ref, kd_ref, vd_ref, qn_ref, kc_ref, vc_ref, ks_ref, vs_ref, kw_ref, vw_ref, g_ref,
                 *, seq):
    tm = x_ref.shape[0]
    xb = x_ref[...].astype(bf16)
    tables = _rope_tables(pos_ref[...], c_ref)
    lane = lax.broadcasted_iota(jnp.int32, (tm, LANES), 1)
    low = lane < HEAD_DIM
    scale = QK_SCALE

    hd2 = DIFF_HEADS * 2 * HEAD_DIM

    def diff_q(y):
        for c in range(hd2 // LANES):
            sl = slice(c * LANES, (c + 1) * LANES)
            qd_ref[:, sl] = (_rope(y[:, sl], tables) * scale).astype(bf16)

    def diff_k(y):
        for c in range(hd2 // LANES):
            sl = slice(c * LANES, (c + 1) * LANES)
            kd_ref[:, sl] = _rope(y[:, sl], tables).astype(bf16)

    def diff_v(y):
        ones = jnp.ones((tm, LANES), bf16)
        for c in range(hd2 // LANES):
            vd_ref[:, (2 * c) * LANES:(2 * c + 1) * LANES] = y[:, c * LANES:(c + 1) * LANES].astype(bf16)
            vd_ref[:, (2 * c + 1) * LANES:(2 * c + 2) * LANES] = ones

    def nsa_q(y):
        for c in range(NSA_HEADS // 2):
            slab = _rope(y[:, c * LANES:(c + 1) * LANES], tables) * scale
            qn_ref[:, (2 * c) * LANES:(2 * c + 1) * LANES] = jnp.where(low, slab, 0.0).astype(bf16)
            qn_ref[:, (2 * c + 1) * LANES:(2 * c + 2) * LANES] = jnp.where(
                low, pltpu.roll(slab, HEAD_DIM, 1), 0.0).astype(bf16)

    def cmp_kv(y):
        kc_ref[...] = y[:, :LANES]
        vc_ref[...] = y[:, LANES:]

    def grouped(ref, y, fill):
        ref[0] = jnp.where(low, y, fill).astype(ref.dtype)
        ref[1] = jnp.where(low, pltpu.roll(y, HEAD_DIM, 1), fill).astype(ref.dtype)

    def sel_kv(y):
        row = lax.broadcasted_iota(jnp.int32, (tm, LANES), 0) + lax.rem(pl.program_id(0) * tm, seq)
        onehot = jnp.where(lane - HEAD_DIM == row // SLC_BLOCK, 1.0, 0.0)
        grouped(ks_ref, _rope(y[:, :LANES], tables), onehot)
        grouped(vs_ref, y[:, LANES:], 1.0)

    def win_kv(y):
        grouped(kw_ref, _rope(y[:, :LANES], tables), 0.0)
        grouped(vw_ref, y[:, LANES:], 1.0)

    def gate(y):
        sig = jax.nn.sigmoid(y)
        g_ref[0] = sig
        g_ref[1] = pltpu.roll(sig, LANES - NSA_GQA * 3, 1)

    widths = [hd2, hd2, hd2, NSA_HEADS * HEAD_DIM, 2 * LANES, 2 * LANES, 2 * LANES, LANES]
    epilogues = [diff_q, diff_k, diff_v, nsa_q, cmp_kv, sel_kv, win_kv, gate]
    cols = [sum(widths[:n]) for n in range(len(widths))]

    def seg(n):
        return jnp.dot(xb, w_ref[:, cols[n]:cols[n] + widths[n]], preferred_element_type=f32)

    y_next = seg(0)
    for n, epilogue in enumerate(epilogues):
        y = y_next
        if n + 1 < len(epilogues):
            y_next = seg(n + 1)
        epilogue(y)


def _proj(x2, pos_col, w_pad, consts, seq):
    n, d_model = x2.shape
    tm = PROJ_ROWS
    hd2 = DIFF_HEADS * 2 * HEAD_DIM
    row_spec = lambda w: pl.BlockSpec((tm, w), lambda i: (i, 0))
    grp_spec = pl.BlockSpec((NSA_KV_GROUPS, tm, LANES), lambda i: (0, i, 0))
    out_shape = (
        jax.ShapeDtypeStruct((n, hd2), bf16), jax.ShapeDtypeStruct((n, hd2), bf16),
        jax.ShapeDtypeStruct((n, 2 * hd2), bf16), jax.ShapeDtypeStruct((n, NSA_HEADS * LANES), bf16),
        jax.ShapeDtypeStruct((n, LANES), f32), jax.ShapeDtypeStruct((n, LANES), f32),
        jax.ShapeDtypeStruct((NSA_KV_GROUPS, n, LANES), bf16), jax.ShapeDtypeStruct((NSA_KV_GROUPS, n, LANES), bf16),
        jax.ShapeDtypeStruct((NSA_KV_GROUPS, n, LANES), bf16), jax.ShapeDtypeStruct((NSA_KV_GROUPS, n, LANES), bf16),
        jax.ShapeDtypeStruct((NSA_KV_GROUPS, n, LANES), f32),
    )
    out_specs = (row_spec(hd2), row_spec(hd2), row_spec(2 * hd2), row_spec(NSA_HEADS * LANES),
                 row_spec(LANES), row_spec(LANES), grp_spec, grp_spec, grp_spec, grp_spec, grp_spec)
    return pl.pallas_call(
        functools.partial(_proj_kernel, seq=seq),
        grid=(n // tm,),
        in_specs=[row_spec(d_model), pl.BlockSpec((tm, 1), lambda i: (i, 0)),
                  _const_spec(w_pad.shape), _const_spec(consts.shape)],
        out_specs=out_specs, out_shape=out_shape,
        compiler_params=_cparams(("parallel",)), name="proj",
    )(x2, pos_col, w_pad, consts)


def _compress_one(c_ref, pea_ref, peb_ref, w1a_ref, w1b_ref, b1_ref, w2_ref):
    n = c_ref.shape[1] // CMP_STRIDE
    ck = jnp.concatenate([c_ref[0, pl.ds(t, n, stride=CMP_STRIDE), :] for t in range(CMP_STRIDE)], axis=1)
    hid_a = jnp.dot((ck + pea_ref[...]).astype(bf16), w1a_ref[...], preferred_element_type=f32)
    hid_b = jnp.dot((ck + peb_ref[...]).astype(bf16), w1b_ref[...], preferred_element_type=f32)
    hid = hid_a + pltpu.roll(hid_b, n - 1, 0) + b1_ref[...]
    hid = jax.nn.gelu(hid)
    return jnp.dot(hid.astype(bf16), w2_ref[...], preferred_element_type=f32)


def _compress_kernel(kc_ref, vc_ref, pos_ref, c_ref,
                     pak_ref, pbk_ref, w1ak_ref, w1bk_ref, b1k_ref, w2k_ref,
                     pav_ref, pbv_ref, w1av_ref, w1bv_ref, b1v_ref, w2v_ref,
                     kcmp_ref, vcmp_ref):
    k = _compress_one(kc_ref, pak_ref, pbk_ref, w1ak_ref, w1bk_ref, b1k_ref, w2k_ref)
    k = _rope(k, _rope_tables(pos_ref[0], c_ref))
    low = lax.broadcasted_iota(jnp.int32, k.shape, 1) < HEAD_DIM
    kcmp_ref[0, 0] = jnp.where(low, k, 0.0).astype(bf16)
    kcmp_ref[0, 1] = jnp.where(low, pltpu.roll(k, HEAD_DIM, 1), 0.0).astype(bf16)
    v = _compress_one(vc_ref, pav_ref, pbv_ref, w1av_ref, w1bv_ref, b1v_ref, w2v_ref)
    vcmp_ref[0, 0] = jnp.where(low, v, 0.0).astype(bf16)
    vcmp_ref[0, 1] = jnp.where(low, pltpu.roll(v, HEAD_DIM, 1), 0.0).astype(bf16)


def _compress_weights(pe, w1, b1, w2):
    r = CMP_BLOCK // CMP_STRIDE
    assert r == 2
    eye = jnp.eye(NSA_KV_GROUPS, dtype=f32)
    w1r = w1.reshape(CMP_BLOCK, HEAD_DIM, CMP_HIDDEN)
    per = CMP_STRIDE * NSA_KV_GROUPS * HEAD_DIM

    def big(part):
        return jnp.einsum('tcm,gh->tgchm', part, eye).reshape(per, NSA_KV_GROUPS * CMP_HIDDEN).astype(bf16)

    def pe_row(part):
        return jnp.broadcast_to(part[:, None, :], (CMP_STRIDE, NSA_KV_GROUPS, HEAD_DIM)).reshape(1, per)

    w2b = jnp.einsum('mc,gh->gmhc', w2, eye).reshape(NSA_KV_GROUPS * CMP_HIDDEN, NSA_KV_GROUPS * HEAD_DIM)
    return (pe_row(pe[:CMP_STRIDE]), pe_row(pe[CMP_STRIDE:]), big(w1r[:CMP_STRIDE]), big(w1r[CMP_STRIDE:]),
            jnp.tile(b1, NSA_KV_GROUPS).reshape(1, -1), w2b.astype(bf16))


def _compress(kc, vc, cmp_pos, consts, wk, wv):
    b, s, width = kc.shape
    nch = s // CMP_STRIDE
    seq_spec = pl.BlockSpec((1, s, width), lambda i: (i, 0, 0))
    w_specs = [_const_spec(w.shape) for w in wk + wv]
    return pl.pallas_call(
        _compress_kernel,
        grid=(b,),
        in_specs=[seq_spec, seq_spec, pl.BlockSpec((1, nch, 1), lambda i: (i, 0, 0)),
                  _const_spec(consts.shape)] + w_specs,
        out_specs=(pl.BlockSpec((1, NSA_KV_GROUPS, nch, LANES), lambda i: (i, 0, 0, 0)),
                   pl.BlockSpec((1, NSA_KV_GROUPS, nch, LANES), lambda i: (i, 0, 0, 0))),
        out_shape=(jax.ShapeDtypeStruct((b, NSA_KV_GROUPS, nch, LANES), bf16),
                   jax.ShapeDtypeStruct((b, NSA_KV_GROUPS, nch, LANES), bf16)),
        compiler_params=_cparams(("parallel",)), name="compress",
    )(kc, vc, cmp_pos, consts, *wk, *wv)


def _softmax_init(m_ref, acc_ref):
    m_ref[...] = jnp.full(m_ref.shape, NEG_INF, f32)
    acc_ref[...] = jnp.zeros(acc_ref.shape, f32)


def _softmax_step(s, v_ones, m_ref, acc_ref, rs):
    m_prev = m_ref[rs, :]
    m_new = jnp.maximum(m_prev, jnp.max(s, axis=1, keepdims=True))
    alpha = jnp.exp2(m_prev - m_new)
    p = jnp.exp2((s - jnp.concatenate([m_new] * (s.shape[1] // LANES), axis=1)).astype(bf16))
    pv = jnp.dot(p, v_ones, preferred_element_type=f32)
    acc_ref[rs, :] = jnp.concatenate([alpha] * (acc_ref.shape[1] // LANES), axis=1) * acc_ref[rs, :] + pv
    m_ref[rs, :] = m_new


def _scores(q, k):
    return lax.dot_general(q, k, (((1,), (1,)), ((), ())), preferred_element_type=f32)


def _run_steps(steps, m_ref, acc_ref):
    s_next = steps[0][0]()
    for n, (_, v_ones_fn, rows) in enumerate(steps):
        s = s_next
        if n + 1 < len(steps):
            s_next = steps[n + 1][0]()
        _softmax_step(s, v_ones_fn(), m_ref, acc_ref, rows)


def _diff_kernel(q_ref, k_ref, v_ref, lam_ref, g_ref, o_ref, q2_ref, m_ref, acc_ref, *, lam_init):
    tq = q_ref.shape[1]
    tk = min(DIFF_K_TILE, tq)
    per_q = tq // tk
    rc = min(SOFTMAX_ROWS, tq)
    vdim = o_ref.shape[2]
    i = pl.program_id(2)
    q = q_ref[0]
    low = lax.broadcasted_iota(jnp.int32, q.shape, 1) < HEAD_DIM
    zero = jnp.zeros_like(q)
    q2_ref[0:tq, :] = jnp.where(low, q, zero)
    q2_ref[tq:2 * tq, :] = jnp.where(low, zero, q)
    _softmax_init(m_ref, acc_ref)

    def tile_steps(j, k_off=None):
        off = pl.multiple_of(j * tk, tk)
        steps = []
        for c in range(2 * tq // rc):
            q_off = (c * rc) % tq
            cols = tk if k_off is None else min(tk, q_off + rc - k_off)
            if cols <= 0:
                continue
            masked = k_off is not None and q_off - k_off + 1 < cols

            def score(c=c, q_off=q_off, cols=cols, masked=masked):
                s = _scores(q2_ref[c * rc:(c + 1) * rc, :], k_ref[0, pl.ds(off, cols), :])
                if masked:
                    r = q_off + lax.broadcasted_iota(jnp.int32, (rc, cols), 0)
                    s = jnp.where(r >= k_off + lax.broadcasted_iota(jnp.int32, (rc, cols), 1), s, NEG_INF)
                return s

            steps.append((score, lambda cols=cols: v_ref[0, pl.ds(off, cols), :], slice(c * rc, (c + 1) * rc)))
        return steps

    group = 2 if per_q % 2 == 0 else 1

    def body(j, carry):
        _run_steps(sum((tile_steps(group * j + t) for t in range(group)), []), m_ref, acc_ref)
        return carry

    lax.fori_loop(0, i * (per_q // group), body, 0)
    _run_steps(sum((tile_steps(i * per_q + t, t * tk) for t in range(per_q)), []), m_ref, acc_ref)

    lam_v = lam_ref[...]
    lam = (jnp.exp(jnp.sum(lam_v[0:1] * lam_v[1:2], axis=1, keepdims=True))
           - jnp.exp(jnp.sum(lam_v[2:3] * lam_v[3:4], axis=1, keepdims=True)) + lam_init)
    o = acc_ref[:, :vdim] / acc_ref[:, vdim:]
    od = o[:tq] - lam * o[tq:]
    od = od * lax.rsqrt(jnp.mean(od * od, axis=-1, keepdims=True) + RMS_EPS)
    o_ref[0] = (od * g_ref[...] * (1.0 - lam_init)).astype(o_ref.dtype)


def _diff_attention(qd, kd, vd, lam_vec, diff_g, lam_init):
    b, s, _ = qd.shape
    tq = min(ATTN_TILE, s)
    vdim = vd.shape[2] // DIFF_HEADS // 2
    assert vdim == LANES
    return pl.pallas_call(
        functools.partial(_diff_kernel, lam_init=lam_init),
        grid=(b, DIFF_HEADS, s // tq),
        in_specs=[pl.BlockSpec((1, tq, LANES), lambda bi, h, i: (bi, i, h)),
                  pl.BlockSpec((1, s, LANES), lambda bi, h, i: (bi, 0, h)),
                  pl.BlockSpec((1, s, 2 * vdim), lambda bi, h, i: (bi, 0, h)),
                  _const_spec(lam_vec.shape), _const_spec(diff_g.shape)],
        out_specs=pl.BlockSpec((1, tq, vdim), lambda bi, h, i: (bi, i, h)),
        out_shape=jax.ShapeDtypeStruct((b, s, DIFF_HEADS * vdim), bf16),
        scratch_shapes=[pltpu.VMEM((2 * tq, LANES), bf16), pltpu.VMEM((2 * tq, LANES), f32),
                        pltpu.VMEM((2 * tq, 2 * vdim), f32)],
        compiler_params=_cparams(("parallel", "parallel", "arbitrary")), name="diff_attn",
    )(qd, kd, vd, lam_vec, diff_g)


def _nsa_kernel(q_ref, kc_ref, vc_ref, ks_ref, vs_ref, kw_ref, vw_ref, g_ref, ov_ref, o_ref,
                qa_ref, m_ref, acc_ref, mw_ref, accw_ref, oc_ref, imp_ref, rank_ref, gate_ref, *, top_k):
    tq = q_ref.shape[1]
    seq = ks_ref.shape[2]
    tk = min(NSA_K_TILE, seq)
    ncp = kc_ref.shape[2]
    nslc = ov_ref.shape[0]
    rep = NSA_GQA
    i = pl.program_id(2)
    s0 = i * tq
    heads = [slice(r * tq, (r + 1) * tq) for r in range(rep)]

    def tile_steps(q_of, k_ref, v_ref, back, width, lo=None, hi=None):
        off = pl.multiple_of(s0 - back, tq)
        steps = []
        for r in range(rep):

            def score(r=r):
                s = _scores(q_of(r), k_ref[0, 0, pl.ds(off, width), :])
                if lo is None and hi is None:
                    return s
                d = (lax.broadcasted_iota(jnp.int32, (tq, width), 1)
                     - lax.broadcasted_iota(jnp.int32, (tq, width), 0))
                if hi is not None:
                    s = jnp.where(d <= hi, s, NEG_INF)
                if lo is not None:
                    s = jnp.where(d > lo, s, NEG_INF)
                return s

            steps.append((score, lambda: v_ref[0, 0, pl.ds(off, width), :], heads[r]))
        return steps

    def normalized(ref):
        acc = ref[...]
        low = lax.broadcasted_iota(jnp.int32, acc.shape, 1) < HEAD_DIM
        return acc / jnp.where(low, pltpu.roll(acc, HEAD_DIM, 1), 1.0)

    def compressed_branch(width, window_steps):
        t_col = s0 + lax.broadcasted_iota(jnp.int32, (tq, width), 0)
        cmp_end = lax.broadcasted_iota(jnp.int32, (tq, width), 1) * CMP_STRIDE + (CMP_BLOCK - 1)
        cmp_ok = cmp_end <= t_col
        kc = kc_ref[0, 0, 0:width, :]
        vc = vc_ref[0, 0, 0:width, :]
        raw = [_scores(q_ref[0, :, r * LANES:(r + 1) * LANES], kc) for r in range(rep)]
        _softmax_init(mw_ref, accw_ref)
        _run_steps(window_steps, mw_ref, accw_ref)
        psum = jnp.zeros((tq, width), f32)
        for r in range(rep):
            s = jnp.where(cmp_ok, raw[r], NEG_INF)
            p = jnp.where(cmp_ok, jnp.exp2(s - jnp.max(s, axis=1, keepdims=True)), 0.0)
            den = jnp.sum(p, axis=1, keepdims=True)
            p = p / jnp.where(den > 0.0, den, 1.0)
            psum = psum + p
            oc_ref[heads[r], :] = jnp.dot(p.astype(bf16), vc, preferred_element_type=f32)
        p_hi = psum.astype(bf16)
        p_lo = (psum - p_hi.astype(f32)).astype(bf16)
        ov = ov_ref[:, 0:width]
        imp_ref[...] = _scores(ov, p_hi) + _scores(ov, p_lo)
        gates = g_ref[0, 0]
        for c in range(3 * rep):
            gate_ref[c] = jnp.broadcast_to(gates[:, c:c + 1], (tq, LANES))

    win = functools.partial(tile_steps, lambda r: q_ref[0, :, r * LANES:(r + 1) * LANES], kw_ref, vw_ref)
    n_back = WINDOW // tq
    windows = [lambda n=n: win(n * tq, (n + 1) * tq, hi=n * tq) for n in range(n_back)]
    windows.append(lambda: win(WINDOW, WINDOW + tq, lo=0, hi=WINDOW))
    half = ncp // 2
    has_half = half % LANES == 0
    last_half_tile = half // (tq // CMP_STRIDE) - 1 if has_half else -1
    assert not has_half or last_half_tile >= n_back
    for n, window in enumerate(windows):
        this_window = (i == n) if n < n_back else (i >= n_back)
        if not has_half:
            pl.when(this_window)(functools.partial(lambda w: compressed_branch(ncp, w()), window))
        elif n < n_back:
            pl.when(this_window)(functools.partial(lambda w: compressed_branch(half, w()), window))
        else:
            pl.when(this_window & (i <= last_half_tile))(
                functools.partial(lambda w: compressed_branch(half, w()), window))
            pl.when(i > last_half_tile)(functools.partial(lambda w: compressed_branch(ncp, w()), window))

    blk = lax.broadcasted_iota(jnp.int32, (nslc, tq), 0)
    cur = (s0 + lax.broadcasted_iota(jnp.int32, (nslc, tq), 1)) // SLC_BLOCK
    imp = jnp.where(blk > cur, -1.0, imp_ref[...])
    imp = jnp.where((blk == 0) | (blk == cur) | (blk == cur - 1), FORCED_SCORE, imp)
    rank_ref[...] = jnp.zeros((nslc, tq), f32)
    per_tile = tq // SLC_BLOCK
    for first in range(0, nslc, per_tile):

        @pl.when(first <= i * per_tile)
        def _():
            rank = rank_ref[...]
            for jp in range(first, min(first + per_tile, nslc)):
                other = imp[jp:jp + 1, :]
                ahead = jnp.where(blk > jp, jnp.where(other >= imp, 1.0, 0.0), jnp.where(other > imp, 1.0, 0.0))
                rank = rank + ahead
            rank_ref[...] = rank

    bias_t = jnp.where(rank_ref[...] < top_k, 0.0, SEL_BIAS)
    pieces = [jnp.zeros((HEAD_DIM, tq), f32), bias_t]
    if nslc < LANES - HEAD_DIM:
        pieces.append(jnp.zeros((LANES - HEAD_DIM - nslc, tq), f32))
    bias = jnp.concatenate(pieces, axis=0).T.astype(bf16)
    for r in range(rep):
        qa_ref[heads[r], :] = q_ref[0, :, r * LANES:(r + 1) * LANES] + bias

    _softmax_init(m_ref, acc_ref)
    sel = functools.partial(tile_steps, lambda r: qa_ref[heads[r], :], ks_ref, vs_ref)
    trip = NSA_TRIP_TILES * tk

    def sel_body(j, carry):
        back = s0 - j * trip
        _run_steps(sum((sel(back - t * tk, tk) for t in range(NSA_TRIP_TILES)), []), m_ref, acc_ref)
        return carry

    lax.fori_loop(0, s0 // trip, sel_body, 0)
    tail = lax.rem(i, trip // tq)
    for n in range(trip // tq):
        steps = sum((sel(n * tq - t * tk, tk) for t in range(n // 2)), [])
        steps += sel(tq, tk, hi=tq) if n % 2 else sel(0, tq, hi=0)
        pl.when(tail == n)(functools.partial(_run_steps, steps, m_ref, acc_ref))

    o_sel = normalized(acc_ref)
    o_win = normalized(accw_ref)

    lane = lax.broadcasted_iota(jnp.int32, (tq, LANES), 1)
    gated = []
    for r in range(rep):
        gated.append(gate_ref[3 * r] * oc_ref[heads[r], :] + gate_ref[3 * r + 1] * o_sel[heads[r]]
                     + gate_ref[3 * r + 2] * o_win[heads[r]])
    for c in range(rep // 2):
        o_ref[0, :, c * LANES:(c + 1) * LANES] = jnp.where(
            lane < HEAD_DIM, gated[2 * c], pltpu.roll(gated[2 * c + 1], HEAD_DIM, 1)).astype(o_ref.dtype)


def _nsa_attention(qn, kcmp, vcmp, ksa, vsa, kwa, vwa, gates, overlap_t, top_k):
    b, s, _ = qn.shape
    tq = min(NSA_Q_TILE, s)
    ncp = kcmp.shape[2]
    rep = NSA_GQA
    assert WINDOW == 2 * tq and min(NSA_K_TILE, s) == 2 * tq
    cmp_spec = pl.BlockSpec((1, 1, ncp, LANES), lambda bi, g, i: (bi, g, 0, 0))
    seq_spec = pl.BlockSpec((1, 1, s, LANES), lambda bi, g, i: (g, bi, 0, 0))
    rows = rep * tq
    return pl.pallas_call(
        functools.partial(_nsa_kernel, top_k=top_k),
        grid=(b, NSA_KV_GROUPS, s // tq),
        in_specs=[pl.BlockSpec((1, tq, rep * LANES), lambda bi, g, i: (bi, i, g)),
                  cmp_spec, cmp_spec, seq_spec, seq_spec, seq_spec, seq_spec,
                  pl.BlockSpec((1, 1, tq, LANES), lambda bi, g, i: (g, bi, i, 0)),
                  _const_spec(overlap_t.shape)],
        out_specs=pl.BlockSpec((1, tq, rep * HEAD_DIM), lambda bi, g, i: (bi, i, g)),
        out_shape=jax.ShapeDtypeStruct((b, s, NSA_HEADS * HEAD_DIM), bf16),
        scratch_shapes=[pltpu.VMEM((rows, LANES), bf16)] + [pltpu.VMEM((rows, LANES), f32)] * 5 + [
                        pltpu.VMEM(overlap_t.shape[:1] + (tq,), f32),
                        pltpu.VMEM(overlap_t.shape[:1] + (tq,), f32), pltpu.VMEM((3 * rep, tq, LANES), f32)],
        compiler_params=_cparams(("parallel", "parallel", "arbitrary")), name="nsa_attn",
    )(qn, kcmp, vcmp, ksa, vsa, kwa, vwa, gates, overlap_t)


def _outproj_kernel(x_ref, od_ref, on_ref, wa_ref, wb_ref, g_ref, b_ref, o_ref, *, alpha):
    parts = 4
    rows = x_ref.shape[0] // parts

    def mix(r):
        rs = slice(r * rows, (r + 1) * rows)
        return (jnp.dot(od_ref[rs, :], wa_ref[...], preferred_element_type=f32)
                + jnp.dot(on_ref[rs, :], wb_ref[...], preferred_element_type=f32))

    y_next = mix(0)
    for r in range(parts):
        rs = slice(r * rows, (r + 1) * rows)
        y = y_next
        if r + 1 < parts:
            y_next = mix(r + 1)
        o_ref[rs, :] = _layer_norm(alpha * x_ref[rs, :] + y, g_ref[...], b_ref[...])


def _outproj(x2, od, on, wa, wb, g, b, alpha):
    n, d_model = x2.shape
    tm = 2 * DENSE_ROWS
    row_spec = lambda w: pl.BlockSpec((tm, w), lambda i: (i, 0))
    return pl.pallas_call(
        functools.partial(_outproj_kernel, alpha=alpha),
        grid=(n // tm,),
        in_specs=[row_spec(d_model), row_spec(od.shape[1]), row_spec(on.shape[1]),
                  _const_spec(wa.shape), _const_spec(wb.shape), _const_spec(g.shape), _const_spec(b.shape)],
        out_specs=row_spec(d_model),
        out_shape=jax.ShapeDtypeStruct((n, d_model), f32),
        compiler_params=_cparams(("parallel",)), name="outproj_ln",
    )(x2, od, on, wa, wb, g, b)


def _ffn_kernel(h_ref, halo_ref, wu_ref, cw_ref, cb_ref, wd_ref, g_ref, b_ref, o_ref, hb_ref, u_ref, act_ref,
                *, alpha, tiles_per_seq):
    tm = h_ref.shape[0]
    tf = FFN_CHUNK
    n_chunks = wd_ref.shape[0] // tf
    h = h_ref[...]
    first = lax.rem(pl.program_id(0), tiles_per_seq) == 0
    halo = jnp.where(first, 0.0, halo_ref[...])
    d_ff = wd_ref.shape[0]
    hb_ref[...] = jnp.concatenate([halo, h], axis=0).astype(bf16)

    def up(c):
        for part, col in enumerate((c * tf, d_ff + c * tf)):
            u_ref[c % 2, :, part * tf:(part + 1) * tf] = jnp.dot(
                hb_ref[...], wu_ref[:, col:col + tf], preferred_element_type=f32)

    def conv(c, part):
        col = part * d_ff + c * tf
        u_all = u_ref[c % 2, :, part * tf:(part + 1) * tf]
        u = cb_ref[:, col:col + tf]
        for k in range(CONV_WIDTH):
            lag = CONV_WIDTH - 1 - k
            shifted = u_all if lag == 0 else pltpu.roll(u_all, lag, 0)
            u = u + cw_ref[k:k + 1, col:col + tf] * shifted[HALO:, :]
        return u

    up(0)
    for c in range(n_chunks):
        if c + 1 < n_chunks:
            up(c + 1)
        act_ref[:, c * tf:(c + 1) * tf] = (jax.nn.silu(conv(c, 0)) * conv(c, 1)).astype(bf16)
    parts = 2
    rows = tm // parts
    down = lambda r: jnp.dot(act_ref[r * rows:(r + 1) * rows, :], wd_ref[...], preferred_element_type=f32)
    y_next = down(0)
    for r in range(parts):
        rs = slice(r * rows, (r + 1) * rows)
        y = y_next
        if r + 1 < parts:
            y_next = down(r + 1)
        o_ref[rs, :] = _layer_norm(alpha * h_ref[rs, :] + y, g_ref[...], b_ref[...])


def _ffn(h1, wu, cw, cb, wd, g, b, alpha, seq):
    n, d_model = h1.shape
    tm = min(DENSE_ROWS, seq)
    single = dict(pipeline_mode=pl.Buffered(1))
    return pl.pallas_call(
        functools.partial(_ffn_kernel, alpha=alpha, tiles_per_seq=seq // tm),
        grid=(n // tm,),
        in_specs=[pl.BlockSpec((tm, d_model), lambda i: (i, 0)),
                  pl.BlockSpec((HALO, d_model), lambda i: (jnp.maximum(i * (tm // HALO) - 1, 0), 0)),
                  pl.BlockSpec(wu.shape, lambda i: (0, 0), **single),
                  _const_spec(cw.shape), _const_spec(cb.shape),
                  pl.BlockSpec(wd.shape, lambda i: (0, 0), **single),
                  _const_spec(g.shape), _const_spec(b.shape)],
        out_specs=pl.BlockSpec((tm, d_model), lambda i: (i, 0)),
        out_shape=jax.ShapeDtypeStruct((n, d_model), f32),
        scratch_shapes=[pltpu.VMEM((HALO + tm, d_model), bf16), pltpu.VMEM((2, HALO + tm, 2 * FFN_CHUNK), f32),
                        pltpu.VMEM((tm, wd.shape[0]), bf16)],
        compiler_params=_cparams(("parallel",)), name="ffn_ln",
    )(h1, h1, wu, cw, cb, wd, g, b)


def kernel(x, positions, w_in, lambda_q1, lambda_k1, lambda_q2, lambda_k2, diff_norm_g, cmp_pe_k, cmp_w1_k, cmp_b1_k, cmp_w2_k, cmp_pe_v, cmp_w1_v, cmp_b1_v, cmp_w2_v, w_out, ln1_g, ln1_b, w_up, conv_w, conv_b, w_down, ln2_g, ln2_b):
    b, s, d_model = x.shape
    depth = w_in.shape[0]
    n = b * s
    d_ff = w_down.shape[1]
    assert s % ATTN_TILE == 0 or s < ATTN_TILE
    assert s % SLC_BLOCK == 0 and s // SLC_BLOCK <= LANES - HEAD_DIM and d_ff % FFN_CHUNK == 0
    alpha = (2 * depth) ** 0.25
    consts = _rope_consts()
    pos_col = positions.reshape(n, 1)

    n_chunk = s // CMP_STRIDE
    n_cmp = n_chunk - CMP_BLOCK // CMP_STRIDE + 1
    n_slc = s // SLC_BLOCK
    top_k = min(SLC_TOPK, n_slc)
    cmp_pos = positions[:, CMP_BLOCK - 1::CMP_STRIDE][:, :n_cmp]
    cmp_pos = jnp.pad(cmp_pos, ((0, 0), (0, n_chunk - n_cmp))).reshape(b, n_chunk, 1)
    cs = np.arange(n_chunk)[None, :] * CMP_STRIDE
    ss = np.arange(n_slc)[:, None] * SLC_BLOCK
    ov = np.clip(np.minimum(cs + CMP_BLOCK, ss + SLC_BLOCK) - np.maximum(cs, ss), 0, None) / CMP_BLOCK
    ov[:, n_cmp:] = 0.0
    overlap_t = jnp.asarray(ov, dtype=bf16)

    h = x.reshape(n, d_model)
    for l in range(depth):
        lam_init = 0.8 - 0.6 * math.exp(-0.3 * l)
        d_in = w_in.shape[2]
        w_pad = jnp.pad(w_in[l], ((0, 0), (0, -d_in % LANES))).astype(bf16)
        qd, kd, vd, qn, kc, vc, ksa, vsa, kwa, vwa, gates = _proj(h, pos_col, w_pad, consts, s)

        kcmp, vcmp = _compress(
            kc.reshape(b, s, LANES), vc.reshape(b, s, LANES), cmp_pos, consts,
            _compress_weights(cmp_pe_k[l], cmp_w1_k[l], cmp_b1_k[l], cmp_w2_k[l]),
            _compress_weights(cmp_pe_v[l], cmp_w1_v[l], cmp_b1_v[l], cmp_w2_v[l]))

        lam_vec = jnp.stack([lambda_q1[l], lambda_k1[l], lambda_q2[l], lambda_k2[l]]).astype(f32)
        od = _diff_attention(qd.reshape(b, s, -1), kd.reshape(b, s, -1), vd.reshape(b, s, -1),
                             lam_vec, diff_norm_g[l].reshape(1, -1).astype(f32), lam_init)
        grouped = lambda a: a.reshape(NSA_KV_GROUPS, b, s, LANES)
        on = _nsa_attention(qn.reshape(b, s, -1), kcmp, vcmp, grouped(ksa), grouped(vsa), grouped(kwa),
                            grouped(vwa), grouped(gates), overlap_t, top_k)

        d_diff = od.shape[2]
        wo = w_out[l].astype(bf16)
        h = _outproj(h, od.reshape(n, -1), on.reshape(n, -1), wo[:d_diff], wo[d_diff:],
                     ln1_g[l].reshape(1, -1), ln1_b[l].reshape(1, -1), alpha)
        h = _ffn(h, w_up[l].astype(bf16), conv_w[l], conv_b[l].reshape(1, -1),
                 w_down[l].astype(bf16), ln2_g[l].reshape(1, -1), ln2_b[l].reshape(1, -1), alpha, s)
    return h.reshape(b, s, d_model)
```

```python
import functools
import math

import jax
import jax.numpy as jnp
import numpy as np
from jax import lax
from jax.experimental import pallas as pl
from jax.experimental.pallas import tpu as pltpu

f32 = jnp.float32
bf16 = jnp.bfloat16

LANES = 128
HEAD_DIM = 64
ROPE_DIM = HEAD_DIM // 4
ROPE_THETA = 500000.0
DIFF_HEADS = 4
NSA_HEADS = 8
NSA_KV_GROUPS = 2
NSA_GQA = NSA_HEADS // NSA_KV_GROUPS
CMP_BLOCK = 32
CMP_STRIDE = 16
CMP_HIDDEN = 2 * HEAD_DIM
SLC_BLOCK = 64
SLC_TOPK = 16
WINDOW = 512
CONV_WIDTH = 3
LN_EPS = 1e-5
RMS_EPS = 1e-5
NEG_INF = -1e30
SEL_BIAS = -1e9
FORCED_SCORE = 1e6
VMEM_LIMIT = 56 * 1024 * 1024

LOG2E = 1.4426950408889634
QK_SCALE = HEAD_DIM ** -0.5 * LOG2E

PROJ_ROWS = 512
ATTN_TILE = 4096
DIFF_K_TILE = 512
NSA_Q_TILE = 256
NSA_K_TILE = 512
NSA_TRIP_TILES = 4
SOFTMAX_ROWS = 256
DENSE_ROWS = 512
FFN_CHUNK = 256
HALO = 8


def _cparams(sem):
    return pltpu.CompilerParams(dimension_semantics=sem, vmem_limit_bytes=VMEM_LIMIT)


def _const_spec(shape):
    n = len(shape)
    return pl.BlockSpec(shape, lambda *_: (0,) * n)


def _layer_norm(y, g, b):
    mu = jnp.mean(y, axis=-1, keepdims=True)
    d = y - mu
    var = jnp.mean(d * d, axis=-1, keepdims=True)
    return d * lax.rsqrt(var + LN_EPS) * g + b


def _rope_consts():
    lane = np.arange(LANES)
    in_head = lane % HEAD_DIM
    half = ROPE_DIM // 2
    inv_freq = 1.0 / (ROPE_THETA ** (jnp.arange(half, dtype=f32) / half))
    c = jnp.zeros((8, LANES), f32)
    c = c.at[0].set(jnp.tile(inv_freq, LANES // half))
    c = c.at[1].set(jnp.asarray(in_head < ROPE_DIM, f32))
    c = c.at[2].set(jnp.asarray(np.where(in_head < half, -1.0, np.where(in_head < ROPE_DIM, 1.0, 0.0)), f32))
    c = c.at[3].set(jnp.asarray(in_head < half, f32))
    return c


def _rope_tables(pos_col, c_ref):
    ang = pos_col.astype(f32) * c_ref[0:1, :]
    cos_t = jnp.where(c_ref[1:2, :] > 0.0, jnp.cos(ang), 1.0)
    sin_t = jnp.sin(ang) * c_ref[2:3, :]
    return cos_t, sin_t, c_ref[3:4, :] > 0.0


def _rope(y, tables):
    cos_t, sin_t, first = tables
    half = ROPE_DIM // 2
    partner = jnp.where(first, pltpu.roll(y, LANES - half, 1), pltpu.roll(y, half, 1))
    return y * cos_t + partner * sin_t


def _proj_kernel(x_ref, pos_ref, w_ref, c_ref,
                 qd_ref, kd_ref, vd_ref, qn_ref, kc_ref, vc_ref, ks_ref, vs_ref, kw_ref, vw_ref, g_ref,
                 *, seq):
    tm = x_ref.shape[0]
    xb = x_ref[...].astype(bf16)
    tables = _rope_tables(pos_ref[...], c_ref)
    lane = lax.broadcasted_iota(jnp.int32, (tm, LANES), 1)
    low = lane < HEAD_DIM
    scale = QK_SCALE

    hd2 = DIFF_HEADS * 2 * HEAD_DIM

    def diff_q(y):
        for c in range(hd2 // LANES):
            sl = slice(c * LANES, (c + 1) * LANES)
            qd_ref[:, sl] = (_rope(y[:, sl], tables) * scale).astype(bf16)

    def diff_k(y):
        for c in range(hd2 // LANES):
            sl = slice(c * LANES, (c + 1) * LANES)
            kd_ref[:, sl] = _rope(y[:, sl], tables).astype(bf16)

    def diff_v(y):
        ones = jnp.ones((tm, LANES), bf16)
        for c in range(hd2 // LANES):
            vd_ref[:, (2 * c) * LANES:(2 * c + 1) * LANES] = y[:, c * LANES:(c + 1) * LANES].astype(bf16)
            vd_ref[:, (2 * c + 1) * LANES:(2 * c + 2) * LANES] = ones

    def nsa_q(y):
        for c in range(NSA_HEADS // 2):
            slab = _rope(y[:, c * LANES:(c + 1) * LANES], tables) * scale
            qn_ref[:, (2 * c) * LANES:(2 * c + 1) * LANES] = jnp.where(low, slab, 0.0).astype(bf16)
            qn_ref[:, (2 * c + 1) * LANES:(2 * c + 2) * LANES] = jnp.where(
                low, pltpu.roll(slab, HEAD_DIM, 1), 0.0).astype(bf16)

    def cmp_kv(y):
        kc_ref[...] = y[:, :LANES]
        vc_ref[...] = y[:, LANES:]

    def grouped(ref, y, fill):
        ref[0] = jnp.where(low, y, fill).astype(ref.dtype)
        ref[1] = jnp.where(low, pltpu.roll(y, HEAD_DIM, 1), fill).astype(ref.dtype)

    def sel_kv(y):
        row = lax.broadcasted_iota(jnp.int32, (tm, LANES), 0) + lax.rem(pl.program_id(0) * tm, seq)
        onehot = jnp.where(lane - HEAD_DIM == row // SLC_BLOCK, 1.0, 0.0)
        grouped(ks_ref, _rope(y[:, :LANES], tables), onehot)
        grouped(vs_ref, y[:, LANES:], 1.0)

    def win_kv(y):
        grouped(kw_ref, _rope(y[:, :LANES], tables), 0.0)
        grouped(vw_ref, y[:, LANES:], 1.0)

    def gate(y):
        sig = jax.nn.sigmoid(y)
        g_ref[0] = sig
        g_ref[1] = pltpu.roll(sig, LANES - NSA_GQA * 3, 1)

    widths = [hd2, hd2, hd2, NSA_HEADS * HEAD_DIM, 2 * LANES, 2 * LANES, 2 * LANES, LANES]
    epilogues = [diff_q, diff_k, diff_v, nsa_q, cmp_kv, sel_kv, win_kv, gate]
    cols = [sum(widths[:n]) for n in range(len(widths))]

    def seg(n):
        return jnp.dot(xb, w_ref[:, cols[n]:cols[n] + widths[n]], preferred_element_type=f32)

    y_next = seg(0)
    for n, epilogue in enumerate(epilogues):
        y = y_next
        if n + 1 < len(epilogues):
            y_next = seg(n + 1)
        epilogue(y)


def _proj(x2, pos_col, w_pad, consts, seq):
    n, d_model = x2.shape
    tm = PROJ_ROWS
    hd2 = DIFF_HEADS * 2 * HEAD_DIM
    row_spec = lambda w: pl.BlockSpec((tm, w), lambda i: (i, 0))
    grp_spec = pl.BlockSpec((NSA_KV_GROUPS, tm, LANES), lambda i: (0, i, 0))
    out_shape = (
        jax.ShapeDtypeStruct((n, hd2), bf16), jax.ShapeDtypeStruct((n, hd2), bf16),
        jax.ShapeDtypeStruct((n, 2 * hd2), bf16), jax.ShapeDtypeStruct((n, NSA_HEADS * LANES), bf16),
        jax.ShapeDtypeStruct((n, LANES), f32), jax.ShapeDtypeStruct((n, LANES), f32),
        jax.ShapeDtypeStruct((NSA_KV_GROUPS, n, LANES), bf16), jax.ShapeDtypeStruct((NSA_KV_GROUPS, n, LANES), bf16),
        jax.ShapeDtypeStruct((NSA_KV_GROUPS, n, LANES), bf16), jax.ShapeDtypeStruct((NSA_KV_GROUPS, n, LANES), bf16),
        jax.ShapeDtypeStruct((NSA_KV_GROUPS, n, LANES), f32),
    )
    out_specs = (row_spec(hd2), row_spec(hd2), row_spec(2 * hd2), row_spec(NSA_HEADS * LANES),
                 row_spec(LANES), row_spec(LANES), grp_spec, grp_spec, grp_spec, grp_spec, grp_spec)
    return pl.pallas_call(
        functools.partial(_proj_kernel, seq=seq),
        grid=(n // tm,),
        in_specs=[row_spec(d_model), pl.BlockSpec((tm, 1), lambda i: (i, 0)),
                  _const_spec(w_pad.shape), _const_spec(consts.shape)],
        out_specs=out_specs, out_shape=out_shape,
        compiler_params=_cparams(("parallel",)), name="proj",
    )(x2, pos_col, w_pad, consts)


def _compress_one(c_ref, pea_ref, peb_ref, w1a_ref, w1b_ref, b1_ref, w2_ref):
    n = c_ref.shape[1] // CMP_STRIDE
    ck = jnp.concatenate([c_ref[0, pl.ds(t, n, stride=CMP_STRIDE), :] for t in range(CMP_STRIDE)], axis=1)
    hid_a = jnp.dot((ck + pea_ref[...]).astype(bf16), w1a_ref[...], preferred_element_type=f32)
    hid_b = jnp.dot((ck + peb_ref[...]).astype(bf16), w1b_ref[...], preferred_element_type=f32)
    hid = hid_a + pltpu.roll(hid_b, n - 1, 0) + b1_ref[...]
    hid = jax.nn.gelu(hid)
    return jnp.dot(hid.astype(bf16), w2_ref[...], preferred_element_type=f32)


def _compress_kernel(kc_ref, vc_ref, pos_ref, c_ref,
                     pak_ref, pbk_ref, w1ak_ref, w1bk_ref, b1k_ref, w2k_ref,
                     pav_ref, pbv_ref, w1av_ref, w1bv_ref, b1v_ref, w2v_ref,
                     kcmp_ref, vcmp_ref):
    k = _compress_one(kc_ref, pak_ref, pbk_ref, w1ak_ref, w1bk_ref, b1k_ref, w2k_ref)
    k = _rope(k, _rope_tables(pos_ref[0], c_ref))
    low = lax.broadcasted_iota(jnp.int32, k.shape, 1) < HEAD_DIM
    kcmp_ref[0, 0] = jnp.where(low, k, 0.0).astype(bf16)
    kcmp_ref[0, 1] = jnp.where(low, pltpu.roll(k, HEAD_DIM, 1), 0.0).astype(bf16)
    v = _compress_one(vc_ref, pav_ref, pbv_ref, w1av_ref, w1bv_ref, b1v_ref, w2v_ref)
    vcmp_ref[0, 0] = jnp.where(low, v, 0.0).astype(bf16)
    vcmp_ref[0, 1] = jnp.where(low, pltpu.roll(v, HEAD_DIM, 1), 0.0).astype(bf16)


def _compress_weights(pe, w1, b1, w2):
    r = CMP_BLOCK // CMP_STRIDE
    assert r == 2
    eye = jnp.eye(NSA_KV_GROUPS, dtype=f32)
    w1r = w1.reshape(CMP_BLOCK, HEAD_DIM, CMP_HIDDEN)
    per = CMP_STRIDE * NSA_KV_GROUPS * HEAD_DIM

    def big(part):
        return jnp.einsum('tcm,gh->tgchm', part, eye).reshape(per, NSA_KV_GROUPS * CMP_HIDDEN).astype(bf16)

    def pe_row(part):
        return jnp.broadcast_to(part[:, None, :], (CMP_STRIDE, NSA_KV_GROUPS, HEAD_DIM)).reshape(1, per)

    w2b = jnp.einsum('mc,gh->gmhc', w2, eye).reshape(NSA_KV_GROUPS * CMP_HIDDEN, NSA_KV_GROUPS * HEAD_DIM)
    return (pe_row(pe[:CMP_STRIDE]), pe_row(pe[CMP_STRIDE:]), big(w1r[:CMP_STRIDE]), big(w1r[CMP_STRIDE:]),
            jnp.tile(b1, NSA_KV_GROUPS).reshape(1, -1), w2b.astype(bf16))


def _compress(kc, vc, cmp_pos, consts, wk, wv):
    b, s, width = kc.shape
    nch = s // CMP_STRIDE
    seq_spec = pl.BlockSpec((1, s, width), lambda i: (i, 0, 0))
    w_specs = [_const_spec(w.shape) for w in wk + wv]
    return pl.pallas_call(
        _compress_kernel,
        grid=(b,),
        in_specs=[seq_spec, seq_spec, pl.BlockSpec((1, nch, 1), lambda i: (i, 0, 0)),
                  _const_spec(consts.shape)] + w_specs,
        out_specs=(pl.BlockSpec((1, NSA_KV_GROUPS, nch, LANES), lambda i: (i, 0, 0, 0)),
                   pl.BlockSpec((1, NSA_KV_GROUPS, nch, LANES), lambda i: (i, 0, 0, 0))),
        out_shape=(jax.ShapeDtypeStruct((b, NSA_KV_GROUPS, nch, LANES), bf16),
                   jax.ShapeDtypeStruct((b, NSA_KV_GROUPS, nch, LANES), bf16)),
        compiler_params=_cparams(("parallel",)), name="compress",
    )(kc, vc, cmp_pos, consts, *wk, *wv)


def _softmax_init(m_ref, acc_ref):
    m_ref[...] = jnp.full(m_ref.shape, NEG_INF, f32)
    acc_ref[...] = jnp.zeros(acc_ref.shape, f32)


def _softmax_step(s, v_ones, m_ref, acc_ref, rs):
    m_prev = m_ref[rs, :]
    m_new = jnp.maximum(m_prev, jnp.max(s, axis=1, keepdims=True))
    alpha = jnp.exp2(m_prev - m_new)
    p = jnp.exp2((s - jnp.concatenate([m_new] * (s.shape[1] // LANES), axis=1)).astype(bf16))
    pv = jnp.dot(p, v_ones, preferred_element_type=f32)
    acc_ref[rs, :] = jnp.concatenate([alpha] * (acc_ref.shape[1] // LANES), axis=1) * acc_ref[rs, :] + pv
    m_ref[rs, :] = m_new


def _scores(q, k):
    return lax.dot_general(q, k, (((1,), (1,)), ((), ())), preferred_element_type=f32)


def _run_steps(steps, m_ref, acc_ref):
    s_next = steps[0][0]()
    for n, (_, v_ones_fn, rows) in enumerate(steps):
        s = s_next
        if n + 1 < len(steps):
            s_next = steps[n + 1][0]()
        _softmax_step(s, v_ones_fn(), m_ref, acc_ref, rows)


def _diff_kernel(q_ref, k_ref, v_ref, lam_ref, g_ref, o_ref, q2_ref, m_ref, acc_ref, *, lam_init):
    tq = q_ref.shape[1]
    tk = min(DIFF_K_TILE, tq)
    per_q = tq // tk
    rc = min(SOFTMAX_ROWS, tq)
    vdim = o_ref.shape[2]
    i = pl.program_id(2)
    q = q_ref[0]
    low = lax.broadcasted_iota(jnp.int32, q.shape, 1) < HEAD_DIM
    zero = jnp.zeros_like(q)
    q2_ref[0:tq, :] = jnp.where(low, q, zero)
    q2_ref[tq:2 * tq, :] = jnp.where(low, zero, q)
    _softmax_init(m_ref, acc_ref)

    def tile_steps(j, k_off=None):
        off = pl.multiple_of(j * tk, tk)
        steps = []

        def add(row0, rows):
            q_off = row0 % tq
            cols = tk if k_off is None else min(tk, q_off + rows - k_off)
            if cols <= 0:
                return
            masked = k_off is not None and q_off - k_off + 1 < cols

            def score():
                s = _scores(q2_ref[row0:row0 + rows, :], k_ref[0, pl.ds(off, cols), :])
                if masked:
                    r = q_off + lax.broadcasted_iota(jnp.int32, (rows, cols), 0)
                    s = jnp.where(r >= k_off + lax.broadcasted_iota(jnp.int32, (rows, cols), 1), s, NEG_INF)
                return s

            steps.append((score, lambda: v_ref[0, pl.ds(off, cols), :], slice(row0, row0 + rows)))

        big = min(2 * rc, tq)
        for row0 in range(0, 2 * tq, big):
            if k_off is None or row0 % tq - k_off + 1 >= tk:
                add(row0, big)
            else:
                for sub in range(row0, row0 + big, rc):
                    add(sub, rc)
        return steps

    group = 2 if per_q % 2 == 0 else 1

    def body(j, carry):
        _run_steps(sum((tile_steps(group * j + t) for t in range(group)), []), m_ref, acc_ref)
        return carry

    lax.fori_loop(0, i * (per_q // group), body, 0)
    _run_steps(sum((tile_steps(i * per_q + t, t * tk) for t in range(per_q)), []), m_ref, acc_ref)

    lam_v = lam_ref[...]
    lam = (jnp.exp(jnp.sum(lam_v[0:1] * lam_v[1:2], axis=1, keepdims=True))
           - jnp.exp(jnp.sum(lam_v[2:3] * lam_v[3:4], axis=1, keepdims=True)) + lam_init)
    o = acc_ref[:, :vdim] / acc_ref[:, vdim:]
    od = o[:tq] - lam * o[tq:]
    od = od * lax.rsqrt(jnp.mean(od * od, axis=-1, keepdims=True) + RMS_EPS)
    o_ref[0] = (od * g_ref[...] * (1.0 - lam_init)).astype(o_ref.dtype)


def _diff_attention(qd, kd, vd, lam_vec, diff_g, lam_init):
    b, s, _ = qd.shape
    tq = min(ATTN_TILE, s)
    vdim = vd.shape[2] // DIFF_HEADS // 2
    assert vdim == LANES
    return pl.pallas_call(
        functools.partial(_diff_kernel, lam_init=lam_init),
        grid=(b, DIFF_HEADS, s // tq),
        in_specs=[pl.BlockSpec((1, tq, LANES), lambda bi, h, i: (bi, i, h)),
                  pl.BlockSpec((1, s, LANES), lambda bi, h, i: (bi, 0, h)),
                  pl.BlockSpec((1, s, 2 * vdim), lambda bi, h, i: (bi, 0, h)),
                  _const_spec(lam_vec.shape), _const_spec(diff_g.shape)],
        out_specs=pl.BlockSpec((1, tq, vdim), lambda bi, h, i: (bi, i, h)),
        out_shape=jax.ShapeDtypeStruct((b, s, DIFF_HEADS * vdim), bf16),
        scratch_shapes=[pltpu.VMEM((2 * tq, LANES), bf16), pltpu.VMEM((2 * tq, LANES), f32),
                        pltpu.VMEM((2 * tq, 2 * vdim), f32)],
        compiler_params=_cparams(("parallel", "parallel", "arbitrary")), name="diff_attn",
    )(qd, kd, vd, lam_vec, diff_g)


def _nsa_kernel(q_ref, kc_ref, vc_ref, ks_ref, vs_ref, kw_ref, vw_ref, g_ref, ov_ref, o_ref,
                qa_ref, m_ref, acc_ref, mw_ref, accw_ref, oc_ref, imp_ref, rank_ref, gate_ref, *, top_k):
    tq = q_ref.shape[1]
    seq = ks_ref.shape[2]
    tk = min(NSA_K_TILE, seq)
    ncp = kc_ref.shape[2]
    nslc = ov_ref.shape[0]
    rep = NSA_GQA
    i = pl.program_id(2)
    s0 = i * tq
    heads = [slice(r * tq, (r + 1) * tq) for r in range(rep)]

    def tile_steps(q_of, k_ref, v_ref, back, width, lo=None, hi=None, pair_q=None):
        off = pl.multiple_of(s0 - back, tq)
        steps = []
        if pair_q is not None and lo is None and hi is None:
            for r in range(0, rep, 2):
                steps.append((lambda r=r: _scores(pair_q(r), k_ref[0, 0, pl.ds(off, width), :]),
                              lambda: v_ref[0, 0, pl.ds(off, width), :], slice(r * tq, (r + 2) * tq)))
            return steps
        for r in range(rep):

            def score(r=r):
                s = _scores(q_of(r), k_ref[0, 0, pl.ds(off, width), :])
                if lo is None and hi is None:
                    return s
                d = (lax.broadcasted_iota(jnp.int32, (tq, width), 1)
                     - lax.broadcasted_iota(jnp.int32, (tq, width), 0))
                if hi is not None:
                    s = jnp.where(d <= hi, s, NEG_INF)
                if lo is not None:
                    s = jnp.where(d > lo, s, NEG_INF)
                return s

            steps.append((score, lambda: v_ref[0, 0, pl.ds(off, width), :], heads[r]))
        return steps

    def normalized(ref):
        acc = ref[...]
        low = lax.broadcasted_iota(jnp.int32, acc.shape, 1) < HEAD_DIM
        return acc / jnp.where(low, pltpu.roll(acc, HEAD_DIM, 1), 1.0)

    def compressed_branch(width, window_steps):
        t_col = s0 + lax.broadcasted_iota(jnp.int32, (tq, width), 0)
        cmp_end = lax.broadcasted_iota(jnp.int32, (tq, width), 1) * CMP_STRIDE + (CMP_BLOCK - 1)
        cmp_ok = cmp_end <= t_col
        kc = kc_ref[0, 0, 0:width, :]
        vc = vc_ref[0, 0, 0:width, :]
        raw = [_scores(q_ref[0, :, r * LANES:(r + 1) * LANES], kc) for r in range(rep)]
        _softmax_init(mw_ref, accw_ref)
        _run_steps(window_steps, mw_ref, accw_ref)
        psum = jnp.zeros((tq, width), f32)
        for r in range(rep):
            s = jnp.where(cmp_ok, raw[r], NEG_INF)
            p = jnp.where(cmp_ok, jnp.exp2(s - jnp.max(s, axis=1, keepdims=True)), 0.0)
            den = jnp.sum(p, axis=1, keepdims=True)
            p = p / jnp.where(den > 0.0, den, 1.0)
            psum = psum + p
            oc_ref[heads[r], :] = jnp.dot(p.astype(bf16), vc, preferred_element_type=f32)
        p_hi = psum.astype(bf16)
        p_lo = (psum - p_hi.astype(f32)).astype(bf16)
        ov = ov_ref[:, 0:width]
        imp_ref[...] = _scores(ov, p_hi) + _scores(ov, p_lo)
        gates = g_ref[0, 0]
        for c in range(3 * rep):
            gate_ref[c] = jnp.broadcast_to(gates[:, c:c + 1], (tq, LANES))

    win = functools.partial(tile_steps, lambda r: q_ref[0, :, r * LANES:(r + 1) * LANES], kw_ref, vw_ref)
    n_back = WINDOW // tq
    windows = [lambda n=n: win(n * tq, (n + 1) * tq, hi=n * tq) for n in range(n_back)]
    windows.append(lambda: win(WINDOW, WINDOW + tq, lo=0, hi=WINDOW))
    half = ncp // 2
    has_half = half % LANES == 0
    last_half_tile = half // (tq // CMP_STRIDE) - 1 if has_half else -1
    assert not has_half or last_half_tile >= n_back
    for n, window in enumerate(windows):
        this_window = (i == n) if n < n_back else (i >= n_back)
        if not has_half:
            pl.when(this_window)(functools.partial(lambda w: compressed_branch(ncp, w()), window))
        elif n < n_back:
            pl.when(this_window)(functools.partial(lambda w: compressed_branch(half, w()), window))
        else:
            pl.when(this_window & (i <= last_half_tile))(
                functools.partial(lambda w: compressed_branch(half, w()), window))
            pl.when(i > last_half_tile)(functools.partial(lambda w: compressed_branch(ncp, w()), window))

    blk = lax.broadcasted_iota(jnp.int32, (nslc, tq), 0)
    cur = (s0 + lax.broadcasted_iota(jnp.int32, (nslc, tq), 1)) // SLC_BLOCK
    imp = jnp.where(blk > cur, -1.0, imp_ref[...])
    imp = jnp.where((blk == 0) | (blk == cur) | (blk == cur - 1), FORCED_SCORE, imp)
    rank_ref[...] = jnp.zeros((nslc, tq), f32)
    per_tile = tq // SLC_BLOCK
    for first in range(0, nslc, per_tile):

        @pl.when(first <= i * per_tile)
        def _():
            rank = rank_ref[...]
            for jp in range(first, min(first + per_tile, nslc)):
                other = imp[jp:jp + 1, :]
                ahead = jnp.where(blk > jp, jnp.where(other >= imp, 1.0, 0.0), jnp.where(other > imp, 1.0, 0.0))
                rank = rank + ahead
            rank_ref[...] = rank

    bias_t = jnp.where(rank_ref[...] < top_k, 0.0, SEL_BIAS)
    pieces = [jnp.zeros((HEAD_DIM, tq), f32), bias_t]
    if nslc < LANES - HEAD_DIM:
        pieces.append(jnp.zeros((LANES - HEAD_DIM - nslc, tq), f32))
    bias = jnp.concatenate(pieces, axis=0).T.astype(bf16)
    for r in range(rep):
        qa_ref[heads[r], :] = q_ref[0, :, r * LANES:(r + 1) * LANES] + bias

    _softmax_init(m_ref, acc_ref)
    sel = functools.partial(tile_steps, lambda r: qa_ref[heads[r], :], ks_ref, vs_ref,
                            pair_q=lambda r: qa_ref[r * tq:(r + 2) * tq, :])
    trip = NSA_TRIP_TILES * tk

    def sel_body(j, carry):
        back = s0 - j * trip
        _run_steps(sum((sel(back - t * 2 * tk, 2 * tk) for t in range(NSA_TRIP_TILES // 2)), []), m_ref, acc_ref)
        return carry

    lax.fori_loop(0, s0 // trip, sel_body, 0)
    tail = lax.rem(i, trip // tq)
    for n in range(trip // tq):
        whole = n // 2
        steps = sum((sel(n * tq - t * 2 * tk, 2 * tk) for t in range(whole // 2)), [])
        if whole % 2:
            steps += sel(n * tq - (whole - 1) * tk, tk)
        steps += sel(tq, tk, hi=tq) if n % 2 else sel(0, tq, hi=0)
        pl.when(tail == n)(functools.partial(_run_steps, steps, m_ref, acc_ref))

    o_sel = normalized(acc_ref)
    o_win = normalized(accw_ref)

    lane = lax.broadcasted_iota(jnp.int32, (tq, LANES), 1)
    gated = []
    for r in range(rep):
        gated.append(gate_ref[3 * r] * oc_ref[heads[r], :] + gate_ref[3 * r + 1] * o_sel[heads[r]]
                     + gate_ref[3 * r + 2] * o_win[heads[r]])
    for c in range(rep // 2):
        o_ref[0, :, c * LANES:(c + 1) * LANES] = jnp.where(
            lane < HEAD_DIM, gated[2 * c], pltpu.roll(gated[2 * c + 1], HEAD_DIM, 1)).astype(o_ref.dtype)


def _nsa_attention(qn, kcmp, vcmp, ksa, vsa, kwa, vwa, gates, overlap_t, top_k):
    b, s, _ = qn.shape
    tq = min(NSA_Q_TILE, s)
    ncp = kcmp.shape[2]
    rep = NSA_GQA
    assert WINDOW == 2 * tq and min(NSA_K_TILE, s) == 2 * tq
    cmp_spec = pl.BlockSpec((1, 1, ncp, LANES), lambda bi, g, i: (bi, g, 0, 0))
    seq_spec = pl.BlockSpec((1, 1, s, LANES), lambda bi, g, i: (g, bi, 0, 0))
    rows = rep * tq
    return pl.pallas_call(
        functools.partial(_nsa_kernel, top_k=top_k),
        grid=(b, NSA_KV_GROUPS, s // tq),
        in_specs=[pl.BlockSpec((1, tq, rep * LANES), lambda bi, g, i: (bi, i, g)),
                  cmp_spec, cmp_spec, seq_spec, seq_spec, seq_spec, seq_spec,
                  pl.BlockSpec((1, 1, tq, LANES), lambda bi, g, i: (g, bi, i, 0)),
                  _const_spec(overlap_t.shape)],
        out_specs=pl.BlockSpec((1, tq, rep * HEAD_DIM), lambda bi, g, i: (bi, i, g)),
        out_shape=jax.ShapeDtypeStruct((b, s, NSA_HEADS * HEAD_DIM), bf16),
        scratch_shapes=[pltpu.VMEM((rows, LANES), bf16)] + [pltpu.VMEM((rows, LANES), f32)] * 5 + [
                        pltpu.VMEM(overlap_t.shape[:1] + (tq,), f32),
                        pltpu.VMEM(overlap_t.shape[:1] + (tq,), f32), pltpu.VMEM((3 * rep, tq, LANES), f32)],
        compiler_params=_cparams(("parallel", "parallel", "arbitrary")), name="nsa_attn",
    )(qn, kcmp, vcmp, ksa, vsa, kwa, vwa, gates, overlap_t)


def _outproj_kernel(x_ref, od_ref, on_ref, wa_ref, wb_ref, g_ref, b_ref, o_ref, *, alpha):
    parts = 4
    rows = x_ref.shape[0] // parts

    def mix(r):
        rs = slice(r * rows, (r + 1) * rows)
        return (jnp.dot(od_ref[rs, :], wa_ref[...], preferred_element_type=f32)
                + jnp.dot(on_ref[rs, :], wb_ref[...], preferred_element_type=f32))

    y_next = mix(0)
    for r in range(parts):
        rs = slice(r * rows, (r + 1) * rows)
        y = y_next
        if r + 1 < parts:
            y_next = mix(r + 1)
        o_ref[rs, :] = _layer_norm(alpha * x_ref[rs, :] + y, g_ref[...], b_ref[...])


def _outproj(x2, od, on, wa, wb, g, b, alpha):
    n, d_model = x2.shape
    tm = 2 * DENSE_ROWS
    row_spec = lambda w: pl.BlockSpec((tm, w), lambda i: (i, 0))
    return pl.pallas_call(
        functools.partial(_outproj_kernel, alpha=alpha),
        grid=(n // tm,),
        in_specs=[row_spec(d_model), row_spec(od.shape[1]), row_spec(on.shape[1]),
                  _const_spec(wa.shape), _const_spec(wb.shape), _const_spec(g.shape), _const_spec(b.shape)],
        out_specs=row_spec(d_model),
        out_shape=jax.ShapeDtypeStruct((n, d_model), f32),
        compiler_params=_cparams(("parallel",)), name="outproj_ln",
    )(x2, od, on, wa, wb, g, b)


def _ffn_kernel(h_ref, halo_ref, wu_ref, cw_ref, cb_ref, wd_ref, g_ref, b_ref, o_ref, hb_ref, u_ref, act_ref,
                *, alpha, tiles_per_seq):
    tm = h_ref.shape[0]
    tf = FFN_CHUNK
    n_chunks = wd_ref.shape[0] // tf
    h = h_ref[...]
    first = lax.rem(pl.program_id(0), tiles_per_seq) == 0
    halo = jnp.where(first, 0.0, halo_ref[...])
    d_ff = wd_ref.shape[0]
    hb_ref[...] = jnp.concatenate([halo, h], axis=0).astype(bf16)

    def up(c):
        for part, col in enumerate((c * tf, d_ff + c * tf)):
            u_ref[c % 2, :, part * tf:(part + 1) * tf] = jnp.dot(
                hb_ref[...], wu_ref[:, col:col + tf], preferred_element_type=f32)

    def conv(c, part):
        col = part * d_ff + c * tf
        u_all = u_ref[c % 2, :, part * tf:(part + 1) * tf]
        u = cb_ref[:, col:col + tf]
        for k in range(CONV_WIDTH):
            lag = CONV_WIDTH - 1 - k
            shifted = u_all if lag == 0 else pltpu.roll(u_all, lag, 0)
            u = u + cw_ref[k:k + 1, col:col + tf] * shifted[HALO:, :]
        return u

    up(0)
    for c in range(n_chunks):
        if c + 1 < n_chunks:
            up(c + 1)
        act_ref[:, c * tf:(c + 1) * tf] = (jax.nn.silu(conv(c, 0)) * conv(c, 1)).astype(bf16)
    parts = 2
    rows = tm // parts
    down = lambda r: jnp.dot(act_ref[r * rows:(r + 1) * rows, :], wd_ref[...], preferred_element_type=f32)
    y_next = down(0)
    for r in range(parts):
        rs = slice(r * rows, (r + 1) * rows)
        y = y_next
        if r + 1 < parts:
            y_next = down(r + 1)
        o_ref[rs, :] = _layer_norm(alpha * h_ref[rs, :] + y, g_ref[...], b_ref[...])


def _ffn(h1, wu, cw, cb, wd, g, b, alpha, seq):
    n, d_model = h1.shape
    tm = min(DENSE_ROWS, seq)
    single = dict(pipeline_mode=pl.Buffered(1))
    return pl.pallas_call(
        functools.partial(_ffn_kernel, alpha=alpha, tiles_per_seq=seq // tm),
        grid=(n // tm,),
        in_specs=[pl.BlockSpec((tm, d_model), lambda i: (i, 0)),
                  pl.BlockSpec((HALO, d_model), lambda i: (jnp.maximum(i * (tm // HALO) - 1, 0), 0)),
                  pl.BlockSpec(wu.shape, lambda i: (0, 0), **single),
                  _const_spec(cw.shape), _const_spec(cb.shape),
                  pl.BlockSpec(wd.shape, lambda i: (0, 0), **single),
                  _const_spec(g.shape), _const_spec(b.shape)],
        out_specs=pl.BlockSpec((tm, d_model), lambda i: (i, 0)),
        out_shape=jax.ShapeDtypeStruct((n, d_model), f32),
        scratch_shapes=[pltpu.VMEM((HALO + tm, d_model), bf16), pltpu.VMEM((2, HALO + tm, 2 * FFN_CHUNK), f32),
                        pltpu.VMEM((tm, wd.shape[0]), bf16)],
        compiler_params=_cparams(("parallel",)), name="ffn_ln",
    )(h1, h1, wu, cw, cb, wd, g, b)


def kernel(x, positions, w_in, lambda_q1, lambda_k1, lambda_q2, lambda_k2, diff_norm_g, cmp_pe_k, cmp_w1_k, cmp_b1_k, cmp_w2_k, cmp_pe_v, cmp_w1_v, cmp_b1_v, cmp_w2_v, w_out, ln1_g, ln1_b, w_up, conv_w, conv_b, w_down, ln2_g, ln2_b):
    b, s, d_model = x.shape
    depth = w_in.shape[0]
    n = b * s
    d_ff = w_down.shape[1]
    assert s % ATTN_TILE == 0 or s < ATTN_TILE
    assert s % SLC_BLOCK == 0 and s // SLC_BLOCK <= LANES - HEAD_DIM and d_ff % FFN_CHUNK == 0
    alpha = (2 * depth) ** 0.25
    consts = _rope_consts()
    pos_col = positions.reshape(n, 1)

    n_chunk = s // CMP_STRIDE
    n_cmp = n_chunk - CMP_BLOCK // CMP_STRIDE + 1
    n_slc = s // SLC_BLOCK
    top_k = min(SLC_TOPK, n_slc)
    cmp_pos = positions[:, CMP_BLOCK - 1::CMP_STRIDE][:, :n_cmp]
    cmp_pos = jnp.pad(cmp_pos, ((0, 0), (0, n_chunk - n_cmp))).reshape(b, n_chunk, 1)
    cs = np.arange(n_chunk)[None, :] * CMP_STRIDE
    ss = np.arange(n_slc)[:, None] * SLC_BLOCK
    ov = np.clip(np.minimum(cs + CMP_BLOCK, ss + SLC_BLOCK) - np.maximum(cs, ss), 0, None) / CMP_BLOCK
    ov[:, n_cmp:] = 0.0
    overlap_t = jnp.asarray(ov, dtype=bf16)

    h = x.reshape(n, d_model)
    for l in range(depth):
        lam_init = 0.8 - 0.6 * math.exp(-0.3 * l)
        d_in = w_in.shape[2]
        w_pad = jnp.pad(w_in[l], ((0, 0), (0, -d_in % LANES))).astype(bf16)
        qd, kd, vd, qn, kc, vc, ksa, vsa, kwa, vwa, gates = _proj(h, pos_col, w_pad, consts, s)

        kcmp, vcmp = _compress(
            kc.reshape(b, s, LANES), vc.reshape(b, s, LANES), cmp_pos, consts,
            _compress_weights(cmp_pe_k[l], cmp_w1_k[l], cmp_b1_k[l], cmp_w2_k[l]),
            _compress_weights(cmp_pe_v[l], cmp_w1_v[l], cmp_b1_v[l], cmp_w2_v[l]))

        lam_vec = jnp.stack([lambda_q1[l], lambda_k1[l], lambda_q2[l], lambda_k2[l]]).astype(f32)
        od = _diff_attention(qd.reshape(b, s, -1), kd.reshape(b, s, -1), vd.reshape(b, s, -1),
                             lam_vec, diff_norm_g[l].reshape(1, -1).astype(f32), lam_init)
        grouped = lambda a: a.reshape(NSA_KV_GROUPS, b, s, LANES)
        on = _nsa_attention(qn.reshape(b, s, -1), kcmp, vcmp, grouped(ksa), grouped(vsa), grouped(kwa),
                            grouped(vwa), grouped(gates), overlap_t, top_k)

        d_diff = od.shape[2]
        wo = w_out[l].astype(bf16)
        h = _outproj(h, od.reshape(n, -1), on.reshape(n, -1), wo[:d_diff], wo[d_diff:],
                     ln1_g[l].reshape(1, -1), ln1_b[l].reshape(1, -1), alpha)
        h = _ffn(h, w_up[l].astype(bf16), conv_w[l], conv_b[l].reshape(1, -1),
                 w_down[l].astype(bf16), ln2_g[l].reshape(1, -1), ln2_b[l].reshape(1, -1), alpha, s)
    return h.reshape(b, s, d_model)
```

```python
import functools
import math

import jax
import jax.numpy as jnp
import numpy as np
from jax import lax
from jax.experimental import pallas as pl
from jax.experimental.pallas import tpu as pltpu

f32 = jnp.float32
bf16 = jnp.bfloat16

LANES = 128
HEAD_DIM = 64
ROPE_DIM = HEAD_DIM // 4
ROPE_THETA = 500000.0
DIFF_HEADS = 4
NSA_HEADS = 8
NSA_KV_GROUPS = 2
NSA_GQA = NSA_HEADS // NSA_KV_GROUPS
CMP_BLOCK = 32
CMP_STRIDE = 16
CMP_HIDDEN = 2 * HEAD_DIM
SLC_BLOCK = 64
SLC_TOPK = 16
WINDOW = 512
CONV_WIDTH = 3
LN_EPS = 1e-5
RMS_EPS = 1e-5
NEG_INF = -1e30
SEL_BIAS = -1e9
FORCED_SCORE = 1e6
VMEM_LIMIT = 56 * 1024 * 1024

LOG2E = 1.4426950408889634
QK_SCALE = HEAD_DIM ** -0.5 * LOG2E

PROJ_ROWS = 512
ATTN_TILE = 4096
DIFF_K_TILE = 512
NSA_Q_TILE = 256
NSA_K_TILE = 512
NSA_TRIP_TILES = 4
SOFTMAX_ROWS = 256
DENSE_ROWS = 512
FFN_CHUNK = 256
HALO = 8


def _cparams(sem):
    return pltpu.CompilerParams(dimension_semantics=sem, vmem_limit_bytes=VMEM_LIMIT)


def _const_spec(shape):
    n = len(shape)
    return pl.BlockSpec(shape, lambda *_: (0,) * n)


def _layer_norm(y, g, b):
    mu = jnp.mean(y, axis=-1, keepdims=True)
    d = y - mu
    var = jnp.mean(d * d, axis=-1, keepdims=True)
    return d * lax.rsqrt(var + LN_EPS) * g + b


def _rope_consts():
    lane = np.arange(LANES)
    in_head = lane % HEAD_DIM
    half = ROPE_DIM // 2
    inv_freq = 1.0 / (ROPE_THETA ** (jnp.arange(half, dtype=f32) / half))
    c = jnp.zeros((8, LANES), f32)
    c = c.at[0].set(jnp.tile(inv_freq, LANES // half))
    c = c.at[1].set(jnp.asarray(in_head < ROPE_DIM, f32))
    c = c.at[2].set(jnp.asarray(np.where(in_head < half, -1.0, np.where(in_head < ROPE_DIM, 1.0, 0.0)), f32))
    c = c.at[3].set(jnp.asarray(in_head < half, f32))
    return c


def _rope_tables(pos_col, c_ref):
    ang = pos_col.astype(f32) * c_ref[0:1, :]
    cos_t = jnp.where(c_ref[1:2, :] > 0.0, jnp.cos(ang), 1.0)
    sin_t = jnp.sin(ang) * c_ref[2:3, :]
    return cos_t, sin_t, c_ref[3:4, :] > 0.0


def _rope(y, tables):
    cos_t, sin_t, first = tables
    half = ROPE_DIM // 2
    partner = jnp.where(first, pltpu.roll(y, LANES - half, 1), pltpu.roll(y, half, 1))
    return y * cos_t + partner * sin_t


def _proj_kernel(x_ref, pos_ref, w_ref, c_ref,
                 qd_ref, kd_ref, vd_ref, qn_ref, kc_ref, vc_ref, ks_ref, vs_ref, kw_ref, vw_ref, g_ref,
                 *, seq):
    tm = x_ref.shape[0]
    xb = x_ref[...].astype(bf16)
    tables = _rope_tables(pos_ref[...], c_ref)
    lane = lax.broadcasted_iota(jnp.int32, (tm, LANES), 1)
    low = lane < HEAD_DIM
    scale = QK_SCALE

    hd2 = DIFF_HEADS * 2 * HEAD_DIM

    def diff_q(y):
        for c in range(hd2 // LANES):
            sl = slice(c * LANES, (c + 1) * LANES)
            qd_ref[:, sl] = (_rope(y[:, sl], tables) * scale).astype(bf16)

    def diff_k(y):
        for c in range(hd2 // LANES):
            sl = slice(c * LANES, (c + 1) * LANES)
            kd_ref[:, sl] = _rope(y[:, sl], tables).astype(bf16)

    def diff_v(y):
        ones = jnp.ones((tm, LANES), bf16)
        for c in range(hd2 // LANES):
            vd_ref[:, (2 * c) * LANES:(2 * c + 1) * LANES] = y[:, c * LANES:(c + 1) * LANES].astype(bf16)
            vd_ref[:, (2 * c + 1) * LANES:(2 * c + 2) * LANES] = ones

    def nsa_q(y):
        for c in range(NSA_HEADS // 2):
            slab = _rope(y[:, c * LANES:(c + 1) * LANES], tables) * scale
            qn_ref[:, (2 * c) * LANES:(2 * c + 1) * LANES] = jnp.where(low, slab, 0.0).astype(bf16)
            qn_ref[:, (2 * c + 1) * LANES:(2 * c + 2) * LANES] = jnp.where(
                low, pltpu.roll(slab, HEAD_DIM, 1), 0.0).astype(bf16)

    def cmp_kv(y):
        kc_ref[...] = y[:, :LANES]
        vc_ref[...] = y[:, LANES:]

    def grouped(ref, y, fill):
        ref[0] = jnp.where(low, y, fill).astype(ref.dtype)
        ref[1] = jnp.where(low, pltpu.roll(y, HEAD_DIM, 1), fill).astype(ref.dtype)

    def sel_kv(y):
        row = lax.broadcasted_iota(jnp.int32, (tm, LANES), 0) + lax.rem(pl.program_id(0) * tm, seq)
        onehot = jnp.where(lane - HEAD_DIM == row // SLC_BLOCK, 1.0, 0.0)
        grouped(ks_ref, _rope(y[:, :LANES], tables), onehot)
        grouped(vs_ref, y[:, LANES:], 1.0)

    def win_kv(y):
        grouped(kw_ref, _rope(y[:, :LANES], tables), 0.0)
        grouped(vw_ref, y[:, LANES:], 1.0)

    def gate(y):
        sig = jax.nn.sigmoid(y)
        g_ref[0] = sig
        g_ref[1] = pltpu.roll(sig, LANES - NSA_GQA * 3, 1)

    widths = [hd2, hd2, hd2, NSA_HEADS * HEAD_DIM, 2 * LANES, 2 * LANES, 2 * LANES, LANES]
    epilogues = [diff_q, diff_k, diff_v, nsa_q, cmp_kv, sel_kv, win_kv, gate]
    cols = [sum(widths[:n]) for n in range(len(widths))]

    def seg(n):
        return jnp.dot(xb, w_ref[:, cols[n]:cols[n] + widths[n]], preferred_element_type=f32)

    y_next = seg(0)
    for n, epilogue in enumerate(epilogues):
        y = y_next
        if n + 1 < len(epilogues):
            y_next = seg(n + 1)
        epilogue(y)


def _proj(x2, pos_col, w_pad, consts, seq):
    n, d_model = x2.shape
    tm = PROJ_ROWS
    hd2 = DIFF_HEADS * 2 * HEAD_DIM
    row_spec = lambda w: pl.BlockSpec((tm, w), lambda i: (i, 0))
    grp_spec = pl.BlockSpec((NSA_KV_GROUPS, tm, LANES), lambda i: (0, i, 0))
    out_shape = (
        jax.ShapeDtypeStruct((n, hd2), bf16), jax.ShapeDtypeStruct((n, hd2), bf16),
        jax.ShapeDtypeStruct((n, 2 * hd2), bf16), jax.ShapeDtypeStruct((n, NSA_HEADS * LANES), bf16),
        jax.ShapeDtypeStruct((n, LANES), f32), jax.ShapeDtypeStruct((n, LANES), f32),
        jax.ShapeDtypeStruct((NSA_KV_GROUPS, n, LANES), bf16), jax.ShapeDtypeStruct((NSA_KV_GROUPS, n, LANES), bf16),
        jax.ShapeDtypeStruct((NSA_KV_GROUPS, n, LANES), bf16), jax.ShapeDtypeStruct((NSA_KV_GROUPS, n, LANES), bf16),
        jax.ShapeDtypeStruct((NSA_KV_GROUPS, n, LANES), f32),
    )
    out_specs = (row_spec(hd2), row_spec(hd2), row_spec(2 * hd2), row_spec(NSA_HEADS * LANES),
                 row_spec(LANES), row_spec(LANES), grp_spec, grp_spec, grp_spec, grp_spec, grp_spec)
    return pl.pallas_call(
        functools.partial(_proj_kernel, seq=seq),
        grid=(n // tm,),
        in_specs=[row_spec(d_model), pl.BlockSpec((tm, 1), lambda i: (i, 0)),
                  _const_spec(w_pad.shape), _const_spec(consts.shape)],
        out_specs=out_specs, out_shape=out_shape,
        compiler_params=_cparams(("parallel",)), name="proj",
    )(x2, pos_col, w_pad, consts)


def _compress_one(c_ref, pea_ref, peb_ref, w1a_ref, w1b_ref, b1_ref, w2_ref):
    n = c_ref.shape[1] // CMP_STRIDE
    ck = jnp.concatenate([c_ref[0, pl.ds(t, n, stride=CMP_STRIDE), :] for t in range(CMP_STRIDE)], axis=1)
    hid_a = jnp.dot((ck + pea_ref[...]).astype(bf16), w1a_ref[...], preferred_element_type=f32)
    hid_b = jnp.dot((ck + peb_ref[...]).astype(bf16), w1b_ref[...], preferred_element_type=f32)
    hid = hid_a + pltpu.roll(hid_b, n - 1, 0) + b1_ref[...]
    hid = jax.nn.gelu(hid)
    return jnp.dot(hid.astype(bf16), w2_ref[...], preferred_element_type=f32)


def _compress_kernel(kc_ref, vc_ref, pos_ref, c_ref,
                     pak_ref, pbk_ref, w1ak_ref, w1bk_ref, b1k_ref, w2k_ref,
                     pav_ref, pbv_ref, w1av_ref, w1bv_ref, b1v_ref, w2v_ref,
                     kcmp_ref, vcmp_ref):
    k = _compress_one(kc_ref, pak_ref, pbk_ref, w1ak_ref, w1bk_ref, b1k_ref, w2k_ref)
    k = _rope(k, _rope_tables(pos_ref[0], c_ref))
    low = lax.broadcasted_iota(jnp.int32, k.shape, 1) < HEAD_DIM
    kcmp_ref[0, 0] = jnp.where(low, k, 0.0).astype(bf16)
    kcmp_ref[0, 1] = jnp.where(low, pltpu.roll(k, HEAD_DIM, 1), 0.0).astype(bf16)
    v = _compress_one(vc_ref, pav_ref, pbv_ref, w1av_ref, w1bv_ref, b1v_ref, w2v_ref)
    vcmp_ref[0, 0] = jnp.where(low, v, 0.0).astype(bf16)
    vcmp_ref[0, 1] = jnp.where(low, pltpu.roll(v, HEAD_DIM, 1), 0.0).astype(bf16)


def _compress_weights(pe, w1, b1, w2):
    r = CMP_BLOCK // CMP_STRIDE
    assert r == 2
    eye = jnp.eye(NSA_KV_GROUPS, dtype=f32)
    w1r = w1.reshape(CMP_BLOCK, HEAD_DIM, CMP_HIDDEN)
    per = CMP_STRIDE * NSA_KV_GROUPS * HEAD_DIM

    def big(part):
        return jnp.einsum('tcm,gh->tgchm', part, eye).reshape(per, NSA_KV_GROUPS * CMP_HIDDEN).astype(bf16)

    def pe_row(part):
        return jnp.broadcast_to(part[:, None, :], (CMP_STRIDE, NSA_KV_GROUPS, HEAD_DIM)).reshape(1, per)

    w2b = jnp.einsum('mc,gh->gmhc', w2, eye).reshape(NSA_KV_GROUPS * CMP_HIDDEN, NSA_KV_GROUPS * HEAD_DIM)
    return (pe_row(pe[:CMP_STRIDE]), pe_row(pe[CMP_STRIDE:]), big(w1r[:CMP_STRIDE]), big(w1r[CMP_STRIDE:]),
            jnp.tile(b1, NSA_KV_GROUPS).reshape(1, -1), w2b.astype(bf16))


def _compress(kc, vc, cmp_pos, consts, wk, wv):
    b, s, width = kc.shape
    nch = s // CMP_STRIDE
    seq_spec = pl.BlockSpec((1, s, width), lambda i: (i, 0, 0))
    w_specs = [_const_spec(w.shape) for w in wk + wv]
    return pl.pallas_call(
        _compress_kernel,
        grid=(b,),
        in_specs=[seq_spec, seq_spec, pl.BlockSpec((1, nch, 1), lambda i: (i, 0, 0)),
                  _const_spec(consts.shape)] + w_specs,
        out_specs=(pl.BlockSpec((1, NSA_KV_GROUPS, nch, LANES), lambda i: (i, 0, 0, 0)),
                   pl.BlockSpec((1, NSA_KV_GROUPS, nch, LANES), lambda i: (i, 0, 0, 0))),
        out_shape=(jax.ShapeDtypeStruct((b, NSA_KV_GROUPS, nch, LANES), bf16),
                   jax.ShapeDtypeStruct((b, NSA_KV_GROUPS, nch, LANES), bf16)),
        compiler_params=_cparams(("parallel",)), name="compress",
    )(kc, vc, cmp_pos, consts, *wk, *wv)


def _softmax_init(m_ref, acc_ref):
    m_ref[...] = jnp.full(m_ref.shape, NEG_INF, f32)
    acc_ref[...] = jnp.zeros(acc_ref.shape, f32)


def _softmax_step(s, v_ones, m_ref, acc_ref, rs):
    m_prev = m_ref[rs, :]
    m_new = jnp.maximum(m_prev, jnp.max(s, axis=1, keepdims=True))
    alpha = jnp.exp2(m_prev - m_new)
    p = jnp.exp2((s - jnp.concatenate([m_new] * (s.shape[1] // LANES), axis=1)).astype(bf16))
    pv = jnp.dot(p, v_ones, preferred_element_type=f32)
    acc_ref[rs, :] = jnp.concatenate([alpha] * (acc_ref.shape[1] // LANES), axis=1) * acc_ref[rs, :] + pv
    m_ref[rs, :] = m_new


def _scores(q, k):
    return lax.dot_general(q, k, (((1,), (1,)), ((), ())), preferred_element_type=f32)


def _run_steps(steps, m_ref, acc_ref):
    s_next = steps[0][0]()
    for n, (_, v_ones_fn, rows) in enumerate(steps):
        s = s_next
        if n + 1 < len(steps):
            s_next = steps[n + 1][0]()
        _softmax_step(s, v_ones_fn(), m_ref, acc_ref, rows)


def _diff_kernel(q_ref, k_ref, v_ref, lam_ref, g_ref, o_ref, q2_ref, m_ref, acc_ref, *, lam_init):
    tq = q_ref.shape[1]
    tk = min(DIFF_K_TILE, tq)
    per_q = tq // tk
    rc = min(SOFTMAX_ROWS, tq)
    vdim = o_ref.shape[2]
    i = pl.program_id(2)
    q = q_ref[0]
    low = lax.broadcasted_iota(jnp.int32, q.shape, 1) < HEAD_DIM
    zero = jnp.zeros_like(q)
    q2_ref[0:tq, :] = jnp.where(low, q, zero)
    q2_ref[tq:2 * tq, :] = jnp.where(low, zero, q)
    _softmax_init(m_ref, acc_ref)

    def tile_steps(j, k_off=None):
        off = pl.multiple_of(j * tk, tk)
        steps = []

        def add(row0, rows):
            q_off = row0 % tq
            cols = tk if k_off is None else min(tk, q_off + rows - k_off)
            if cols <= 0:
                return
            masked = k_off is not None and q_off - k_off + 1 < cols

            def score():
                s = _scores(q2_ref[row0:row0 + rows, :], k_ref[0, pl.ds(off, cols), :])
                if masked:
                    r = q_off + lax.broadcasted_iota(jnp.int32, (rows, cols), 0)
                    s = jnp.where(r >= k_off + lax.broadcasted_iota(jnp.int32, (rows, cols), 1), s, NEG_INF)
                return s

            steps.append((score, lambda: v_ref[0, pl.ds(off, cols), :], slice(row0, row0 + rows)))

        big = min(2 * rc, tq)
        for row0 in range(0, 2 * tq, big):
            if k_off is None or row0 % tq - k_off + 1 >= tk:
                add(row0, big)
            else:
                for sub in range(row0, row0 + big, rc):
                    add(sub, rc)
        return steps

    group = 2 if per_q % 2 == 0 else 1

    def body(j, carry):
        _run_steps(sum((tile_steps(group * j + t) for t in range(group)), []), m_ref, acc_ref)
        return carry

    lax.fori_loop(0, i * (per_q // group), body, 0)
    _run_steps(sum((tile_steps(i * per_q + t, t * tk) for t in range(per_q)), []), m_ref, acc_ref)

    lam_v = lam_ref[...]
    lam = (jnp.exp(jnp.sum(lam_v[0:1] * lam_v[1:2], axis=1, keepdims=True))
           - jnp.exp(jnp.sum(lam_v[2:3] * lam_v[3:4], axis=1, keepdims=True)) + lam_init)
    o = acc_ref[:, :vdim] / acc_ref[:, vdim:]
    od = o[:tq] - lam * o[tq:]
    od = od * lax.rsqrt(jnp.mean(od * od, axis=-1, keepdims=True) + RMS_EPS)
    o_ref[0] = (od * g_ref[...] * (1.0 - lam_init)).astype(o_ref.dtype)


def _diff_attention(qd, kd, vd, lam_vec, diff_g, lam_init):
    b, s, _ = qd.shape
    tq = min(ATTN_TILE, s)
    vdim = vd.shape[2] // DIFF_HEADS // 2
    assert vdim == LANES
    return pl.pallas_call(
        functools.partial(_diff_kernel, lam_init=lam_init),
        grid=(b, DIFF_HEADS, s // tq),
        in_specs=[pl.BlockSpec((1, tq, LANES), lambda bi, h, i: (bi, i, h)),
                  pl.BlockSpec((1, s, LANES), lambda bi, h, i: (bi, 0, h)),
                  pl.BlockSpec((1, s, 2 * vdim), lambda bi, h, i: (bi, 0, h)),
                  _const_spec(lam_vec.shape), _const_spec(diff_g.shape)],
        out_specs=pl.BlockSpec((1, tq, vdim), lambda bi, h, i: (bi, i, h)),
        out_shape=jax.ShapeDtypeStruct((b, s, DIFF_HEADS * vdim), bf16),
        scratch_shapes=[pltpu.VMEM((2 * tq, LANES), bf16), pltpu.VMEM((2 * tq, LANES), f32),
                        pltpu.VMEM((2 * tq, 2 * vdim), f32)],
        compiler_params=_cparams(("parallel", "parallel", "arbitrary")), name="diff_attn",
    )(qd, kd, vd, lam_vec, diff_g)


def _nsa_kernel(q_ref, kc_ref, vc_ref, ks_ref, vs_ref, kw_ref, vw_ref, g_ref, ov_ref, o_ref,
                qa_ref, m_ref, acc_ref, mw_ref, accw_ref, oc_ref, imp_ref, rank_ref, gate_ref, *, top_k):
    tq = q_ref.shape[1]
    seq = ks_ref.shape[2]
    tk = min(NSA_K_TILE, seq)
    ncp = kc_ref.shape[2]
    nslc = ov_ref.shape[0]
    rep = NSA_GQA
    i = pl.program_id(2)
    s0 = i * tq
    heads = [slice(r * tq, (r + 1) * tq) for r in range(rep)]

    def tile_steps(pair_q, k_ref, v_ref, back, width, lo=None, hi=None):
        off = pl.multiple_of(s0 - back, tq)
        steps = []
        for r in range(0, rep, 2):

            def score(r=r):
                s = _scores(pair_q(r), k_ref[0, 0, pl.ds(off, width), :])
                if lo is None and hi is None:
                    return s
                d = (lax.broadcasted_iota(jnp.int32, (2 * tq, width), 1)
                     - lax.rem(lax.broadcasted_iota(jnp.int32, (2 * tq, width), 0), tq))
                if hi is not None:
                    s = jnp.where(d <= hi, s, NEG_INF)
                if lo is not None:
                    s = jnp.where(d > lo, s, NEG_INF)
                return s

            steps.append((score, lambda: v_ref[0, 0, pl.ds(off, width), :], slice(r * tq, (r + 2) * tq)))
        return steps

    def normalized(ref):
        acc = ref[...]
        low = lax.broadcasted_iota(jnp.int32, acc.shape, 1) < HEAD_DIM
        return acc / jnp.where(low, pltpu.roll(acc, HEAD_DIM, 1), 1.0)

    def compressed_branch(width, window_steps):
        t_col = s0 + lax.broadcasted_iota(jnp.int32, (tq, width), 0)
        cmp_end = lax.broadcasted_iota(jnp.int32, (tq, width), 1) * CMP_STRIDE + (CMP_BLOCK - 1)
        cmp_ok = cmp_end <= t_col
        kc = kc_ref[0, 0, 0:width, :]
        vc = vc_ref[0, 0, 0:width, :]
        raw = [_scores(q_ref[0, :, r * LANES:(r + 1) * LANES], kc) for r in range(rep)]
        _softmax_init(mw_ref, accw_ref)
        _run_steps(window_steps, mw_ref, accw_ref)
        psum = jnp.zeros((tq, width), f32)
        for r in range(rep):
            s = jnp.where(cmp_ok, raw[r], NEG_INF)
            p = jnp.where(cmp_ok, jnp.exp2(s - jnp.max(s, axis=1, keepdims=True)), 0.0)
            den = jnp.sum(p, axis=1, keepdims=True)
            p = p / jnp.where(den > 0.0, den, 1.0)
            psum = psum + p
            oc_ref[heads[r], :] = jnp.dot(p.astype(bf16), vc, preferred_element_type=f32)
        p_hi = psum.astype(bf16)
        p_lo = (psum - p_hi.astype(f32)).astype(bf16)
        ov = ov_ref[:, 0:width]
        imp_ref[...] = _scores(ov, p_hi) + _scores(ov, p_lo)
        gates = g_ref[0, 0]
        for c in range(3 * rep):
            gate_ref[c] = jnp.broadcast_to(gates[:, c:c + 1], (tq, LANES))

    win = functools.partial(
        tile_steps,
        lambda r: jnp.concatenate([q_ref[0, :, h * LANES:(h + 1) * LANES] for h in (r, r + 1)], axis=0),
        kw_ref, vw_ref)
    n_back = WINDOW // tq
    windows = [lambda n=n: win(n * tq, (n + 1) * tq, hi=n * tq) for n in range(n_back)]
    windows.append(lambda: win(WINDOW, WINDOW + tq, lo=0, hi=WINDOW))
    half = ncp // 2
    has_half = half % LANES == 0
    last_half_tile = half // (tq // CMP_STRIDE) - 1 if has_half else -1
    assert not has_half or last_half_tile >= n_back
    for n, window in enumerate(windows):
        this_window = (i == n) if n < n_back else (i >= n_back)
        if not has_half:
            pl.when(this_window)(functools.partial(lambda w: compressed_branch(ncp, w()), window))
        elif n < n_back:
            pl.when(this_window)(functools.partial(lambda w: compressed_branch(half, w()), window))
        else:
            pl.when(this_window & (i <= last_half_tile))(
                functools.partial(lambda w: compressed_branch(half, w()), window))
            pl.when(i > last_half_tile)(functools.partial(lambda w: compressed_branch(ncp, w()), window))

    blk = lax.broadcasted_iota(jnp.int32, (nslc, tq), 0)
    cur = (s0 + lax.broadcasted_iota(jnp.int32, (nslc, tq), 1)) // SLC_BLOCK
    imp = jnp.where(blk > cur, -1.0, imp_ref[...])
    imp = jnp.where((blk == 0) | (blk == cur) | (blk == cur - 1), FORCED_SCORE, imp)
    rank_ref[...] = jnp.zeros((nslc, tq), f32)
    per_tile = tq // SLC_BLOCK
    for first in range(0, nslc, per_tile):

        @pl.when(first <= i * per_tile)
        def _():
            rank = rank_ref[...]
            for jp in range(first, min(first + per_tile, nslc)):
                other = imp[jp:jp + 1, :]
                ahead = jnp.where(blk > jp, jnp.where(other >= imp, 1.0, 0.0), jnp.where(other > imp, 1.0, 0.0))
                rank = rank + ahead
            rank_ref[...] = rank

    bias_t = jnp.where(rank_ref[...] < top_k, 0.0, SEL_BIAS)
    pieces = [jnp.zeros((HEAD_DIM, tq), f32), bias_t]
    if nslc < LANES - HEAD_DIM:
        pieces.append(jnp.zeros((LANES - HEAD_DIM - nslc, tq), f32))
    bias = jnp.concatenate(pieces, axis=0).T.astype(bf16)
    for r in range(rep):
        qa_ref[heads[r], :] = q_ref[0, :, r * LANES:(r + 1) * LANES] + bias

    _softmax_init(m_ref, acc_ref)
    sel = functools.partial(tile_steps, lambda r: qa_ref[r * tq:(r + 2) * tq, :], ks_ref, vs_ref)
    trip = NSA_TRIP_TILES * tk

    def sel_body(j, carry):
        back = s0 - j * trip
        _run_steps(sum((sel(back - t * 2 * tk, 2 * tk) for t in range(NSA_TRIP_TILES // 2)), []), m_ref, acc_ref)
        return carry

    lax.fori_loop(0, s0 // trip, sel_body, 0)
    tail = lax.rem(i, trip // tq)
    for n in range(trip // tq):
        whole = n // 2
        steps = sum((sel(n * tq - t * 2 * tk, 2 * tk) for t in range(whole // 2)), [])
        if whole % 2:
            steps += sel(n * tq - (whole - 1) * tk, tk)
        steps += sel(tq, tk, hi=tq) if n % 2 else sel(0, tq, hi=0)
        pl.when(tail == n)(functools.partial(_run_steps, steps, m_ref, acc_ref))

    o_sel = normalized(acc_ref)
    o_win = normalized(accw_ref)

    lane = lax.broadcasted_iota(jnp.int32, (tq, LANES), 1)
    gated = []
    for r in range(rep):
        gated.append(gate_ref[3 * r] * oc_ref[heads[r], :] + gate_ref[3 * r + 1] * o_sel[heads[r]]
                     + gate_ref[3 * r + 2] * o_win[heads[r]])
    for c in range(rep // 2):
        o_ref[0, :, c * LANES:(c + 1) * LANES] = jnp.where(
            lane < HEAD_DIM, gated[2 * c], pltpu.roll(gated[2 * c + 1], HEAD_DIM, 1)).astype(o_ref.dtype)


def _nsa_attention(qn, kcmp, vcmp, ksa, vsa, kwa, vwa, gates, overlap_t, top_k):
    b, s, _ = qn.shape
    tq = min(NSA_Q_TILE, s)
    ncp = kcmp.shape[2]
    rep = NSA_GQA
    assert WINDOW == 2 * tq and min(NSA_K_TILE, s) == 2 * tq
    cmp_spec = pl.BlockSpec((1, 1, ncp, LANES), lambda bi, g, i: (bi, g, 0, 0))
    seq_spec = pl.BlockSpec((1, 1, s, LANES), lambda bi, g, i: (g, bi, 0, 0))
    rows = rep * tq
    return pl.pallas_call(
        functools.partial(_nsa_kernel, top_k=top_k),
        grid=(b, NSA_KV_GROUPS, s // tq),
        in_specs=[pl.BlockSpec((1, tq, rep * LANES), lambda bi, g, i: (bi, i, g)),
                  cmp_spec, cmp_spec, seq_spec, seq_spec, seq_spec, seq_spec,
                  pl.BlockSpec((1, 1, tq, LANES), lambda bi, g, i: (g, bi, i, 0)),
                  _const_spec(overlap_t.shape)],
        out_specs=pl.BlockSpec((1, tq, rep * HEAD_DIM), lambda bi, g, i: (bi, i, g)),
        out_shape=jax.ShapeDtypeStruct((b, s, NSA_HEADS * HEAD_DIM), bf16),
        scratch_shapes=[pltpu.VMEM((rows, LANES), bf16)] + [pltpu.VMEM((rows, LANES), f32)] * 5 + [
                        pltpu.VMEM(overlap_t.shape[:1] + (tq,), f32),
                        pltpu.VMEM(overlap_t.shape[:1] + (tq,), f32), pltpu.VMEM((3 * rep, tq, LANES), f32)],
        compiler_params=_cparams(("parallel", "parallel", "arbitrary")), name="nsa_attn",
    )(qn, kcmp, vcmp, ksa, vsa, kwa, vwa, gates, overlap_t)


def _outproj_kernel(x_ref, od_ref, on_ref, wa_ref, wb_ref, g_ref, b_ref, o_ref, *, alpha):
    parts = 4
    rows = x_ref.shape[0] // parts

    def mix(r):
        rs = slice(r * rows, (r + 1) * rows)
        return (jnp.dot(od_ref[rs, :], wa_ref[...], preferred_element_type=f32)
                + jnp.dot(on_ref[rs, :], wb_ref[...], preferred_element_type=f32))

    y_next = mix(0)
    for r in range(parts):
        rs = slice(r * rows, (r + 1) * rows)
        y = y_next
        if r + 1 < parts:
            y_next = mix(r + 1)
        o_ref[rs, :] = _layer_norm(alpha * x_ref[rs, :] + y, g_ref[...], b_ref[...])


def _outproj(x2, od, on, wa, wb, g, b, alpha):
    n, d_model = x2.shape
    tm = 2 * DENSE_ROWS
    row_spec = lambda w: pl.BlockSpec((tm, w), lambda i: (i, 0))
    return pl.pallas_call(
        functools.partial(_outproj_kernel, alpha=alpha),
        grid=(n // tm,),
        in_specs=[row_spec(d_model), row_spec(od.shape[1]), row_spec(on.shape[1]),
                  _const_spec(wa.shape), _const_spec(wb.shape), _const_spec(g.shape), _const_spec(b.shape)],
        out_specs=row_spec(d_model),
        out_shape=jax.ShapeDtypeStruct((n, d_model), f32),
        compiler_params=_cparams(("parallel",)), name="outproj_ln",
    )(x2, od, on, wa, wb, g, b)


def _ffn_kernel(h_ref, halo_ref, wu_ref, cw_ref, cb_ref, wd_ref, g_ref, b_ref, o_ref, hb_ref, u_ref, act_ref,
                *, alpha, tiles_per_seq):
    tm = h_ref.shape[0]
    tf = FFN_CHUNK
    n_chunks = wd_ref.shape[0] // tf
    h = h_ref[...]
    first = lax.rem(pl.program_id(0), tiles_per_seq) == 0
    halo = jnp.where(first, 0.0, halo_ref[...])
    d_ff = wd_ref.shape[0]
    hb_ref[...] = jnp.concatenate([halo, h], axis=0).astype(bf16)

    def up(c):
        for part, col in enumerate((c * tf, d_ff + c * tf)):
            u_ref[c % 2, :, part * tf:(part + 1) * tf] = jnp.dot(
                hb_ref[...], wu_ref[:, col:col + tf], preferred_element_type=f32)

    def conv(c, part):
        col = part * d_ff + c * tf
        u_all = u_ref[c % 2, :, part * tf:(part + 1) * tf]
        u = cb_ref[:, col:col + tf]
        for k in range(CONV_WIDTH):
            lag = CONV_WIDTH - 1 - k
            shifted = u_all if lag == 0 else pltpu.roll(u_all, lag, 0)
            u = u + cw_ref[k:k + 1, col:col + tf] * shifted[HALO:, :]
        return u

    up(0)
    for c in range(n_chunks):
        if c + 1 < n_chunks:
            up(c + 1)
        act_ref[:, c * tf:(c + 1) * tf] = (jax.nn.silu(conv(c, 0)) * conv(c, 1)).astype(bf16)
    parts = 2
    rows = tm // parts
    down = lambda r: jnp.dot(act_ref[r * rows:(r + 1) * rows, :], wd_ref[...], preferred_element_type=f32)
    y_next = down(0)
    for r in range(parts):
        rs = slice(r * rows, (r + 1) * rows)
        y = y_next
        if r + 1 < parts:
            y_next = down(r + 1)
        o_ref[rs, :] = _layer_norm(alpha * h_ref[rs, :] + y, g_ref[...], b_ref[...])


def _ffn(h1, wu, cw, cb, wd, g, b, alpha, seq):
    n, d_model = h1.shape
    tm = min(DENSE_ROWS, seq)
    single = dict(pipeline_mode=pl.Buffered(1))
    return pl.pallas_call(
        functools.partial(_ffn_kernel, alpha=alpha, tiles_per_seq=seq // tm),
        grid=(n // tm,),
        in_specs=[pl.BlockSpec((tm, d_model), lambda i: (i, 0)),
                  pl.BlockSpec((HALO, d_model), lambda i: (jnp.maximum(i * (tm // HALO) - 1, 0), 0)),
                  pl.BlockSpec(wu.shape, lambda i: (0, 0), **single),
                  _const_spec(cw.shape), _const_spec(cb.shape),
                  pl.BlockSpec(wd.shape, lambda i: (0, 0), **single),
                  _const_spec(g.shape), _const_spec(b.shape)],
        out_specs=pl.BlockSpec((tm, d_model), lambda i: (i, 0)),
        out_shape=jax.ShapeDtypeStruct((n, d_model), f32),
        scratch_shapes=[pltpu.VMEM((HALO + tm, d_model), bf16), pltpu.VMEM((2, HALO + tm, 2 * FFN_CHUNK), f32),
                        pltpu.VMEM((tm, wd.shape[0]), bf16)],
        compiler_params=_cparams(("parallel",)), name="ffn_ln",
    )(h1, h1, wu, cw, cb, wd, g, b)


def kernel(x, positions, w_in, lambda_q1, lambda_k1, lambda_q2, lambda_k2, diff_norm_g, cmp_pe_k, cmp_w1_k, cmp_b1_k, cmp_w2_k, cmp_pe_v, cmp_w1_v, cmp_b1_v, cmp_w2_v, w_out, ln1_g, ln1_b, w_up, conv_w, conv_b, w_down, ln2_g, ln2_b):
    b, s, d_model = x.shape
    depth = w_in.shape[0]
    n = b * s
    d_ff = w_down.shape[1]
    assert s % ATTN_TILE == 0 or s < ATTN_TILE
    assert s % SLC_BLOCK == 0 and s // SLC_BLOCK <= LANES - HEAD_DIM and d_ff % FFN_CHUNK == 0
    alpha = (2 * depth) ** 0.25
    consts = _rope_consts()
    pos_col = positions.reshape(n, 1)

    n_chunk = s // CMP_STRIDE
    n_cmp = n_chunk - CMP_BLOCK // CMP_STRIDE + 1
    n_slc = s // SLC_BLOCK
    top_k = min(SLC_TOPK, n_slc)
    cmp_pos = positions[:, CMP_BLOCK - 1::CMP_STRIDE][:, :n_cmp]
    cmp_pos = jnp.pad(cmp_pos, ((0, 0), (0, n_chunk - n_cmp))).reshape(b, n_chunk, 1)
    cs = np.arange(n_chunk)[None, :] * CMP_STRIDE
    ss = np.arange(n_slc)[:, None] * SLC_BLOCK
    ov = np.clip(np.minimum(cs + CMP_BLOCK, ss + SLC_BLOCK) - np.maximum(cs, ss), 0, None) / CMP_BLOCK
    ov[:, n_cmp:] = 0.0
    overlap_t = jnp.asarray(ov, dtype=bf16)

    h = x.reshape(n, d_model)
    for l in range(depth):
        lam_init = 0.8 - 0.6 * math.exp(-0.3 * l)
        d_in = w_in.shape[2]
        w_pad = jnp.pad(w_in[l], ((0, 0), (0, -d_in % LANES))).astype(bf16)
        qd, kd, vd, qn, kc, vc, ksa, vsa, kwa, vwa, gates = _proj(h, pos_col, w_pad, consts, s)

        kcmp, vcmp = _compress(
            kc.reshape(b, s, LANES), vc.reshape(b, s, LANES), cmp_pos, consts,
            _compress_weights(cmp_pe_k[l], cmp_w1_k[l], cmp_b1_k[l], cmp_w2_k[l]),
            _compress_weights(cmp_pe_v[l], cmp_w1_v[l], cmp_b1_v[l], cmp_w2_v[l]))

        lam_vec = jnp.stack([lambda_q1[l], lambda_k1[l], lambda_q2[l], lambda_k2[l]]).astype(f32)
        od = _diff_attention(qd.reshape(b, s, -1), kd.reshape(b, s, -1), vd.reshape(b, s, -1),
                             lam_vec, diff_norm_g[l].reshape(1, -1).astype(f32), lam_init)
        grouped = lambda a: a.reshape(NSA_KV_GROUPS, b, s, LANES)
        on = _nsa_attention(qn.reshape(b, s, -1), kcmp, vcmp, grouped(ksa), grouped(vsa), grouped(kwa),
                            grouped(vwa), grouped(gates), overlap_t, top_k)

        d_diff = od.shape[2]
        wo = w_out[l].astype(bf16)
        h = _outproj(h, od.reshape(n, -1), on.reshape(n, -1), wo[:d_diff], wo[d_diff:],
                     ln1_g[l].reshape(1, -1), ln1_b[l].reshape(1, -1), alpha)
        h = _ffn(h, w_up[l].astype(bf16), conv_w[l], conv_b[l].reshape(1, -1),
                 w_down[l].astype(bf16), ln2_g[l].reshape(1, -1), ln2_b[l].reshape(1, -1), alpha, s)
    return h.reshape(b, s, d_model)
```

```python
import functools
import math

import jax
import jax.numpy as jnp
import numpy as np
from jax import lax
from jax.experimental import pallas as pl
from jax.experimental.pallas import tpu as pltpu

f32 = jnp.float32
bf16 = jnp.bfloat16

LANES = 128
HEAD_DIM = 64
ROPE_DIM = HEAD_DIM // 4
ROPE_THETA = 500000.0
DIFF_HEADS = 4
NSA_HEADS = 8
NSA_KV_GROUPS = 2
NSA_GQA = NSA_HEADS // NSA_KV_GROUPS
CMP_BLOCK = 32
CMP_STRIDE = 16
CMP_HIDDEN = 2 * HEAD_DIM
SLC_BLOCK = 64
SLC_TOPK = 16
WINDOW = 512
CONV_WIDTH = 3
LN_EPS = 1e-5
RMS_EPS = 1e-5
NEG_INF = -1e30
SEL_BIAS = NEG_INF
FORCED_SCORE = 1e6
VMEM_LIMIT = 56 * 1024 * 1024

LOG2E = 1.4426950408889634
QK_SCALE = HEAD_DIM ** -0.5 * LOG2E

PROJ_ROWS = 512
ATTN_TILE = 4096
DIFF_K_TILE = 512
NSA_Q_TILE = 256
NSA_K_TILE = 512
NSA_TRIP_TILES = 4
SOFTMAX_ROWS = 256
DENSE_ROWS = 512
FFN_CHUNK = 256
HALO = 8


def _cparams(sem):
    return pltpu.CompilerParams(dimension_semantics=sem, vmem_limit_bytes=VMEM_LIMIT)


def _const_spec(shape):
    n = len(shape)
    return pl.BlockSpec(shape, lambda *_: (0,) * n)


def _layer_norm(y, g, b):
    mu = jnp.mean(y, axis=-1, keepdims=True)
    d = y - mu
    var = jnp.mean(d * d, axis=-1, keepdims=True)
    return d * lax.rsqrt(var + LN_EPS) * g + b


def _rope_consts():
    lane = np.arange(LANES)
    in_head = lane % HEAD_DIM
    half = ROPE_DIM // 2
    inv_freq = 1.0 / (ROPE_THETA ** (jnp.arange(half, dtype=f32) / half))
    c = jnp.zeros((8, LANES), f32)
    c = c.at[0].set(jnp.tile(inv_freq, LANES // half))
    c = c.at[1].set(jnp.asarray(in_head < ROPE_DIM, f32))
    c = c.at[2].set(jnp.asarray(np.where(in_head < half, -1.0, np.where(in_head < ROPE_DIM, 1.0, 0.0)), f32))
    c = c.at[3].set(jnp.asarray(in_head < half, f32))
    return c


def _rope_tables(pos_col, c_ref):
    ang = pos_col.astype(f32) * c_ref[0:1, :]
    cos_t = jnp.where(c_ref[1:2, :] > 0.0, jnp.cos(ang), 1.0)
    sin_t = jnp.sin(ang) * c_ref[2:3, :]
    return cos_t, sin_t, c_ref[3:4, :] > 0.0


def _rope(y, tables):
    cos_t, sin_t, first = tables
    half = ROPE_DIM // 2
    partner = jnp.where(first, pltpu.roll(y, LANES - half, 1), pltpu.roll(y, half, 1))
    return y * cos_t + partner * sin_t


def _proj_kernel(x_ref, pos_ref, w_ref, c_ref,
                 qd_ref, kd_ref, vd_ref, qn_ref, kc_ref, vc_ref, ks_ref, vs_ref, kw_ref, vw_ref, g_ref,
                 *, seq):
    tm = x_ref.shape[0]
    xb = x_ref[...].astype(bf16)
    tables = _rope_tables(pos_ref[...], c_ref)
    lane = lax.broadcasted_iota(jnp.int32, (tm, LANES), 1)
    low = lane < HEAD_DIM
    scale = QK_SCALE

    hd2 = DIFF_HEADS * 2 * HEAD_DIM

    def diff_q(y):
        for c in range(hd2 // LANES):
            sl = slice(c * LANES, (c + 1) * LANES)
            qd_ref[:, sl] = (_rope(y[:, sl], tables) * scale).astype(bf16)

    def diff_k(y):
        for c in range(hd2 // LANES):
            sl = slice(c * LANES, (c + 1) * LANES)
            kd_ref[:, sl] = _rope(y[:, sl], tables).astype(bf16)

    def diff_v(y):
        ones = jnp.ones((tm, LANES), bf16)
        for c in range(hd2 // LANES):
            vd_ref[:, (2 * c) * LANES:(2 * c + 1) * LANES] = y[:, c * LANES:(c + 1) * LANES].astype(bf16)
            vd_ref[:, (2 * c + 1) * LANES:(2 * c + 2) * LANES] = ones

    def nsa_q(y):
        for c in range(NSA_HEADS // 2):
            slab = _rope(y[:, c * LANES:(c + 1) * LANES], tables) * scale
            qn_ref[:, (2 * c) * LANES:(2 * c + 1) * LANES] = jnp.where(low, slab, 0.0).astype(bf16)
            qn_ref[:, (2 * c + 1) * LANES:(2 * c + 2) * LANES] = jnp.where(
                low, pltpu.roll(slab, HEAD_DIM, 1), 0.0).astype(bf16)

    def cmp_kv(y):
        kc_ref[...] = y[:, :LANES]
        vc_ref[...] = y[:, LANES:]

    def grouped(ref, y, fill):
        ref[0] = jnp.where(low, y, fill).astype(ref.dtype)
        ref[1] = jnp.where(low, pltpu.roll(y, HEAD_DIM, 1), fill).astype(ref.dtype)

    def sel_kv(y):
        row = lax.broadcasted_iota(jnp.int32, (tm, LANES), 0) + lax.rem(pl.program_id(0) * tm, seq)
        onehot = jnp.where(lane - HEAD_DIM == row // SLC_BLOCK, 1.0, 0.0)
        grouped(ks_ref, _rope(y[:, :LANES], tables), onehot)
        grouped(vs_ref, y[:, LANES:], 1.0)

    def win_kv(y):
        grouped(kw_ref, _rope(y[:, :LANES], tables), 0.0)
        grouped(vw_ref, y[:, LANES:], 1.0)

    def gate(y):
        sig = jax.nn.sigmoid(y)
        g_ref[0] = sig
        g_ref[1] = pltpu.roll(sig, LANES - NSA_GQA * 3, 1)

    widths = [hd2, hd2, hd2, NSA_HEADS * HEAD_DIM, 2 * LANES, 2 * LANES, 2 * LANES, LANES]
    epilogues = [diff_q, diff_k, diff_v, nsa_q, cmp_kv, sel_kv, win_kv, gate]
    cols = [sum(widths[:n]) for n in range(len(widths))]

    def seg(n):
        return jnp.dot(xb, w_ref[:, cols[n]:cols[n] + widths[n]], preferred_element_type=f32)

    y_next = seg(0)
    for n, epilogue in enumerate(epilogues):
        y = y_next
        if n + 1 < len(epilogues):
            y_next = seg(n + 1)
        epilogue(y)


def _proj(x2, pos_col, w_pad, consts, seq):
    n, d_model = x2.shape
    tm = PROJ_ROWS
    hd2 = DIFF_HEADS * 2 * HEAD_DIM
    row_spec = lambda w: pl.BlockSpec((tm, w), lambda i: (i, 0))
    grp_spec = pl.BlockSpec((NSA_KV_GROUPS, tm, LANES), lambda i: (0, i, 0))
    out_shape = (
        jax.ShapeDtypeStruct((n, hd2), bf16), jax.ShapeDtypeStruct((n, hd2), bf16),
        jax.ShapeDtypeStruct((n, 2 * hd2), bf16), jax.ShapeDtypeStruct((n, NSA_HEADS * LANES), bf16),
        jax.ShapeDtypeStruct((n, LANES), f32), jax.ShapeDtypeStruct((n, LANES), f32),
        jax.ShapeDtypeStruct((NSA_KV_GROUPS, n, LANES), bf16), jax.ShapeDtypeStruct((NSA_KV_GROUPS, n, LANES), bf16),
        jax.ShapeDtypeStruct((NSA_KV_GROUPS, n, LANES), bf16), jax.ShapeDtypeStruct((NSA_KV_GROUPS, n, LANES), bf16),
        jax.ShapeDtypeStruct((NSA_KV_GROUPS, n, LANES), f32),
    )
    out_specs = (row_spec(hd2), row_spec(hd2), row_spec(2 * hd2), row_spec(NSA_HEADS * LANES),
                 row_spec(LANES), row_spec(LANES), grp_spec, grp_spec, grp_spec, grp_spec, grp_spec)
    return pl.pallas_call(
        functools.partial(_proj_kernel, seq=seq),
        grid=(n // tm,),
        in_specs=[row_spec(d_model), pl.BlockSpec((tm, 1), lambda i: (i, 0)),
                  _const_spec(w_pad.shape), _const_spec(consts.shape)],
        out_specs=out_specs, out_shape=out_shape,
        compiler_params=_cparams(("parallel",)), name="proj",
    )(x2, pos_col, w_pad, consts)


def _compress_one(c_ref, pea_ref, peb_ref, w1a_ref, w1b_ref, b1_ref, w2_ref):
    n = c_ref.shape[1] // CMP_STRIDE
    ck = jnp.concatenate([c_ref[0, pl.ds(t, n, stride=CMP_STRIDE), :] for t in range(CMP_STRIDE)], axis=1)
    hid_a = jnp.dot((ck + pea_ref[...]).astype(bf16), w1a_ref[...], preferred_element_type=f32)
    hid_b = jnp.dot((ck + peb_ref[...]).astype(bf16), w1b_ref[...], preferred_element_type=f32)
    hid = hid_a + pltpu.roll(hid_b, n - 1, 0) + b1_ref[...]
    hid = jax.nn.gelu(hid)
    return jnp.dot(hid.astype(bf16), w2_ref[...], preferred_element_type=f32)


def _compress_kernel(kc_ref, vc_ref, pos_ref, c_ref,
                     pak_ref, pbk_ref, w1ak_ref, w1bk_ref, b1k_ref, w2k_ref,
                     pav_ref, pbv_ref, w1av_ref, w1bv_ref, b1v_ref, w2v_ref,
                     kcmp_ref, vcmp_ref):
    k = _compress_one(kc_ref, pak_ref, pbk_ref, w1ak_ref, w1bk_ref, b1k_ref, w2k_ref)
    k = _rope(k, _rope_tables(pos_ref[0], c_ref))
    low = lax.broadcasted_iota(jnp.int32, k.shape, 1) < HEAD_DIM
    kcmp_ref[0, 0] = jnp.where(low, k, 0.0).astype(bf16)
    kcmp_ref[0, 1] = jnp.where(low, pltpu.roll(k, HEAD_DIM, 1), 0.0).astype(bf16)
    v = _compress_one(vc_ref, pav_ref, pbv_ref, w1av_ref, w1bv_ref, b1v_ref, w2v_ref)
    vcmp_ref[0, 0] = jnp.where(low, v, 0.0).astype(bf16)
    vcmp_ref[0, 1] = jnp.where(low, pltpu.roll(v, HEAD_DIM, 1), 0.0).astype(bf16)


def _compress_weights(pe, w1, b1, w2):
    r = CMP_BLOCK // CMP_STRIDE
    assert r == 2
    eye = jnp.eye(NSA_KV_GROUPS, dtype=f32)
    w1r = w1.reshape(CMP_BLOCK, HEAD_DIM, CMP_HIDDEN)
    per = CMP_STRIDE * NSA_KV_GROUPS * HEAD_DIM

    def big(part):
        return jnp.einsum('tcm,gh->tgchm', part, eye).reshape(per, NSA_KV_GROUPS * CMP_HIDDEN).astype(bf16)

    def pe_row(part):
        return jnp.broadcast_to(part[:, None, :], (CMP_STRIDE, NSA_KV_GROUPS, HEAD_DIM)).reshape(1, per)

    w2b = jnp.einsum('mc,gh->gmhc', w2, eye).reshape(NSA_KV_GROUPS * CMP_HIDDEN, NSA_KV_GROUPS * HEAD_DIM)
    return (pe_row(pe[:CMP_STRIDE]), pe_row(pe[CMP_STRIDE:]), big(w1r[:CMP_STRIDE]), big(w1r[CMP_STRIDE:]),
            jnp.tile(b1, NSA_KV_GROUPS).reshape(1, -1), w2b.astype(bf16))


def _compress(kc, vc, cmp_pos, consts, wk, wv):
    b, s, width = kc.shape
    nch = s // CMP_STRIDE
    seq_spec = pl.BlockSpec((1, s, width), lambda i: (i, 0, 0))
    w_specs = [_const_spec(w.shape) for w in wk + wv]
    return pl.pallas_call(
        _compress_kernel,
        grid=(b,),
        in_specs=[seq_spec, seq_spec, pl.BlockSpec((1, nch, 1), lambda i: (i, 0, 0)),
                  _const_spec(consts.shape)] + w_specs,
        out_specs=(pl.BlockSpec((1, NSA_KV_GROUPS, nch, LANES), lambda i: (i, 0, 0, 0)),
                   pl.BlockSpec((1, NSA_KV_GROUPS, nch, LANES), lambda i: (i, 0, 0, 0))),
        out_shape=(jax.ShapeDtypeStruct((b, NSA_KV_GROUPS, nch, LANES), bf16),
                   jax.ShapeDtypeStruct((b, NSA_KV_GROUPS, nch, LANES), bf16)),
        compiler_params=_cparams(("parallel",)), name="compress",
    )(kc, vc, cmp_pos, consts, *wk, *wv)


def _softmax_init(m_ref, acc_ref):
    m_ref[...] = jnp.full(m_ref.shape, NEG_INF, f32)
    acc_ref[...] = jnp.zeros(acc_ref.shape, f32)


def _softmax_step(s, v_ones, m_ref, acc_ref, rs):
    m_prev = m_ref[rs, :]
    m_new = jnp.maximum(m_prev, jnp.max(s, axis=1, keepdims=True))
    alpha = jnp.exp2(m_prev - m_new)
    p = jnp.exp2((s - jnp.concatenate([m_new] * (s.shape[1] // LANES), axis=1)).astype(bf16))
    pv = jnp.dot(p, v_ones, preferred_element_type=f32)
    acc_ref[rs, :] = jnp.concatenate([alpha] * (acc_ref.shape[1] // LANES), axis=1) * acc_ref[rs, :] + pv
    m_ref[rs, :] = m_new


def _scores(q, k):
    return lax.dot_general(q, k, (((1,), (1,)), ((), ())), preferred_element_type=f32)


def _run_steps(steps, m_ref, acc_ref):
    s_next = steps[0][0]()
    for n, (_, v_ones_fn, rows) in enumerate(steps):
        s = s_next
        if n + 1 < len(steps):
            s_next = steps[n + 1][0]()
        _softmax_step(s, v_ones_fn(), m_ref, acc_ref, rows)


def _diff_kernel(q_ref, k_ref, v_ref, lam_ref, g_ref, o_ref, q2_ref, m_ref, acc_ref, *, lam_init):
    tq = q_ref.shape[1]
    tk = min(DIFF_K_TILE, tq)
    per_q = tq // tk
    rc = min(SOFTMAX_ROWS, tq)
    vdim = o_ref.shape[2]
    i = pl.program_id(2)
    q = q_ref[0]
    low = lax.broadcasted_iota(jnp.int32, q.shape, 1) < HEAD_DIM
    zero = jnp.zeros_like(q)
    q2_ref[0:tq, :] = jnp.where(low, q, zero)
    q2_ref[tq:2 * tq, :] = jnp.where(low, zero, q)
    _softmax_init(m_ref, acc_ref)

    def tile_steps(j, k_off=None):
        off = pl.multiple_of(j * tk, tk)
        steps = []

        def add(row0, rows):
            q_off = row0 % tq
            cols = tk if k_off is None else min(tk, q_off + rows - k_off)
            if cols <= 0:
                return
            masked = k_off is not None and q_off - k_off + 1 < cols

            def score():
                s = _scores(q2_ref[row0:row0 + rows, :], k_ref[0, pl.ds(off, cols), :])
                if masked:
                    r = q_off + lax.broadcasted_iota(jnp.int32, (rows, cols), 0)
                    s = jnp.where(r >= k_off + lax.broadcasted_iota(jnp.int32, (rows, cols), 1), s, NEG_INF)
                return s

            steps.append((score, lambda: v_ref[0, pl.ds(off, cols), :], slice(row0, row0 + rows)))

        big = min(2 * rc, tq)
        for row0 in range(0, 2 * tq, big):
            if k_off is None or row0 % tq - k_off + 1 >= tk:
                add(row0, big)
            else:
                for sub in range(row0, row0 + big, rc):
                    add(sub, rc)
        return steps

    group = 2 if per_q % 2 == 0 else 1

    def body(j, carry):
        _run_steps(sum((tile_steps(group * j + t) for t in range(group)), []), m_ref, acc_ref)
        return carry

    lax.fori_loop(0, i * (per_q // group), body, 0)
    _run_steps(sum((tile_steps(i * per_q + t, t * tk) for t in range(per_q)), []), m_ref, acc_ref)

    lam_v = lam_ref[...]
    lam = (jnp.exp(jnp.sum(lam_v[0:1] * lam_v[1:2], axis=1, keepdims=True))
           - jnp.exp(jnp.sum(lam_v[2:3] * lam_v[3:4], axis=1, keepdims=True)) + lam_init)
    o = acc_ref[:, :vdim] / acc_ref[:, vdim:]
    od = o[:tq] - lam * o[tq:]
    od = od * lax.rsqrt(jnp.mean(od * od, axis=-1, keepdims=True) + RMS_EPS)
    o_ref[0] = (od * g_ref[...] * (1.0 - lam_init)).astype(o_ref.dtype)


def _diff_attention(qd, kd, vd, lam_vec, diff_g, lam_init):
    b, s, _ = qd.shape
    tq = min(ATTN_TILE, s)
    vdim = vd.shape[2] // DIFF_HEADS // 2
    assert vdim == LANES
    return pl.pallas_call(
        functools.partial(_diff_kernel, lam_init=lam_init),
        grid=(b, DIFF_HEADS, s // tq),
        in_specs=[pl.BlockSpec((1, tq, LANES), lambda bi, h, i: (bi, i, h)),
                  pl.BlockSpec((1, s, LANES), lambda bi, h, i: (bi, 0, h)),
                  pl.BlockSpec((1, s, 2 * vdim), lambda bi, h, i: (bi, 0, h)),
                  _const_spec(lam_vec.shape), _const_spec(diff_g.shape)],
        out_specs=pl.BlockSpec((1, tq, vdim), lambda bi, h, i: (bi, i, h)),
        out_shape=jax.ShapeDtypeStruct((b, s, DIFF_HEADS * vdim), bf16),
        scratch_shapes=[pltpu.VMEM((2 * tq, LANES), bf16), pltpu.VMEM((2 * tq, LANES), f32),
                        pltpu.VMEM((2 * tq, 2 * vdim), f32)],
        compiler_params=_cparams(("parallel", "parallel", "arbitrary")), name="diff_attn",
    )(qd, kd, vd, lam_vec, diff_g)


def _nsa_kernel(q_ref, kc_ref, vc_ref, ks_ref, vs_ref, kw_ref, vw_ref, g_ref, ov_ref, o_ref,
                qa_ref, m_ref, acc_ref, mw_ref, accw_ref, oc_ref, imp_ref, rank_ref, gate_ref, *, top_k):
    tq = q_ref.shape[1]
    seq = ks_ref.shape[2]
    tk = min(NSA_K_TILE, seq)
    ncp = kc_ref.shape[2]
    nslc = ov_ref.shape[0]
    rep = NSA_GQA
    i = pl.program_id(2)
    s0 = i * tq
    heads = [slice(r * tq, (r + 1) * tq) for r in range(rep)]

    def tile_steps(pair_q, k_ref, v_ref, back, width, lo=None, hi=None):
        off = pl.multiple_of(s0 - back, tq)
        steps = []
        for r in range(0, rep, 2):

            def score(r=r):
                s = _scores(pair_q(r), k_ref[0, 0, pl.ds(off, width), :])
                if lo is None and hi is None:
                    return s
                d = (lax.broadcasted_iota(jnp.int32, (2 * tq, width), 1)
                     - lax.rem(lax.broadcasted_iota(jnp.int32, (2 * tq, width), 0), tq))
                if hi is not None:
                    s = jnp.where(d <= hi, s, NEG_INF)
                if lo is not None:
                    s = jnp.where(d > lo, s, NEG_INF)
                return s

            steps.append((score, lambda: v_ref[0, 0, pl.ds(off, width), :], slice(r * tq, (r + 2) * tq)))
        return steps

    def normalized(ref):
        acc = ref[...]
        low = lax.broadcasted_iota(jnp.int32, acc.shape, 1) < HEAD_DIM
        return acc / jnp.where(low, pltpu.roll(acc, HEAD_DIM, 1), 1.0)

    def compressed_branch(width, window_steps):
        t_col = s0 + lax.broadcasted_iota(jnp.int32, (tq, width), 0)
        cmp_end = lax.broadcasted_iota(jnp.int32, (tq, width), 1) * CMP_STRIDE + (CMP_BLOCK - 1)
        cmp_ok = cmp_end <= t_col
        kc = kc_ref[0, 0, 0:width, :]
        vc = vc_ref[0, 0, 0:width, :]
        raw = [_scores(q_ref[0, :, r * LANES:(r + 1) * LANES], kc) for r in range(rep)]
        _softmax_init(mw_ref, accw_ref)
        _run_steps(window_steps, mw_ref, accw_ref)
        psum = jnp.zeros((tq, width), f32)
        for r in range(rep):
            s = jnp.where(cmp_ok, raw[r], NEG_INF)
            p = jnp.where(cmp_ok, jnp.exp2(s - jnp.max(s, axis=1, keepdims=True)), 0.0)
            den = jnp.sum(p, axis=1, keepdims=True)
            p = p / jnp.where(den > 0.0, den, 1.0)
            psum = psum + p
            oc_ref[heads[r], :] = jnp.dot(p.astype(bf16), vc, preferred_element_type=f32)
        p_hi = psum.astype(bf16)
        p_lo = (psum - p_hi.astype(f32)).astype(bf16)
        ov = ov_ref[:, 0:width]
        imp_ref[...] = _scores(ov, p_hi) + _scores(ov, p_lo)
        gates = g_ref[0, 0]
        for c in range(3 * rep):
            gate_ref[c] = jnp.broadcast_to(gates[:, c:c + 1], (tq, LANES))

    win = functools.partial(
        tile_steps,
        lambda r: jnp.concatenate([q_ref[0, :, h * LANES:(h + 1) * LANES] for h in (r, r + 1)], axis=0),
        kw_ref, vw_ref)
    n_back = WINDOW // tq
    windows = [lambda n=n: win(n * tq, (n + 1) * tq, hi=n * tq) for n in range(n_back)]
    windows.append(lambda: win(WINDOW, WINDOW + tq, lo=0, hi=WINDOW))
    half = ncp // 2
    has_half = half % LANES == 0
    last_half_tile = half // (tq // CMP_STRIDE) - 1 if has_half else -1
    assert not has_half or last_half_tile >= n_back
    for n, window in enumerate(windows):
        this_window = (i == n) if n < n_back else (i >= n_back)
        if not has_half:
            pl.when(this_window)(functools.partial(lambda w: compressed_branch(ncp, w()), window))
        elif n < n_back:
            pl.when(this_window)(functools.partial(lambda w: compressed_branch(half, w()), window))
        else:
            pl.when(this_window & (i <= last_half_tile))(
                functools.partial(lambda w: compressed_branch(half, w()), window))
            pl.when(i > last_half_tile)(functools.partial(lambda w: compressed_branch(ncp, w()), window))

    blk = lax.broadcasted_iota(jnp.int32, (nslc, tq), 0)
    cur = (s0 + lax.broadcasted_iota(jnp.int32, (nslc, tq), 1)) // SLC_BLOCK
    imp = jnp.where(blk > cur, -1.0, imp_ref[...])
    imp = jnp.where((blk == 0) | (blk == cur) | (blk == cur - 1), FORCED_SCORE, imp)
    rank_ref[...] = jnp.zeros((nslc, tq), f32)
    per_tile = tq // SLC_BLOCK
    for first in range(0, nslc, per_tile):

        @pl.when(first <= i * per_tile)
        def _():
            rank = rank_ref[...]
            for jp in range(first, min(first + per_tile, nslc)):
                other = imp[jp:jp + 1, :]
                ahead = jnp.where(blk > jp, jnp.where(other >= imp, 1.0, 0.0), jnp.where(other > imp, 1.0, 0.0))
                rank = rank + ahead
            rank_ref[...] = rank

    bias_t = jnp.where(rank_ref[...] < top_k, 0.0, SEL_BIAS)
    pieces = [jnp.zeros((HEAD_DIM, tq), f32), bias_t]
    if nslc < LANES - HEAD_DIM:
        pieces.append(jnp.zeros((LANES - HEAD_DIM - nslc, tq), f32))
    bias = jnp.concatenate(pieces, axis=0).T.astype(bf16)
    for r in range(rep):
        qa_ref[heads[r], :] = q_ref[0, :, r * LANES:(r + 1) * LANES] + bias

    _softmax_init(m_ref, acc_ref)
    sel = functools.partial(tile_steps, lambda r: qa_ref[r * tq:(r + 2) * tq, :], ks_ref, vs_ref)
    trip = NSA_TRIP_TILES * tk

    def sel_body(j, carry):
        back = s0 - j * trip
        _run_steps(sum((sel(back - t * 2 * tk, 2 * tk) for t in range(NSA_TRIP_TILES // 2)), []), m_ref, acc_ref)
        return carry

    lax.fori_loop(0, s0 // trip, sel_body, 0)
    tail = lax.rem(i, trip // tq)
    for n in range(trip // tq):
        whole = n // 2
        steps = sum((sel(n * tq - t * 2 * tk, 2 * tk) for t in range(whole // 2)), [])
        if whole % 2:
            steps += sel(n * tq - (whole - 1) * tk, tk)
        steps += sel(tq, tk, hi=tq) if n % 2 else sel(0, tq, hi=0)
        pl.when(tail == n)(functools.partial(_run_steps, steps, m_ref, acc_ref))

    o_sel = normalized(acc_ref)
    o_win = normalized(accw_ref)

    lane = lax.broadcasted_iota(jnp.int32, (tq, LANES), 1)
    gated = []
    for r in range(rep):
        gated.append(gate_ref[3 * r] * oc_ref[heads[r], :] + gate_ref[3 * r + 1] * o_sel[heads[r]]
                     + gate_ref[3 * r + 2] * o_win[heads[r]])
    for c in range(rep // 2):
        o_ref[0, :, c * LANES:(c + 1) * LANES] = jnp.where(
            lane < HEAD_DIM, gated[2 * c], pltpu.roll(gated[2 * c + 1], HEAD_DIM, 1)).astype(o_ref.dtype)


def _nsa_attention(qn, kcmp, vcmp, ksa, vsa, kwa, vwa, gates, overlap_t, top_k):
    b, s, _ = qn.shape
    tq = min(NSA_Q_TILE, s)
    ncp = kcmp.shape[2]
    rep = NSA_GQA
    assert WINDOW == 2 * tq and min(NSA_K_TILE, s) == 2 * tq
    cmp_spec = pl.BlockSpec((1, 1, ncp, LANES), lambda bi, g, i: (bi, g, 0, 0))
    seq_spec = pl.BlockSpec((1, 1, s, LANES), lambda bi, g, i: (g, bi, 0, 0))
    rows = rep * tq
    return pl.pallas_call(
        functools.partial(_nsa_kernel, top_k=top_k),
        grid=(b, NSA_KV_GROUPS, s // tq),
        in_specs=[pl.BlockSpec((1, tq, rep * LANES), lambda bi, g, i: (bi, i, g)),
                  cmp_spec, cmp_spec, seq_spec, seq_spec, seq_spec, seq_spec,
                  pl.BlockSpec((1, 1, tq, LANES), lambda bi, g, i: (g, bi, i, 0)),
                  _const_spec(overlap_t.shape)],
        out_specs=pl.BlockSpec((1, tq, rep * HEAD_DIM), lambda bi, g, i: (bi, i, g)),
        out_shape=jax.ShapeDtypeStruct((b, s, NSA_HEADS * HEAD_DIM), bf16),
        scratch_shapes=[pltpu.VMEM((rows, LANES), bf16)] + [pltpu.VMEM((rows, LANES), f32)] * 5 + [
                        pltpu.VMEM(overlap_t.shape[:1] + (tq,), f32),
                        pltpu.VMEM(overlap_t.shape[:1] + (tq,), f32), pltpu.VMEM((3 * rep, tq, LANES), f32)],
        compiler_params=_cparams(("parallel", "parallel", "arbitrary")), name="nsa_attn",
    )(qn, kcmp, vcmp, ksa, vsa, kwa, vwa, gates, overlap_t)


def _outproj_kernel(x_ref, od_ref, on_ref, wa_ref, wb_ref, g_ref, b_ref, o_ref, *, alpha):
    parts = 4
    rows = x_ref.shape[0] // parts

    def mix(r):
        rs = slice(r * rows, (r + 1) * rows)
        return (jnp.dot(od_ref[rs, :], wa_ref[...], preferred_element_type=f32)
                + jnp.dot(on_ref[rs, :], wb_ref[...], preferred_element_type=f32))

    y_next = mix(0)
    for r in range(parts):
        rs = slice(r * rows, (r + 1) * rows)
        y = y_next
        if r + 1 < parts:
            y_next = mix(r + 1)
        o_ref[rs, :] = _layer_norm(alpha * x_ref[rs, :] + y, g_ref[...], b_ref[...])


def _outproj(x2, od, on, wa, wb, g, b, alpha):
    n, d_model = x2.shape
    tm = 2 * DENSE_ROWS
    row_spec = lambda w: pl.BlockSpec((tm, w), lambda i: (i, 0))
    return pl.pallas_call(
        functools.partial(_outproj_kernel, alpha=alpha),
        grid=(n // tm,),
        in_specs=[row_spec(d_model), row_spec(od.shape[1]), row_spec(on.shape[1]),
                  _const_spec(wa.shape), _const_spec(wb.shape), _const_spec(g.shape), _const_spec(b.shape)],
        out_specs=row_spec(d_model),
        out_shape=jax.ShapeDtypeStruct((n, d_model), f32),
        compiler_params=_cparams(("parallel",)), name="outproj_ln",
    )(x2, od, on, wa, wb, g, b)


def _ffn_kernel(h_ref, halo_ref, wu_ref, cw_ref, cb_ref, wd_ref, g_ref, b_ref, o_ref, hb_ref, u_ref, act_ref,
                *, alpha, tiles_per_seq):
    tm = h_ref.shape[0]
    tf = FFN_CHUNK
    n_chunks = wd_ref.shape[0] // tf
    h = h_ref[...]
    first = lax.rem(pl.program_id(0), tiles_per_seq) == 0
    halo = jnp.where(first, 0.0, halo_ref[...])
    d_ff = wd_ref.shape[0]
    hb_ref[...] = jnp.concatenate([halo, h], axis=0).astype(bf16)

    def up(c):
        for part, col in enumerate((c * tf, d_ff + c * tf)):
            u_ref[c % 2, :, part * tf:(part + 1) * tf] = jnp.dot(
                hb_ref[...], wu_ref[:, col:col + tf], preferred_element_type=f32)

    def conv(c, part):
        col = part * d_ff + c * tf
        u_all = u_ref[c % 2, :, part * tf:(part + 1) * tf]
        u = cb_ref[:, col:col + tf]
        for k in range(CONV_WIDTH):
            lag = CONV_WIDTH - 1 - k
            shifted = u_all if lag == 0 else pltpu.roll(u_all, lag, 0)
            u = u + cw_ref[k:k + 1, col:col + tf] * shifted[HALO:, :]
        return u

    up(0)
    for c in range(n_chunks):
        if c + 1 < n_chunks:
            up(c + 1)
        act_ref[:, c * tf:(c + 1) * tf] = (jax.nn.silu(conv(c, 0)) * conv(c, 1)).astype(bf16)
    parts = 2
    rows = tm // parts
    down = lambda r: jnp.dot(act_ref[r * rows:(r + 1) * rows, :], wd_ref[...], preferred_element_type=f32)
    y_next = down(0)
    for r in range(parts):
        rs = slice(r * rows, (r + 1) * rows)
        y = y_next
        if r + 1 < parts:
            y_next = down(r + 1)
        o_ref[rs, :] = _layer_norm(alpha * h_ref[rs, :] + y, g_ref[...], b_ref[...])


def _ffn(h1, wu, cw, cb, wd, g, b, alpha, seq):
    n, d_model = h1.shape
    tm = min(DENSE_ROWS, seq)
    single = dict(pipeline_mode=pl.Buffered(1))
    return pl.pallas_call(
        functools.partial(_ffn_kernel, alpha=alpha, tiles_per_seq=seq // tm),
        grid=(n // tm,),
        in_specs=[pl.BlockSpec((tm, d_model), lambda i: (i, 0)),
                  pl.BlockSpec((HALO, d_model), lambda i: (jnp.maximum(i * (tm // HALO) - 1, 0), 0)),
                  pl.BlockSpec(wu.shape, lambda i: (0, 0), **single),
                  _const_spec(cw.shape), _const_spec(cb.shape),
                  pl.BlockSpec(wd.shape, lambda i: (0, 0), **single),
                  _const_spec(g.shape), _const_spec(b.shape)],
        out_specs=pl.BlockSpec((tm, d_model), lambda i: (i, 0)),
        out_shape=jax.ShapeDtypeStruct((n, d_model), f32),
        scratch_shapes=[pltpu.VMEM((HALO + tm, d_model), bf16), pltpu.VMEM((2, HALO + tm, 2 * FFN_CHUNK), f32),
                        pltpu.VMEM((tm, wd.shape[0]), bf16)],
        compiler_params=_cparams(("parallel",)), name="ffn_ln",
    )(h1, h1, wu, cw, cb, wd, g, b)


def kernel(x, positions, w_in, lambda_q1, lambda_k1, lambda_q2, lambda_k2, diff_norm_g, cmp_pe_k, cmp_w1_k, cmp_b1_k, cmp_w2_k, cmp_pe_v, cmp_w1_v, cmp_b1_v, cmp_w2_v, w_out, ln1_g, ln1_b, w_up, conv_w, conv_b, w_down, ln2_g, ln2_b):
    b, s, d_model = x.shape
    depth = w_in.shape[0]
    n = b * s
    d_ff = w_down.shape[1]
    assert s % ATTN_TILE == 0 or s < ATTN_TILE
    assert s % SLC_BLOCK == 0 and s // SLC_BLOCK <= LANES - HEAD_DIM and d_ff % FFN_CHUNK == 0
    alpha = (2 * depth) ** 0.25
    consts = _rope_consts()
    pos_col = positions.reshape(n, 1)

    n_chunk = s // CMP_STRIDE
    n_cmp = n_chunk - CMP_BLOCK // CMP_STRIDE + 1
    n_slc = s // SLC_BLOCK
    top_k = min(SLC_TOPK, n_slc)
    cmp_pos = positions[:, CMP_BLOCK - 1::CMP_STRIDE][:, :n_cmp]
    cmp_pos = jnp.pad(cmp_pos, ((0, 0), (0, n_chunk - n_cmp))).reshape(b, n_chunk, 1)
    cs = np.arange(n_chunk)[None, :] * CMP_STRIDE
    ss = np.arange(n_slc)[:, None] * SLC_BLOCK
    ov = np.clip(np.minimum(cs + CMP_BLOCK, ss + SLC_BLOCK) - np.maximum(cs, ss), 0, None) / CMP_BLOCK
    ov[:, n_cmp:] = 0.0
    overlap_t = jnp.asarray(ov, dtype=bf16)

    h = x.reshape(n, d_model)
    for l in range(depth):
        lam_init = 0.8 - 0.6 * math.exp(-0.3 * l)
        d_in = w_in.shape[2]
        w_pad = jnp.pad(w_in[l], ((0, 0), (0, -d_in % LANES))).astype(bf16)
        qd, kd, vd, qn, kc, vc, ksa, vsa, kwa, vwa, gates = _proj(h, pos_col, w_pad, consts, s)

        kcmp, vcmp = _compress(
            kc.reshape(b, s, LANES), vc.reshape(b, s, LANES), cmp_pos, consts,
            _compress_weights(cmp_pe_k[l], cmp_w1_k[l], cmp_b1_k[l], cmp_w2_k[l]),
            _compress_weights(cmp_pe_v[l], cmp_w1_v[l], cmp_b1_v[l], cmp_w2_v[l]))

        lam_vec = jnp.stack([lambda_q1[l], lambda_k1[l], lambda_q2[l], lambda_k2[l]]).astype(f32)
        od = _diff_attention(qd.reshape(b, s, -1), kd.reshape(b, s, -1), vd.reshape(b, s, -1),
                             lam_vec, diff_norm_g[l].reshape(1, -1).astype(f32), lam_init)
        grouped = lambda a: a.reshape(NSA_KV_GROUPS, b, s, LANES)
        on = _nsa_attention(qn.reshape(b, s, -1), kcmp, vcmp, grouped(ksa), grouped(vsa), grouped(kwa),
                            grouped(vwa), grouped(gates), overlap_t, top_k)

        d_diff = od.shape[2]
        wo = w_out[l].astype(bf16)
        h = _outproj(h, od.reshape(n, -1), on.reshape(n, -1), wo[:d_diff], wo[d_diff:],
                     ln1_g[l].reshape(1, -1), ln1_b[l].reshape(1, -1), alpha)
        h = _ffn(h, w_up[l].astype(bf16), conv_w[l], conv_b[l].reshape(1, -1),
                 w_down[l].astype(bf16), ln2_g[l].reshape(1, -1), ln2_b[l].reshape(1, -1), alpha, s)
    return h.reshape(b, s, d_model)
```

```python
import functools
import math

import jax
import jax.numpy as jnp
import numpy as np
from jax import lax
from jax.experimental import pallas as pl
from jax.experimental.pallas import tpu as pltpu

f32 = jnp.float32
bf16 = jnp.bfloat16

LANES = 128
HEAD_DIM = 64
ROPE_DIM = HEAD_DIM // 4
ROPE_THETA = 500000.0
DIFF_HEADS = 4
NSA_HEADS = 8
NSA_KV_GROUPS = 2
NSA_GQA = NSA_HEADS // NSA_KV_GROUPS
CMP_BLOCK = 32
CMP_STRIDE = 16
CMP_HIDDEN = 2 * HEAD_DIM
SLC_BLOCK = 64
SLC_TOPK = 16
WINDOW = 512
CONV_WIDTH = 3
LN_EPS = 1e-5
RMS_EPS = 1e-5
NEG_INF = -1e30
SEL_BIAS = NEG_INF
FORCED_SCORE = 1e6
VMEM_LIMIT = 56 * 1024 * 1024

LOG2E = 1.4426950408889634
QK_SCALE = HEAD_DIM ** -0.5 * LOG2E

PROJ_ROWS = 512
ATTN_TILE = 4096
DIFF_K_TILE = 512
NSA_Q_TILE = 256
NSA_K_TILE = 512
NSA_TRIP_TILES = 4
SOFTMAX_ROWS = 256
DENSE_ROWS = 512
FFN_CHUNK = 256
HALO = 8


def _cparams(sem):
    return pltpu.CompilerParams(dimension_semantics=sem, vmem_limit_bytes=VMEM_LIMIT)


def _const_spec(shape):
    n = len(shape)
    return pl.BlockSpec(shape, lambda *_: (0,) * n)


def _layer_norm(y, g, b):
    mu = jnp.mean(y, axis=-1, keepdims=True)
    d = y - mu
    var = jnp.mean(d * d, axis=-1, keepdims=True)
    return d * lax.rsqrt(var + LN_EPS) * g + b


def _rope_consts():
    lane = np.arange(LANES)
    in_head = lane % HEAD_DIM
    half = ROPE_DIM // 2
    inv_freq = 1.0 / (ROPE_THETA ** (jnp.arange(half, dtype=f32) / half))
    c = jnp.zeros((8, LANES), f32)
    c = c.at[0].set(jnp.tile(inv_freq, LANES // half))
    c = c.at[1].set(jnp.asarray(in_head < ROPE_DIM, f32))
    c = c.at[2].set(jnp.asarray(np.where(in_head < half, -1.0, np.where(in_head < ROPE_DIM, 1.0, 0.0)), f32))
    c = c.at[3].set(jnp.asarray(in_head < half, f32))
    return c


def _rope_tables(pos_col, c_ref):
    ang = pos_col.astype(f32) * c_ref[0:1, :]
    cos_t = jnp.where(c_ref[1:2, :] > 0.0, jnp.cos(ang), 1.0)
    sin_t = jnp.sin(ang) * c_ref[2:3, :]
    return cos_t, sin_t, c_ref[3:4, :] > 0.0


def _rope(y, tables):
    cos_t, sin_t, first = tables
    half = ROPE_DIM // 2
    partner = jnp.where(first, pltpu.roll(y, LANES - half, 1), pltpu.roll(y, half, 1))
    return y * cos_t + partner * sin_t


def _proj_kernel(x_ref, pos_ref, w_ref, c_ref,
                 qd_ref, kd_ref, vd_ref, qn_ref, kc_ref, vc_ref, ks_ref, vs_ref, kw_ref, vw_ref, g_ref,
                 *, seq):
    tm = x_ref.shape[0]
    xb = x_ref[...].astype(bf16)
    tables = _rope_tables(pos_ref[...], c_ref)
    lane = lax.broadcasted_iota(jnp.int32, (tm, LANES), 1)
    low = lane < HEAD_DIM
    scale = QK_SCALE

    hd2 = DIFF_HEADS * 2 * HEAD_DIM

    def diff_q(y):
        for c in range(hd2 // LANES):
            sl = slice(c * LANES, (c + 1) * LANES)
            qd_ref[:, sl] = (_rope(y[:, sl], tables) * scale).astype(bf16)

    def diff_k(y):
        for c in range(hd2 // LANES):
            sl = slice(c * LANES, (c + 1) * LANES)
            kd_ref[:, sl] = _rope(y[:, sl], tables).astype(bf16)

    def diff_v(y):
        ones = jnp.ones((tm, LANES), bf16)
        for c in range(hd2 // LANES):
            vd_ref[:, (2 * c) * LANES:(2 * c + 1) * LANES] = y[:, c * LANES:(c + 1) * LANES].astype(bf16)
            vd_ref[:, (2 * c + 1) * LANES:(2 * c + 2) * LANES] = ones

    def nsa_q(y):
        for c in range(NSA_HEADS // 2):
            slab = _rope(y[:, c * LANES:(c + 1) * LANES], tables) * scale
            qn_ref[:, (2 * c) * LANES:(2 * c + 1) * LANES] = jnp.where(low, slab, 0.0).astype(bf16)
            qn_ref[:, (2 * c + 1) * LANES:(2 * c + 2) * LANES] = jnp.where(
                low, pltpu.roll(slab, HEAD_DIM, 1), 0.0).astype(bf16)

    def cmp_kv(y):
        kc_ref[...] = y[:, :LANES]
        vc_ref[...] = y[:, LANES:]

    def grouped(ref, y, fill):
        ref[0] = jnp.where(low, y, fill).astype(ref.dtype)
        ref[1] = jnp.where(low, pltpu.roll(y, HEAD_DIM, 1), fill).astype(ref.dtype)

    def sel_kv(y):
        row = lax.broadcasted_iota(jnp.int32, (tm, LANES), 0) + lax.rem(pl.program_id(0) * tm, seq)
        onehot = jnp.where(lane - HEAD_DIM == row // SLC_BLOCK, 1.0, 0.0)
        grouped(ks_ref, _rope(y[:, :LANES], tables), onehot)
        grouped(vs_ref, y[:, LANES:], 1.0)

    def win_kv(y):
        grouped(kw_ref, _rope(y[:, :LANES], tables), 0.0)
        grouped(vw_ref, y[:, LANES:], 1.0)

    def gate(y):
        sig = jax.nn.sigmoid(y)
        g_ref[0] = sig
        g_ref[1] = pltpu.roll(sig, LANES - NSA_GQA * 3, 1)

    widths = [hd2, hd2, hd2, NSA_HEADS * HEAD_DIM, 2 * LANES, 2 * LANES, 2 * LANES, LANES]
    epilogues = [diff_q, diff_k, diff_v, nsa_q, cmp_kv, sel_kv, win_kv, gate]
    cols = [sum(widths[:n]) for n in range(len(widths))]

    def seg(n):
        return jnp.dot(xb, w_ref[:, cols[n]:cols[n] + widths[n]], preferred_element_type=f32)

    y_next = seg(0)
    for n, epilogue in enumerate(epilogues):
        y = y_next
        if n + 1 < len(epilogues):
            y_next = seg(n + 1)
        epilogue(y)


def _proj(x2, pos_col, w_pad, consts, seq):
    n, d_model = x2.shape
    tm = PROJ_ROWS
    hd2 = DIFF_HEADS * 2 * HEAD_DIM
    row_spec = lambda w: pl.BlockSpec((tm, w), lambda i: (i, 0))
    grp_spec = pl.BlockSpec((NSA_KV_GROUPS, tm, LANES), lambda i: (0, i, 0))
    out_shape = (
        jax.ShapeDtypeStruct((n, hd2), bf16), jax.ShapeDtypeStruct((n, hd2), bf16),
        jax.ShapeDtypeStruct((n, 2 * hd2), bf16), jax.ShapeDtypeStruct((n, NSA_HEADS * LANES), bf16),
        jax.ShapeDtypeStruct((n, LANES), f32), jax.ShapeDtypeStruct((n, LANES), f32),
        jax.ShapeDtypeStruct((NSA_KV_GROUPS, n, LANES), bf16), jax.ShapeDtypeStruct((NSA_KV_GROUPS, n, LANES), bf16),
        jax.ShapeDtypeStruct((NSA_KV_GROUPS, n, LANES), bf16), jax.ShapeDtypeStruct((NSA_KV_GROUPS, n, LANES), bf16),
        jax.ShapeDtypeStruct((NSA_KV_GROUPS, n, LANES), f32),
    )
    out_specs = (row_spec(hd2), row_spec(hd2), row_spec(2 * hd2), row_spec(NSA_HEADS * LANES),
                 row_spec(LANES), row_spec(LANES), grp_spec, grp_spec, grp_spec, grp_spec, grp_spec)
    return pl.pallas_call(
        functools.partial(_proj_kernel, seq=seq),
        grid=(n // tm,),
        in_specs=[row_spec(d_model), pl.BlockSpec((tm, 1), lambda i: (i, 0)),
                  _const_spec(w_pad.shape), _const_spec(consts.shape)],
        out_specs=out_specs, out_shape=out_shape,
        compiler_params=_cparams(("parallel",)), name="proj",
    )(x2, pos_col, w_pad, consts)


def _compress_one(c_ref, pea_ref, peb_ref, w1a_ref, w1b_ref, b1_ref, w2_ref):
    n = c_ref.shape[1] // CMP_STRIDE
    ck = jnp.concatenate([c_ref[0, pl.ds(t, n, stride=CMP_STRIDE), :] for t in range(CMP_STRIDE)], axis=1)
    hid_a = jnp.dot((ck + pea_ref[...]).astype(bf16), w1a_ref[...], preferred_element_type=f32)
    hid_b = jnp.dot((ck + peb_ref[...]).astype(bf16), w1b_ref[...], preferred_element_type=f32)
    hid = hid_a + pltpu.roll(hid_b, n - 1, 0) + b1_ref[...]
    hid = jax.nn.gelu(hid)
    return jnp.dot(hid.astype(bf16), w2_ref[...], preferred_element_type=f32)


def _compress_kernel(kc_ref, vc_ref, pos_ref, c_ref,
                     pak_ref, pbk_ref, w1ak_ref, w1bk_ref, b1k_ref, w2k_ref,
                     pav_ref, pbv_ref, w1av_ref, w1bv_ref, b1v_ref, w2v_ref,
                     kcmp_ref, vcmp_ref):
    k = _compress_one(kc_ref, pak_ref, pbk_ref, w1ak_ref, w1bk_ref, b1k_ref, w2k_ref)
    k = _rope(k, _rope_tables(pos_ref[0], c_ref))
    low = lax.broadcasted_iota(jnp.int32, k.shape, 1) < HEAD_DIM
    kcmp_ref[0, 0] = jnp.where(low, k, 0.0).astype(bf16)
    kcmp_ref[0, 1] = jnp.where(low, pltpu.roll(k, HEAD_DIM, 1), 0.0).astype(bf16)
    v = _compress_one(vc_ref, pav_ref, pbv_ref, w1av_ref, w1bv_ref, b1v_ref, w2v_ref)
    vcmp_ref[0, 0] = jnp.where(low, v, 0.0).astype(bf16)
    vcmp_ref[0, 1] = jnp.where(low, pltpu.roll(v, HEAD_DIM, 1), 0.0).astype(bf16)


def _compress_weights(pe, w1, b1, w2):
    r = CMP_BLOCK // CMP_STRIDE
    assert r == 2
    eye = jnp.eye(NSA_KV_GROUPS, dtype=f32)
    w1r = w1.reshape(CMP_BLOCK, HEAD_DIM, CMP_HIDDEN)
    per = CMP_STRIDE * NSA_KV_GROUPS * HEAD_DIM

    def big(part):
        return jnp.einsum('tcm,gh->tgchm', part, eye).reshape(per, NSA_KV_GROUPS * CMP_HIDDEN).astype(bf16)

    def pe_row(part):
        return jnp.broadcast_to(part[:, None, :], (CMP_STRIDE, NSA_KV_GROUPS, HEAD_DIM)).reshape(1, per)

    w2b = jnp.einsum('mc,gh->gmhc', w2, eye).reshape(NSA_KV_GROUPS * CMP_HIDDEN, NSA_KV_GROUPS * HEAD_DIM)
    return (pe_row(pe[:CMP_STRIDE]), pe_row(pe[CMP_STRIDE:]), big(w1r[:CMP_STRIDE]), big(w1r[CMP_STRIDE:]),
            jnp.tile(b1, NSA_KV_GROUPS).reshape(1, -1), w2b.astype(bf16))


def _compress(kc, vc, cmp_pos, consts, wk, wv):
    b, s, width = kc.shape
    nch = s // CMP_STRIDE
    seq_spec = pl.BlockSpec((1, s, width), lambda i: (i, 0, 0))
    w_specs = [_const_spec(w.shape) for w in wk + wv]
    return pl.pallas_call(
        _compress_kernel,
        grid=(b,),
        in_specs=[seq_spec, seq_spec, pl.BlockSpec((1, nch, 1), lambda i: (i, 0, 0)),
                  _const_spec(consts.shape)] + w_specs,
        out_specs=(pl.BlockSpec((1, NSA_KV_GROUPS, nch, LANES), lambda i: (i, 0, 0, 0)),
                   pl.BlockSpec((1, NSA_KV_GROUPS, nch, LANES), lambda i: (i, 0, 0, 0))),
        out_shape=(jax.ShapeDtypeStruct((b, NSA_KV_GROUPS, nch, LANES), bf16),
                   jax.ShapeDtypeStruct((b, NSA_KV_GROUPS, nch, LANES), bf16)),
        compiler_params=_cparams(("parallel",)), name="compress",
    )(kc, vc, cmp_pos, consts, *wk, *wv)


def _softmax_init(m_ref, acc_ref):
    m_ref[...] = jnp.full(m_ref.shape, NEG_INF, f32)
    acc_ref[...] = jnp.zeros(acc_ref.shape, f32)


def _softmax_step(s, v_ones, m_ref, acc_ref, rs):
    m_prev = m_ref[rs, :]
    m_new = jnp.maximum(m_prev, jnp.max(s, axis=1, keepdims=True))
    alpha = jnp.exp2(m_prev - m_new)
    p = jnp.exp2((s - jnp.concatenate([m_new] * (s.shape[1] // LANES), axis=1)).astype(bf16))
    pv = jnp.dot(p, v_ones, preferred_element_type=f32)
    acc_ref[rs, :] = jnp.concatenate([alpha] * (acc_ref.shape[1] // LANES), axis=1) * acc_ref[rs, :] + pv
    m_ref[rs, :] = m_new


def _scores(q, k):
    return lax.dot_general(q, k, (((1,), (1,)), ((), ())), preferred_element_type=f32)


def _run_steps(steps, m_ref, acc_ref):
    s_next = steps[0][0]()
    for n, (_, v_ones_fn, rows) in enumerate(steps):
        s = s_next
        if n + 1 < len(steps):
            s_next = steps[n + 1][0]()
        _softmax_step(s, v_ones_fn(), m_ref, acc_ref, rows)


def _diff_kernel(q_ref, k_ref, v_ref, lam_ref, g_ref, o_ref, q2_ref, m_ref, acc_ref, *, lam_init):
    tq = q_ref.shape[1]
    tk = min(DIFF_K_TILE, tq)
    per_q = tq // tk
    rc = min(SOFTMAX_ROWS, tq)
    vdim = o_ref.shape[2]
    i = pl.program_id(2)
    q = q_ref[0]
    low = lax.broadcasted_iota(jnp.int32, q.shape, 1) < HEAD_DIM
    zero = jnp.zeros_like(q)
    q2_ref[0:tq, :] = jnp.where(low, q, zero)
    q2_ref[tq:2 * tq, :] = jnp.where(low, zero, q)
    _softmax_init(m_ref, acc_ref)

    def tile_steps(j, k_off=None):
        off = pl.multiple_of(j * tk, tk)
        steps = []

        def add(row0, rows):
            q_off = row0 % tq
            cols = tk if k_off is None else min(tk, q_off + rows - k_off)
            if cols <= 0:
                return
            masked = k_off is not None and q_off - k_off + 1 < cols

            def score():
                s = _scores(q2_ref[row0:row0 + rows, :], k_ref[0, pl.ds(off, cols), :])
                if masked:
                    r = q_off + lax.broadcasted_iota(jnp.int32, (rows, cols), 0)
                    s = jnp.where(r >= k_off + lax.broadcasted_iota(jnp.int32, (rows, cols), 1), s, NEG_INF)
                return s

            steps.append((score, lambda: v_ref[0, pl.ds(off, cols), :], slice(row0, row0 + rows)))

        big = min(2 * rc, tq)
        for row0 in range(0, 2 * tq, big):
            if k_off is None or row0 % tq - k_off + 1 >= tk:
                add(row0, big)
            else:
                for sub in range(row0, row0 + big, rc):
                    add(sub, rc)
        return steps

    group = 2 if per_q % 2 == 0 else 1

    def body(j, carry):
        _run_steps(sum((tile_steps(group * j + t) for t in range(group)), []), m_ref, acc_ref)
        return carry

    lax.fori_loop(0, i * (per_q // group), body, 0)
    _run_steps(sum((tile_steps(i * per_q + t, t * tk) for t in range(per_q)), []), m_ref, acc_ref)

    lam_v = lam_ref[...]
    lam = (jnp.exp(jnp.sum(lam_v[0:1] * lam_v[1:2], axis=1, keepdims=True))
           - jnp.exp(jnp.sum(lam_v[2:3] * lam_v[3:4], axis=1, keepdims=True)) + lam_init)
    o = acc_ref[:, :vdim] / acc_ref[:, vdim:]
    od = o[:tq] - lam * o[tq:]
    od = od * lax.rsqrt(jnp.mean(od * od, axis=-1, keepdims=True) + RMS_EPS)
    o_ref[0] = (od * g_ref[...] * (1.0 - lam_init)).astype(o_ref.dtype)


def _diff_attention(qd, kd, vd, lam_vec, diff_g, lam_init):
    b, s, _ = qd.shape
    tq = min(ATTN_TILE, s)
    vdim = vd.shape[2] // DIFF_HEADS // 2
    assert vdim == LANES
    return pl.pallas_call(
        functools.partial(_diff_kernel, lam_init=lam_init),
        grid=(b, DIFF_HEADS, s // tq),
        in_specs=[pl.BlockSpec((1, tq, LANES), lambda bi, h, i: (bi, i, h)),
                  pl.BlockSpec((1, s, LANES), lambda bi, h, i: (bi, 0, h)),
                  pl.BlockSpec((1, s, 2 * vdim), lambda bi, h, i: (bi, 0, h)),
                  _const_spec(lam_vec.shape), _const_spec(diff_g.shape)],
        out_specs=pl.BlockSpec((1, tq, vdim), lambda bi, h, i: (bi, i, h)),
        out_shape=jax.ShapeDtypeStruct((b, s, DIFF_HEADS * vdim), bf16),
        scratch_shapes=[pltpu.VMEM((2 * tq, LANES), bf16), pltpu.VMEM((2 * tq, LANES), f32),
                        pltpu.VMEM((2 * tq, 2 * vdim), f32)],
        compiler_params=_cparams(("parallel", "parallel", "arbitrary")), name="diff_attn",
    )(qd, kd, vd, lam_vec, diff_g)


def _nsa_kernel(q_ref, kc_ref, vc_ref, ks_ref, vs_ref, kw_ref, vw_ref, g_ref, ov_ref, o_ref,
                qa_ref, m_ref, acc_ref, mw_ref, accw_ref, oc_ref, imp_ref, rank_ref, gate_ref, *, top_k, tq):
    seq = ks_ref.shape[2]
    tk = min(NSA_K_TILE, seq)
    ncp = kc_ref.shape[2]
    nslc = ov_ref.shape[0]
    rep = NSA_GQA
    i = pl.program_id(2)
    s0 = i * tq
    tile = pl.ds(pl.multiple_of(s0, tq), tq)
    heads = [slice(r * tq, (r + 1) * tq) for r in range(rep)]

    def tile_steps(pair_q, k_ref, v_ref, back, width, lo=None, hi=None):
        off = pl.multiple_of(s0 - back, tq)
        steps = []
        for r in range(0, rep, 2):

            def score(r=r):
                s = _scores(pair_q(r), k_ref[0, 0, pl.ds(off, width), :])
                if lo is None and hi is None:
                    return s
                d = (lax.broadcasted_iota(jnp.int32, (2 * tq, width), 1)
                     - lax.rem(lax.broadcasted_iota(jnp.int32, (2 * tq, width), 0), tq))
                if hi is not None:
                    s = jnp.where(d <= hi, s, NEG_INF)
                if lo is not None:
                    s = jnp.where(d > lo, s, NEG_INF)
                return s

            steps.append((score, lambda: v_ref[0, 0, pl.ds(off, width), :], slice(r * tq, (r + 2) * tq)))
        return steps

    def normalized(ref):
        acc = ref[...]
        low = lax.broadcasted_iota(jnp.int32, acc.shape, 1) < HEAD_DIM
        return acc / jnp.where(low, pltpu.roll(acc, HEAD_DIM, 1), 1.0)

    def compressed_branch(width, window_steps):
        t_col = s0 + lax.broadcasted_iota(jnp.int32, (tq, width), 0)
        cmp_end = lax.broadcasted_iota(jnp.int32, (tq, width), 1) * CMP_STRIDE + (CMP_BLOCK - 1)
        cmp_ok = cmp_end <= t_col
        kc = kc_ref[0, 0, 0:width, :]
        vc = vc_ref[0, 0, 0:width, :]
        raw = [_scores(q_ref[0, tile,r * LANES:(r + 1) * LANES], kc) for r in range(rep)]
        _softmax_init(mw_ref, accw_ref)
        _run_steps(window_steps, mw_ref, accw_ref)
        psum = jnp.zeros((tq, width), f32)
        for r in range(rep):
            s = jnp.where(cmp_ok, raw[r], NEG_INF)
            p = jnp.where(cmp_ok, jnp.exp2(s - jnp.max(s, axis=1, keepdims=True)), 0.0)
            den = jnp.sum(p, axis=1, keepdims=True)
            p = p / jnp.where(den > 0.0, den, 1.0)
            psum = psum + p
            oc_ref[heads[r], :] = jnp.dot(p.astype(bf16), vc, preferred_element_type=f32)
        p_hi = psum.astype(bf16)
        p_lo = (psum - p_hi.astype(f32)).astype(bf16)
        ov = ov_ref[:, 0:width]
        imp_ref[...] = _scores(ov, p_hi) + _scores(ov, p_lo)
        gates = g_ref[0, 0, tile, :]
        for c in range(3 * rep):
            gate_ref[c] = jnp.broadcast_to(gates[:, c:c + 1], (tq, LANES))

    win = functools.partial(
        tile_steps,
        lambda r: jnp.concatenate([q_ref[0, tile,h * LANES:(h + 1) * LANES] for h in (r, r + 1)], axis=0),
        kw_ref, vw_ref)
    n_back = WINDOW // tq
    windows = [lambda n=n: win(n * tq, (n + 1) * tq, hi=n * tq) for n in range(n_back)]
    windows.append(lambda: win(WINDOW, WINDOW + tq, lo=0, hi=WINDOW))
    half = ncp // 2
    has_half = half % LANES == 0
    last_half_tile = half // (tq // CMP_STRIDE) - 1 if has_half else -1
    assert not has_half or last_half_tile >= n_back
    for n, window in enumerate(windows):
        this_window = (i == n) if n < n_back else (i >= n_back)
        if not has_half:
            pl.when(this_window)(functools.partial(lambda w: compressed_branch(ncp, w()), window))
        elif n < n_back:
            pl.when(this_window)(functools.partial(lambda w: compressed_branch(half, w()), window))
        else:
            pl.when(this_window & (i <= last_half_tile))(
                functools.partial(lambda w: compressed_branch(half, w()), window))
            pl.when(i > last_half_tile)(functools.partial(lambda w: compressed_branch(ncp, w()), window))

    blk = lax.broadcasted_iota(jnp.int32, (nslc, tq), 0)
    cur = (s0 + lax.broadcasted_iota(jnp.int32, (nslc, tq), 1)) // SLC_BLOCK
    imp = jnp.where(blk > cur, -1.0, imp_ref[...])
    imp = jnp.where((blk == 0) | (blk == cur) | (blk == cur - 1), FORCED_SCORE, imp)
    rank_ref[...] = jnp.zeros((nslc, tq), f32)
    per_tile = tq // SLC_BLOCK
    for first in range(0, nslc, per_tile):

        @pl.when(first <= i * per_tile)
        def _():
            rank = rank_ref[...]
            for jp in range(first, min(first + per_tile, nslc)):
                other = imp[jp:jp + 1, :]
                ahead = jnp.where(blk > jp, jnp.where(other >= imp, 1.0, 0.0), jnp.where(other > imp, 1.0, 0.0))
                rank = rank + ahead
            rank_ref[...] = rank

    bias_t = jnp.where(rank_ref[...] < top_k, 0.0, SEL_BIAS)
    pieces = [jnp.zeros((HEAD_DIM, tq), f32), bias_t]
    if nslc < LANES - HEAD_DIM:
        pieces.append(jnp.zeros((LANES - HEAD_DIM - nslc, tq), f32))
    bias = jnp.concatenate(pieces, axis=0).T.astype(bf16)
    for r in range(rep):
        qa_ref[heads[r], :] = q_ref[0, tile,r * LANES:(r + 1) * LANES] + bias

    _softmax_init(m_ref, acc_ref)
    sel = functools.partial(tile_steps, lambda r: qa_ref[r * tq:(r + 2) * tq, :], ks_ref, vs_ref)
    trip = NSA_TRIP_TILES * tk

    def sel_body(j, carry):
        back = s0 - j * trip
        _run_steps(sum((sel(back - t * 2 * tk, 2 * tk) for t in range(NSA_TRIP_TILES // 2)), []), m_ref, acc_ref)
        return carry

    lax.fori_loop(0, s0 // trip, sel_body, 0)
    tail = lax.rem(i, trip // tq)
    for n in range(trip // tq):
        whole = n // 2
        steps = sum((sel(n * tq - t * 2 * tk, 2 * tk) for t in range(whole // 2)), [])
        if whole % 2:
            steps += sel(n * tq - (whole - 1) * tk, tk)
        steps += sel(tq, tk, hi=tq) if n % 2 else sel(0, tq, hi=0)
        pl.when(tail == n)(functools.partial(_run_steps, steps, m_ref, acc_ref))

    o_sel = normalized(acc_ref)
    o_win = normalized(accw_ref)

    lane = lax.broadcasted_iota(jnp.int32, (tq, LANES), 1)
    gated = []
    for r in range(rep):
        gated.append(gate_ref[3 * r] * oc_ref[heads[r], :] + gate_ref[3 * r + 1] * o_sel[heads[r]]
                     + gate_ref[3 * r + 2] * o_win[heads[r]])
    for c in range(rep // 2):
        o_ref[0, tile, c * LANES:(c + 1) * LANES] = jnp.where(
            lane < HEAD_DIM, gated[2 * c], pltpu.roll(gated[2 * c + 1], HEAD_DIM, 1)).astype(o_ref.dtype)


def _nsa_attention(qn, kcmp, vcmp, ksa, vsa, kwa, vwa, gates, overlap_t, top_k):
    b, s, _ = qn.shape
    tq = min(NSA_Q_TILE, s)
    ncp = kcmp.shape[2]
    rep = NSA_GQA
    assert WINDOW == 2 * tq and min(NSA_K_TILE, s) == 2 * tq
    cmp_spec = pl.BlockSpec((1, 1, ncp, LANES), lambda bi, g, i: (bi, g, 0, 0))
    seq_spec = pl.BlockSpec((1, 1, s, LANES), lambda bi, g, i: (g, bi, 0, 0))
    rows = rep * tq
    return pl.pallas_call(
        functools.partial(_nsa_kernel, top_k=top_k, tq=tq),
        grid=(b, NSA_KV_GROUPS, s // tq),
        in_specs=[pl.BlockSpec((1, s, rep * LANES), lambda bi, g, i: (bi, 0, g)),
                  cmp_spec, cmp_spec, seq_spec, seq_spec, seq_spec, seq_spec, seq_spec,
                  _const_spec(overlap_t.shape)],
        out_specs=pl.BlockSpec((1, s, rep * HEAD_DIM), lambda bi, g, i: (bi, 0, g)),
        out_shape=jax.ShapeDtypeStruct((b, s, NSA_HEADS * HEAD_DIM), bf16),
        scratch_shapes=[pltpu.VMEM((rows, LANES), bf16)] + [pltpu.VMEM((rows, LANES), f32)] * 5 + [
                        pltpu.VMEM(overlap_t.shape[:1] + (tq,), f32),
                        pltpu.VMEM(overlap_t.shape[:1] + (tq,), f32), pltpu.VMEM((3 * rep, tq, LANES), f32)],
        compiler_params=_cparams(("parallel", "parallel", "arbitrary")), name="nsa_attn",
    )(qn, kcmp, vcmp, ksa, vsa, kwa, vwa, gates, overlap_t)


def _outproj_kernel(x_ref, od_ref, on_ref, wa_ref, wb_ref, g_ref, b_ref, o_ref, *, alpha):
    parts = 4
    rows = x_ref.shape[0] // parts

    def mix(r):
        rs = slice(r * rows, (r + 1) * rows)
        return (jnp.dot(od_ref[rs, :], wa_ref[...], preferred_element_type=f32)
                + jnp.dot(on_ref[rs, :], wb_ref[...], preferred_element_type=f32))

    y_next = mix(0)
    for r in range(parts):
        rs = slice(r * rows, (r + 1) * rows)
        y = y_next
        if r + 1 < parts:
            y_next = mix(r + 1)
        o_ref[rs, :] = _layer_norm(alpha * x_ref[rs, :] + y, g_ref[...], b_ref[...])


def _outproj(x2, od, on, wa, wb, g, b, alpha):
    n, d_model = x2.shape
    tm = 2 * DENSE_ROWS
    row_spec = lambda w: pl.BlockSpec((tm, w), lambda i: (i, 0))
    return pl.pallas_call(
        functools.partial(_outproj_kernel, alpha=alpha),
        grid=(n // tm,),
        in_specs=[row_spec(d_model), row_spec(od.shape[1]), row_spec(on.shape[1]),
                  _const_spec(wa.shape), _const_spec(wb.shape), _const_spec(g.shape), _const_spec(b.shape)],
        out_specs=row_spec(d_model),
        out_shape=jax.ShapeDtypeStruct((n, d_model), f32),
        compiler_params=_cparams(("parallel",)), name="outproj_ln",
    )(x2, od, on, wa, wb, g, b)


def _ffn_kernel(h_ref, halo_ref, wu_ref, cw_ref, cb_ref, wd_ref, g_ref, b_ref, o_ref, hb_ref, u_ref, act_ref,
                *, alpha, tiles_per_seq):
    tm = h_ref.shape[0]
    tf = FFN_CHUNK
    n_chunks = wd_ref.shape[0] // tf
    h = h_ref[...]
    first = lax.rem(pl.program_id(0), tiles_per_seq) == 0
    halo = jnp.where(first, 0.0, halo_ref[...])
    d_ff = wd_ref.shape[0]
    hb_ref[...] = jnp.concatenate([halo, h], axis=0).astype(bf16)

    def up(c):
        for part, col in enumerate((c * tf, d_ff + c * tf)):
            u_ref[c % 2, :, part * tf:(part + 1) * tf] = jnp.dot(
                hb_ref[...], wu_ref[:, col:col + tf], preferred_element_type=f32)

    def conv(c, part):
        col = part * d_ff + c * tf
        u_all = u_ref[c % 2, :, part * tf:(part + 1) * tf]
        u = cb_ref[:, col:col + tf]
        for k in range(CONV_WIDTH):
            lag = CONV_WIDTH - 1 - k
            shifted = u_all if lag == 0 else pltpu.roll(u_all, lag, 0)
            u = u + cw_ref[k:k + 1, col:col + tf] * shifted[HALO:, :]
        return u

    up(0)
    for c in range(n_chunks):
        if c + 1 < n_chunks:
            up(c + 1)
        act_ref[:, c * tf:(c + 1) * tf] = (jax.nn.silu(conv(c, 0)) * conv(c, 1)).astype(bf16)
    parts = 2
    rows = tm // parts
    down = lambda r: jnp.dot(act_ref[r * rows:(r + 1) * rows, :], wd_ref[...], preferred_element_type=f32)
    y_next = down(0)
    for r in range(parts):
        rs = slice(r * rows, (r + 1) * rows)
        y = y_next
        if r + 1 < parts:
            y_next = down(r + 1)
        o_ref[rs, :] = _layer_norm(alpha * h_ref[rs, :] + y, g_ref[...], b_ref[...])


def _ffn(h1, wu, cw, cb, wd, g, b, alpha, seq):
    n, d_model = h1.shape
    tm = min(DENSE_ROWS, seq)
    single = dict(pipeline_mode=pl.Buffered(1))
    return pl.pallas_call(
        functools.partial(_ffn_kernel, alpha=alpha, tiles_per_seq=seq // tm),
        grid=(n // tm,),
        in_specs=[pl.BlockSpec((tm, d_model), lambda i: (i, 0)),
                  pl.BlockSpec((HALO, d_model), lambda i: (jnp.maximum(i * (tm // HALO) - 1, 0), 0)),
                  pl.BlockSpec(wu.shape, lambda i: (0, 0), **single),
                  _const_spec(cw.shape), _const_spec(cb.shape),
                  pl.BlockSpec(wd.shape, lambda i: (0, 0), **single),
                  _const_spec(g.shape), _const_spec(b.shape)],
        out_specs=pl.BlockSpec((tm, d_model), lambda i: (i, 0)),
        out_shape=jax.ShapeDtypeStruct((n, d_model), f32),
        scratch_shapes=[pltpu.VMEM((HALO + tm, d_model), bf16), pltpu.VMEM((2, HALO + tm, 2 * FFN_CHUNK), f32),
                        pltpu.VMEM((tm, wd.shape[0]), bf16)],
        compiler_params=_cparams(("parallel",)), name="ffn_ln",
    )(h1, h1, wu, cw, cb, wd, g, b)


def kernel(x, positions, w_in, lambda_q1, lambda_k1, lambda_q2, lambda_k2, diff_norm_g, cmp_pe_k, cmp_w1_k, cmp_b1_k, cmp_w2_k, cmp_pe_v, cmp_w1_v, cmp_b1_v, cmp_w2_v, w_out, ln1_g, ln1_b, w_up, conv_w, conv_b, w_down, ln2_g, ln2_b):
    b, s, d_model = x.shape
    depth = w_in.shape[0]
    n = b * s
    d_ff = w_down.shape[1]
    assert s % ATTN_TILE == 0 or s < ATTN_TILE
    assert s % SLC_BLOCK == 0 and s // SLC_BLOCK <= LANES - HEAD_DIM and d_ff % FFN_CHUNK == 0
    alpha = (2 * depth) ** 0.25
    consts = _rope_consts()
    pos_col = positions.reshape(n, 1)

    n_chunk = s // CMP_STRIDE
    n_cmp = n_chunk - CMP_BLOCK // CMP_STRIDE + 1
    n_slc = s // SLC_BLOCK
    top_k = min(SLC_TOPK, n_slc)
    cmp_pos = positions[:, CMP_BLOCK - 1::CMP_STRIDE][:, :n_cmp]
    cmp_pos = jnp.pad(cmp_pos, ((0, 0), (0, n_chunk - n_cmp))).reshape(b, n_chunk, 1)
    cs = np.arange(n_chunk)[None, :] * CMP_STRIDE
    ss = np.arange(n_slc)[:, None] * SLC_BLOCK
    ov = np.clip(np.minimum(cs + CMP_BLOCK, ss + SLC_BLOCK) - np.maximum(cs, ss), 0, None) / CMP_BLOCK
    ov[:, n_cmp:] = 0.0
    overlap_t = jnp.asarray(ov, dtype=bf16)

    h = x.reshape(n, d_model)
    for l in range(depth):
        lam_init = 0.8 - 0.6 * math.exp(-0.3 * l)
        d_in = w_in.shape[2]
        w_pad = jnp.pad(w_in[l], ((0, 0), (0, -d_in % LANES))).astype(bf16)
        qd, kd, vd, qn, kc, vc, ksa, vsa, kwa, vwa, gates = _proj(h, pos_col, w_pad, consts, s)

        kcmp, vcmp = _compress(
            kc.reshape(b, s, LANES), vc.reshape(b, s, LANES), cmp_pos, consts,
            _compress_weights(cmp_pe_k[l], cmp_w1_k[l], cmp_b1_k[l], cmp_w2_k[l]),
            _compress_weights(cmp_pe_v[l], cmp_w1_v[l], cmp_b1_v[l], cmp_w2_v[l]))

        lam_vec = jnp.stack([lambda_q1[l], lambda_k1[l], lambda_q2[l], lambda_k2[l]]).astype(f32)
        od = _diff_attention(qd.reshape(b, s, -1), kd.reshape(b, s, -1), vd.reshape(b, s, -1),
                             lam_vec, diff_norm_g[l].reshape(1, -1).astype(f32), lam_init)
        grouped = lambda a: a.reshape(NSA_KV_GROUPS, b, s, LANES)
        on = _nsa_attention(qn.reshape(b, s, -1), kcmp, vcmp, grouped(ksa), grouped(vsa), grouped(kwa),
                            grouped(vwa), grouped(gates), overlap_t, top_k)

        d_diff = od.shape[2]
        wo = w_out[l].astype(bf16)
        h = _outproj(h, od.reshape(n, -1), on.reshape(n, -1), wo[:d_diff], wo[d_diff:],
                     ln1_g[l].reshape(1, -1), ln1_b[l].reshape(1, -1), alpha)
        h = _ffn(h, w_up[l].astype(bf16), conv_w[l], conv_b[l].reshape(1, -1),
                 w_down[l].astype(bf16), ln2_g[l].reshape(1, -1), ln2_b[l].reshape(1, -1), alpha, s)
    return h.reshape(b, s, d_model)
```
